```python
import math
import jax, jax.numpy as jnp
from jax import lax
import numpy as np

D_MODEL = 2048
BATCH = 4
SEQ = 2048
DEPTH = 4
DEC_BATCH = 32
DEC_SEQ = 32
PAST_LEN = 4096

CHUNK = 64
D_MIX = D_MODEL
W_GROUP = D_MIX // 4
W_A = W_GROUP
P_A = 64
H_A = W_A // P_A
N_A = 128
G_A = 2
K_A = 4
CONV_DIM_A = W_A + 2 * G_A * N_A
W_B = W_GROUP
D_B = 64
H_B = W_B // D_B
Q_BLOCK = 128
W_C = W_GROUP
H_C = 4
DH_C = W_C // H_C
K_C = 4
W_D = D_MIX - W_A - W_B - W_C
K_D = 31
SPLIT_SIZES = (W_A, CONV_DIM_A, H_A, W_B, W_B, W_B, W_B, W_C, W_C, H_C, H_C, W_D, W_D, W_D)
IN_COLS = sum(SPLIT_SIZES)
ALPHA = (2 * DEPTH) ** 0.25
BETA = (8 * DEPTH) ** -0.25
EPS = 1e-5

kernel_name = "hybrid_streaming_encoder_step"


def layer_norm(x):
    x32 = x.astype(jnp.float32)
    mu = jnp.mean(x32, -1, keepdims=True)
    var = jnp.mean(jnp.square(x32 - mu), -1, keepdims=True)
    return (x32 - mu) * lax.rsqrt(var + EPS)


def rms_norm(x, w):
    x32 = x.astype(jnp.float32)
    return x32 * lax.rsqrt(jnp.mean(x32 * x32, -1, keepdims=True) + EPS) * w


def chunk_len(L):
    return CHUNK if L % CHUNK == 0 else L


def split_cols(t):
    parts = []
    start = 0
    for s in SPLIT_SIZES:
        parts.append(t[..., start:start + s])
        start += s
    return parts


def causal_dwconv(x, buf, w, b):
    xp = jnp.concatenate([buf.astype(x.dtype), x], axis=1)
    y = lax.conv_general_dilated(xp, w[:, None, :].astype(x.dtype), window_strides=(1,), padding='VALID',
                                 dimension_numbers=('NWC', 'WIO', 'NWC'), feature_group_count=x.shape[-1])
    return y + b, xp[:, xp.shape[1] - (w.shape[0] - 1):]


def to_chunks(t, nc, q):
    return jnp.moveaxis(t.reshape((t.shape[0], nc, q) + t.shape[2:]), 1, 0)


def ssd_scan(x, dt, a, bm, cm, h0, q):
    bsz, L, H, P = x.shape
    nc = L // q
    causal = jnp.tril(jnp.ones((q, q), bool))[None, :, :, None]

    def step(h, inp):
        xc, dtc, bc, cc = inp
        acum = jnp.cumsum(dtc * a, axis=1)
        seg = acum[:, :, None, :] - acum[:, None, :, :]
        decay = jnp.exp(jnp.where(causal, seg, -jnp.inf))
        xdt = xc * dtc[..., None]
        y = jnp.einsum('bthn,bshn,btsh,bshp->bthp', cc, bc, decay, xdt)
        y = y + jnp.einsum('bthn,bhpn,bth->bthp', cc, h, jnp.exp(acum))
        last = acum[:, -1]
        w_s = jnp.exp(last[:, None] - acum)
        h_new = jnp.exp(last)[..., None, None] * h + jnp.einsum('bshn,bsh,bshp->bhpn', bc, w_s, xdt)
        return h_new, y

    xs = (to_chunks(x, nc, q), to_chunks(dt, nc, q), to_chunks(bm, nc, q), to_chunks(cm, nc, q))
    h_last, ys = lax.scan(step, h0.astype(jnp.float32), xs)
    return jnp.moveaxis(ys, 0, 1).reshape(bsz, L, H, P), h_last


def stick_breaking_attention(q, k, v, q_pos, k_pos):
    bsz, Lq, H, D = q.shape
    qb = Q_BLOCK if Lq % Q_BLOCK == 0 else Lq
    nb = Lq // qb
    qs = jnp.moveaxis(q.reshape(bsz, nb, qb, H, D), 1, 0)
    ps = q_pos.reshape(nb, qb)
    scale = D ** -0.5
    v32 = v.astype(jnp.float32)

    def block(args):
        qblk, pblk = args
        z = jnp.einsum('bqhd,bkhd->bhqk', qblk, k).astype(jnp.float32) * scale
        mask = (k_pos[None, :] < pblk[:, None])[None, None]
        log_keep = jnp.where(mask, jax.nn.log_sigmoid(-z), 0.0)
        suffix = lax.cumsum(log_keep, axis=3, reverse=True) - log_keep
        w = jnp.where(mask, jnp.exp(jax.nn.log_sigmoid(z) + suffix), 0.0)
        return jnp.einsum('bhqk,bkhd->bqhd', w, v32)

    out = lax.map(block, (qs, ps))
    return jnp.moveaxis(out, 0, 1).reshape(bsz, Lq, H, D)


def mlstm_scan(q, k, v, ipre, logf, c0, n0, m0, chunk):
    bsz, L, H, Dk = q.shape
    nc = L // chunk
    causal = jnp.tril(jnp.ones((chunk, chunk), bool))[None, :, :, None]

    def step(carry, inp):
        cm, nm, mm = carry
        qc, kc, vc, ic, fc = inp
        b = jnp.cumsum(fc, axis=1)
        d = jnp.where(causal, b[:, :, None] - b[:, None] + ic[:, None], -jnp.inf)
        inter = b + mm[:, None]
        m_t = jnp.maximum(inter, jnp.max(d, axis=2))
        w = jnp.exp(d - m_t[:, :, None]) * jnp.einsum('bthd,bshd->btsh', qc, kc)
        g = jnp.exp(inter - m_t)
        num = jnp.einsum('btsh,bshv->bthv', w, vc) + g[..., None] * jnp.einsum('bhvd,bthd->bthv', cm, qc)
        nq = jnp.sum(w, axis=2) + g * jnp.einsum('bhd,bthd->bth', nm, qc)
        h = num / jnp.maximum(jnp.abs(nq), jnp.exp(-m_t))[..., None]
        m_new = m_t[:, -1]
        ws = jnp.exp(b[:, -1:] - b + ic - m_new[:, None])
        g_last = jnp.exp(b[:, -1] + mm - m_new)
        c_new = g_last[..., None, None] * cm + jnp.einsum('bsh,bshv,bshd->bhvd', ws, vc, kc)
        n_new = g_last[..., None] * nm + jnp.einsum('bsh,bshd->bhd', ws, kc)
        return (c_new, n_new, m_new), h

    xs = (to_chunks(q, nc, chunk), to_chunks(k, nc, chunk), to_chunks(v, nc, chunk),
          to_chunks(ipre, nc, chunk), to_chunks(logf, nc, chunk))
    carry0 = (c0.astype(jnp.float32), n0.astype(jnp.float32), m0.astype(jnp.float32))
    (c_f, n_f, m_f), hs = lax.scan(step, carry0, xs)
    return jnp.moveaxis(hs, 0, 1).reshape(bsz, L, H, -1), c_f, n_f, m_f


def mixer_layer(x, c, kv_past, conv_a_buf, ssm0, conv_c_buf, mc0, mn0, mm0, conv_d_buf,
                w_mod, b_mod, w_in, conv_a_w, conv_a_b, dt_bias, a_log, d_skip, norm_a_w,
                conv_c_w, conv_c_b, wq_c, wk_c, wv_c, ig_bias, fg_bias, norm_c_w, skip_c,
                conv_d_w, conv_d_b, ln_d_g, ln_d_b, w_out, ln_g, ln_b):
    bsz, L, _ = x.shape
    ql = chunk_len(L)
    shift, scale, gate = jnp.split(c.astype(jnp.float32) @ w_mod + b_mod, 3, axis=-1)
    u = layer_norm(x) * (1.0 + scale[:, None]) + shift[:, None]
    (z_a, xbc_a, dt_a, q_b, k_b, v_b, g_b, x_c, z_c, i_c, f_c, a_d, b_d, g_d) = split_cols(u @ w_in)

    xbc_a, conv_a_new = causal_dwconv(xbc_a, conv_a_buf, conv_a_w, conv_a_b)
    xbc_a = jax.nn.silu(xbc_a)
    xs_a = xbc_a[..., :W_A].reshape(bsz, L, H_A, P_A)
    bm_a = jnp.repeat(xbc_a[..., W_A:W_A + G_A * N_A].reshape(bsz, L, G_A, N_A), H_A // G_A, axis=2)
    cm_a = jnp.repeat(xbc_a[..., W_A + G_A * N_A:].reshape(bsz, L, G_A, N_A), H_A // G_A, axis=2)
    dt = jax.nn.softplus(dt_a + dt_bias)
    y_a, ssm_new = ssd_scan(xs_a, dt, -jnp.exp(a_log), bm_a, cm_a, ssm0, ql)
    y_a = (y_a + d_skip[:, None] * xs_a).reshape(bsz, L, W_A)
    y_a = rms_norm(y_a * jax.nn.silu(z_a), norm_a_w)

    q_b = q_b.reshape(bsz, L, H_B, D_B)
    k_b = k_b.reshape(bsz, L, H_B, D_B)
    v_b = v_b.reshape(bsz, L, H_B, D_B)
    if kv_past is None:
        past = 0
        k_all, v_all = k_b, v_b
    else:
        past = kv_past[0].shape[1]
        k_all = jnp.concatenate([kv_past[0], k_b], axis=1)
        v_all = jnp.concatenate([kv_past[1], v_b], axis=1)
    y_b = stick_breaking_attention(q_b, k_all, v_all, past + jnp.arange(L), jnp.arange(past + L))
    y_b = y_b.reshape(bsz, L, W_B) * jax.nn.silu(g_b)

    xconv, conv_c_new = causal_dwconv(x_c, conv_c_buf, conv_c_w, conv_c_b)
    xconv = jax.nn.silu(xconv)
    xh = xconv.reshape(bsz, L, H_C, DH_C)
    q_c = jnp.einsum('blhd,hde->blhe', xh, wq_c)
    k_c = jnp.einsum('blhd,hde->blhe', xh, wk_c) * DH_C ** -0.5
    v_c = jnp.einsum('blhd,hde->blhe', x_c.reshape(bsz, L, H_C, DH_C), wv_c)
    h_c, mc_new, mn_new, mm_new = mlstm_scan(q_c, k_c, v_c, i_c + ig_bias, jax.nn.log_sigmoid(f_c + fg_bias),
                                             mc0, mn0, mm0, ql)
    h_c = layer_norm(h_c) * norm_c_w.reshape(H_C, DH_C)
    y_c = (h_c.reshape(bsz, L, W_C) + skip_c * xconv) * jax.nn.silu(z_c)

    glu = a_d * jax.nn.sigmoid(b_d)
    y_d, conv_d_new = causal_dwconv(glu, conv_d_buf, conv_d_w, conv_d_b)
    y_d = jax.nn.silu(layer_norm(y_d) * ln_d_g + ln_d_b) * jax.nn.silu(g_d)

    o = jnp.concatenate([y_a, y_b, y_c, y_d], axis=-1) @ w_out
    x_new = layer_norm(ALPHA * x + (1.0 + gate[:, None]) * o) * ln_g + ln_b
    return x_new, (k_b, v_b, conv_a_new, ssm_new, conv_c_new, mc_new, mn_new, mm_new, conv_d_new)


def run_trunk(x, c, cache_k, cache_v, st_conv_a, st_ssm, st_conv_c, st_mc, st_mn, st_mm, st_conv_d, weights):
    outs = [[] for _ in range(9)]
    for l in range(DEPTH):
        kv = None if cache_k is None else (cache_k[l], cache_v[l])
        x, new = mixer_layer(x, c, kv, st_conv_a[l], st_ssm[l], st_conv_c[l], st_mc[l], st_mn[l], st_mm[l],
                             st_conv_d[l], *[w[l] for w in weights])
        for o, t in zip(outs, new):
            o.append(t)
    return x, [jnp.stack(o) for o in outs]


def setup_inputs(seed: int = 0) -> dict:
    key = jax.random.key(seed)
    ks = iter(jax.random.split(key, 48))

    def nrm(shape, s=1.0):
        return s * jax.random.normal(next(ks), shape, jnp.float32)

    L = DEPTH
    x_prompt = nrm((BATCH, SEQ, D_MODEL))
    x_sample = nrm((DEC_BATCH, DEC_SEQ, D_MODEL))
    cache_k = nrm((L, DEC_BATCH, PAST_LEN, H_B, D_B))
    cache_v = nrm((L, DEC_BATCH, PAST_LEN, H_B, D_B))
    state_conv_a = nrm((L, DEC_BATCH, K_A - 1, CONV_DIM_A))
    state_ssm = nrm((L, DEC_BATCH, H_A, P_A, N_A), 0.1)
    state_conv_c = nrm((L, DEC_BATCH, K_C - 1, W_C))
    state_mlstm_c = nrm((L, DEC_BATCH, H_C, DH_C, DH_C), 0.1)
    state_mlstm_n = nrm((L, DEC_BATCH, H_C, DH_C), 0.1)
    state_mlstm_m = nrm((L, DEC_BATCH, H_C))
    state_conv_d = nrm((L, DEC_BATCH, K_D - 1, W_D), 0.5)
    c_prompt = nrm((BATCH, D_MODEL))
    c_sample = nrm((DEC_BATCH, D_MODEL))
    w_mod = nrm((L, D_MODEL, 3 * D_MODEL), 0.2 * D_MODEL ** -0.5)
    b_mod = nrm((L, 3 * D_MODEL), 0.01)
    w_in = nrm((L, D_MODEL, IN_COLS), D_MODEL ** -0.5)
    conv_a_w = nrm((L, K_A, CONV_DIM_A), K_A ** -0.5)
    conv_a_b = nrm((L, CONV_DIM_A), 0.01)
    dt0 = jnp.exp(jax.random.uniform(next(ks), (L, H_A), jnp.float32, math.log(1e-3), math.log(1e-1)))
    dt_bias = dt0 + jnp.log(-jnp.expm1(-dt0))
    a_log = jnp.log(jax.random.uniform(next(ks), (L, H_A), jnp.float32, 1.0, 16.0))
    d_skip = 1.0 + nrm((L, H_A), 0.1)
    norm_a_w = 1.0 + nrm((L, W_A), 0.1)
    conv_c_w = nrm((L, K_C, W_C), K_C ** -0.5)
    conv_c_b = nrm((L, W_C), 0.01)
    wq_c = nrm((L, H_C, DH_C, DH_C), DH_C ** -0.5)
    wk_c = nrm((L, H_C, DH_C, DH_C), DH_C ** -0.5)
    wv_c = nrm((L, H_C, DH_C, DH_C), DH_C ** -0.5)
    ig_bias = nrm((L, H_C), 0.1)
    fg_bias = jnp.linspace(3.0, 6.0, H_C, dtype=jnp.float32) + nrm((L, H_C), 0.1)
    norm_c_w = 1.0 + nrm((L, W_C), 0.1)
    skip_c = 1.0 + nrm((L, W_C), 0.1)
    conv_d_w = nrm((L, K_D, W_D), K_D ** -0.5)
    conv_d_b = nrm((L, W_D), 0.01)
    ln_d_g = 1.0 + nrm((L, W_D), 0.1)
    ln_d_b = nrm((L, W_D), 0.01)
    w_out = nrm((L, D_MIX, D_MODEL), BETA * D_MIX ** -0.5)
    ln_g = 1.0 + nrm((L, D_MODEL), 0.1)
    ln_b = nrm((L, D_MODEL), 0.01)
    return {"x_prompt": x_prompt, "x_sample": x_sample, "cache_k": cache_k, "cache_v": cache_v,
            "state_conv_a": state_conv_a, "state_ssm": state_ssm, "state_conv_c": state_conv_c,
            "state_mlstm_c": state_mlstm_c, "state_mlstm_n": state_mlstm_n, "state_mlstm_m": state_mlstm_m,
            "state_conv_d": state_conv_d, "c_prompt": c_prompt, "c_sample": c_sample,
            "w_mod": w_mod, "b_mod": b_mod, "w_in": w_in, "conv_a_w": conv_a_w, "conv_a_b": conv_a_b,
            "dt_bias": dt_bias, "a_log": a_log, "d_skip": d_skip, "norm_a_w": norm_a_w,
            "conv_c_w": conv_c_w, "conv_c_b": conv_c_b, "wq_c": wq_c, "wk_c": wk_c, "wv_c": wv_c,
            "ig_bias": ig_bias, "fg_bias": fg_bias, "norm_c_w": norm_c_w, "skip_c": skip_c,
            "conv_d_w": conv_d_w, "conv_d_b": conv_d_b, "ln_d_g": ln_d_g, "ln_d_b": ln_d_b,
            "w_out": w_out, "ln_g": ln_g, "ln_b": ln_b}


def reference(x_prompt, x_sample, cache_k, cache_v, state_conv_a, state_ssm, state_conv_c,
              state_mlstm_c, state_mlstm_n, state_mlstm_m, state_conv_d, c_prompt, c_sample,
              w_mod, b_mod, w_in, conv_a_w, conv_a_b, dt_bias, a_log, d_skip, norm_a_w,
              conv_c_w, conv_c_b, wq_c, wk_c, wv_c, ig_bias, fg_bias, norm_c_w, skip_c,
              conv_d_w, conv_d_b, ln_d_g, ln_d_b, w_out, ln_g, ln_b):
    weights = (w_mod, b_mod, w_in, conv_a_w, conv_a_b, dt_bias, a_log, d_skip, norm_a_w,
               conv_c_w, conv_c_b, wq_c, wk_c, wv_c, ig_bias, fg_bias, norm_c_w, skip_c,
               conv_d_w, conv_d_b, ln_d_g, ln_d_b, w_out, ln_g, ln_b)

    def zeros(*shape):
        return jnp.zeros((DEPTH, BATCH) + shape, jnp.float32)

    y_prompt, sp = run_trunk(x_prompt, c_prompt, None, None,
                             zeros(K_A - 1, CONV_DIM_A), zeros(H_A, P_A, N_A), zeros(K_C - 1, W_C),
                             zeros(H_C, DH_C, DH_C), zeros(H_C, DH_C), zeros(H_C), zeros(K_D - 1, W_D), weights)
    y_sample, ss = run_trunk(x_sample, c_sample, cache_k, cache_v, state_conv_a, state_ssm, state_conv_c,
                             state_mlstm_c, state_mlstm_n, state_mlstm_m, state_conv_d, weights)
    return (y_prompt, y_sample,
            sp[0], sp[1], sp[2], sp[3], sp[4], sp[5], sp[6], sp[7], sp[8],
            ss[0], ss[1], ss[2], ss[3], ss[4], ss[5], ss[6], ss[7], ss[8])
```

```python
import functools
import math

import jax
import jax.numpy as jnp
from jax import lax
from jax.experimental import pallas as pl
from jax.experimental.pallas import tpu as pltpu

D_MODEL = 2048
DEPTH = 4
W_GROUP = 512
H_A, P_A, N_A, G_A, K_A = 8, 64, 128, 2, 4
CONV_DIM_A = W_GROUP + 2 * G_A * N_A
H_B, D_B = 8, 64
H_C, DH_C, K_C = 4, 128, 4
K_D = 31
ALPHA = (2 * DEPTH) ** 0.25
EPS = 1e-5
N_MAIN_GROUPS = 12
SMALL_W = 128
LANE_DT, LANE_I, LANE_F = 0, 8, 12

F32 = jnp.float32
BF16 = jnp.bfloat16
HIGHEST = lax.Precision.HIGHEST
VMEM_LIMIT = 56 * 1024 * 1024


def _cparams(sem):
    return pltpu.CompilerParams(dimension_semantics=sem, vmem_limit_bytes=VMEM_LIMIT)


def _dot(a, b):
    return jnp.dot(a, b, preferred_element_type=F32)


def _dot_nt(a, b):
    return lax.dot_general(a, b, (((1,), (1,)), ((), ())), preferred_element_type=F32)


def _dot_tn(a, b):
    return lax.dot_general(a, b, (((0,), (0,)), ((), ())), preferred_element_type=F32)


def _dot_exact(a, b):
    return jnp.dot(a, b, preferred_element_type=F32, precision=HIGHEST)


def _dot_nt_exact(a, b):
    return lax.dot_general(a, b, (((1,), (1,)), ((), ())), preferred_element_type=F32, precision=HIGHEST)


def _sigmoid(x):
    return 1.0 / (1.0 + jnp.exp(-x))


def _silu(x):
    return x * _sigmoid(x)


def _softplus(x):
    return jnp.maximum(x, 0.0) + jnp.log(1.0 + jnp.exp(-jnp.abs(x)))


def _iota(shape, dim):
    return lax.broadcasted_iota(jnp.int32, shape, dim)


def _lower_tri(n, strict=False):
    r, c = _iota((n, n), 0), _iota((n, n), 1)
    return (c < r) if strict else (c <= r)


MOD_TN = 1024


def _mod_kernel(c_ref, w_ref, b_ref, o_ref):
    o_ref[...] = _dot(c_ref[...].astype(BF16), w_ref[...].astype(BF16)) + b_ref[...]


def modulation(c_all, w_mod, b_mod):
    rows = c_all.shape[0]
    n = w_mod.shape[-1]
    return pl.pallas_call(
        _mod_kernel,
        grid=(DEPTH, n // MOD_TN),
        in_specs=[pl.BlockSpec((rows, D_MODEL), lambda l, j: (0, 0)),
                  pl.BlockSpec((None, D_MODEL, MOD_TN), lambda l, j: (l, 0, j)),
                  pl.BlockSpec((None, 1, MOD_TN), lambda l, j: (l, 0, j))],
        out_specs=pl.BlockSpec((None, rows, MOD_TN), lambda l, j: (l, 0, j)),
        out_shape=jax.ShapeDtypeStruct((DEPTH, rows, n), F32),
        compiler_params=_cparams(("arbitrary", "arbitrary")),
        name="modulation",
    )(c_all, w_mod, b_mod.reshape(DEPTH, 1, n))


def _row_tiling(bsz, seq, target):
    if seq >= target:
        return 1, target
    return min(bsz, target // seq), seq


def _inproj_kernel(x_ref, shift_ref, scale_ref, w_ref, ws_ref, proj_ref, small_ref, u_ref, *, bb, lt, cl):
    j = pl.program_id(2)
    n_l = lt // cl

    @pl.when(j == 0)
    def _():
        def body(it, carry):
            bi = it // n_l
            r0 = pl.multiple_of((it % n_l) * cl, cl)
            x = x_ref[bi, pl.ds(r0, cl), :]
            mu = jnp.mean(x, axis=-1, keepdims=True)
            xc = x - mu
            var = jnp.mean(xc * xc, axis=-1, keepdims=True)
            u = xc * lax.rsqrt(var + EPS) * (1.0 + scale_ref[bi]) + shift_ref[bi]
            u_ref[pl.ds(pl.multiple_of(it * cl, cl), cl), :] = u.astype(BF16)
            return carry
        lax.fori_loop(0, bb * n_l, body, 0)
        small_ref[...] = _dot(u_ref[...], ws_ref[...]).reshape(bb, lt, SMALL_W)

    proj_ref[...] = _dot(u_ref[...], w_ref[...]).reshape(bb, lt, W_GROUP)


def in_projection(x, mod, w_main, w_small):
    bsz, seq, _ = x.shape
    bb, lt = _row_tiling(bsz, seq, 1024)
    cl = min(lt, 128)
    tm = bb * lt
    kern = functools.partial(_inproj_kernel, bb=bb, lt=lt, cl=cl)
    return pl.pallas_call(
        kern,
        grid=(bsz // bb, seq // lt, N_MAIN_GROUPS),
        in_specs=[pl.BlockSpec((bb, lt, D_MODEL), lambda b, l, j: (b, l, 0)),
                  pl.BlockSpec((bb, 1, D_MODEL), lambda b, l, j: (b, 0, 0)),
                  pl.BlockSpec((bb, 1, D_MODEL), lambda b, l, j: (b, 0, 1)),
                  pl.BlockSpec((D_MODEL, W_GROUP), lambda b, l, j: (0, j)),
                  pl.BlockSpec((D_MODEL, SMALL_W), lambda b, l, j: (0, 0))],
        out_specs=[pl.BlockSpec((bb, lt, W_GROUP), lambda b, l, j: (b, l, j)),
                   pl.BlockSpec((bb, lt, SMALL_W), lambda b, l, j: (b, l, 0))],
        out_shape=[jax.ShapeDtypeStruct((bsz, seq, N_MAIN_GROUPS * W_GROUP), F32),
                   jax.ShapeDtypeStruct((bsz, seq, SMALL_W), F32)],
        scratch_shapes=[pltpu.VMEM((tm, D_MODEL), BF16)],
        compiler_params=_cparams(("arbitrary", "arbitrary", "arbitrary")),
        name="in_projection",
    )(x, mod, mod, w_main, w_small)


def _outproj_kernel(ya_ref, yb_ref, yc_ref, yd_ref, x_ref, gate_ref, w_ref, g_ref, b_ref, o_ref, acc_ref,
                    *, bb, lt, cl):
    tm = bb * lt
    acc = _dot(ya_ref[...].reshape(tm, W_GROUP), w_ref[0])
    acc += _dot(yb_ref[...].reshape(tm, W_GROUP), w_ref[1])
    acc += _dot(yc_ref[...].reshape(tm, W_GROUP), w_ref[2])
    acc += _dot(yd_ref[...].reshape(tm, W_GROUP), w_ref[3])
    acc_ref[...] = acc
    n_l = lt // cl

    def body(it, carry):
        bi = it // n_l
        r0 = pl.multiple_of((it % n_l) * cl, cl)
        o = acc_ref[pl.ds(pl.multiple_of(it * cl, cl), cl), :]
        v = ALPHA * x_ref[bi, pl.ds(r0, cl), :] + (1.0 + gate_ref[bi]) * o
        mu = jnp.mean(v, axis=-1, keepdims=True)
        vc = v - mu
        var = jnp.mean(vc * vc, axis=-1, keepdims=True)
        o_ref[bi, pl.ds(r0, cl), :] = vc * lax.rsqrt(var + EPS) * g_ref[...] + b_ref[...]
        return carry
    lax.fori_loop(0, bb * n_l, body, 0)


def out_projection(ys, x, mod, w_out, ln_g, ln_b):
    bsz, seq, _ = x.shape
    bb, lt = _row_tiling(bsz, seq, 512)
    cl = min(lt, 128)
    kern = functools.partial(_outproj_kernel, bb=bb, lt=lt, cl=cl)
    yspec = pl.BlockSpec((bb, lt, W_GROUP), lambda b, l: (b, l, 0))
    return pl.pallas_call(
        kern,
        grid=(bsz // bb, seq // lt),
        in_specs=[yspec, yspec, yspec, yspec,
                  pl.BlockSpec((bb, lt, D_MODEL), lambda b, l: (b, l, 0)),
                  pl.BlockSpec((bb, 1, D_MODEL), lambda b, l: (b, 0, 2)),
                  pl.BlockSpec((4, W_GROUP, D_MODEL), lambda b, l: (0, 0, 0)),
                  pl.BlockSpec((1, D_MODEL), lambda b, l: (0, 0)),
                  pl.BlockSpec((1, D_MODEL), lambda b, l: (0, 0))],
        out_specs=pl.BlockSpec((bb, lt, D_MODEL), lambda b, l: (b, l, 0)),
        out_shape=jax.ShapeDtypeStruct((bsz, seq, D_MODEL), F32),
        scratch_shapes=[pltpu.VMEM((bb * lt, D_MODEL), F32)],
        compiler_params=_cparams(("arbitrary", "arbitrary")),
        name="out_projection",
    )(*ys, x, mod, w_out, ln_g.reshape(1, D_MODEL), ln_b.reshape(1, D_MODEL))


CONV_PAD = 8


def _ssd_kernel(z_ref, x_ref, bc_ref, small_ref, cbuf_ref, ssm0_ref, cw_ref, cb_ref, dtb_ref, alog_ref,
                dskip_ref, nw_ref, y_ref, cnew_ref, ssmnew_ref, ext_ref, state_ref, wcat_ref, xbd_ref,
                ccat_ref, bcat_ref, *, q, nchunks):
    c = pl.program_id(1)
    tail = K_A - 1

    @pl.when(c == 0)
    def _():
        ext_ref[CONV_PAD - tail:CONV_PAD, :] = cbuf_ref[0]
        state_ref[...] = jnp.zeros_like(state_ref)
        for h in range(H_A):
            state_ref[h * P_A:(h + 1) * P_A, h * N_A:(h + 1) * N_A] = ssm0_ref[0, h]

    ext_ref[CONV_PAD:CONV_PAD + q, 0:W_GROUP] = x_ref[0]
    ext_ref[CONV_PAD:CONV_PAD + q, W_GROUP:] = bc_ref[0]
    conv = cb_ref[...]
    for k in range(K_A):
        off = CONV_PAD - tail + k
        conv = conv + cw_ref[k:k + 1, :] * ext_ref[off:off + q, :]
    new_tail = ext_ref[CONV_PAD + q - tail:CONV_PAD + q, :]
    ext_ref[CONV_PAD - tail:CONV_PAD, :] = new_tail

    @pl.when(c == nchunks - 1)
    def _():
        cnew_ref[0] = new_tail

    xbc = _silu(conv)
    xs = xbc[:, :W_GROUP]
    bm = [xbc[:, W_GROUP + g * N_A:W_GROUP + (g + 1) * N_A] for g in range(G_A)]
    cm = [xbc[:, W_GROUP + (G_A + g) * N_A:W_GROUP + (G_A + g + 1) * N_A] for g in range(G_A)]

    lane = _iota((1, SMALL_W), 1)
    head_lanes = lane < H_A
    dt = jnp.where(head_lanes, _softplus(small_ref[0] + dtb_ref[...]), 0.0)
    a = -jnp.exp(alog_ref[...])
    da = dt * a
    tri = _lower_tri(q).astype(F32)
    acum = _dot_exact(tri, da)
    eye8 = (_iota((8, SMALL_W), 0) == _iota((8, SMALL_W), 1)).astype(F32)
    acum_row = _dot_nt_exact(eye8, acum)
    last = acum[q - 1:q, :]
    e_acum = jnp.exp(acum)
    w_s = jnp.exp(last - acum)
    e_last = jnp.exp(last)

    expand = (_iota((SMALL_W, W_GROUP), 1) // P_A == _iota((SMALL_W, W_GROUP), 0)).astype(F32)
    dt_wide = _dot_exact(dt, expand)
    xdt = xs * dt_wide
    xdt_bf = xdt.astype(BF16)
    col_head = _iota((1, W_GROUP), 1) // P_A
    causal = _lower_tri(q)

    gmat = [_dot_nt(cm[g].astype(BF16), bm[g].astype(BF16)) for g in range(G_A)]
    for h in range(H_A):
        g = h // (H_A // G_A)
        seg = acum[:, h:h + 1] - acum_row[h:h + 1, :]
        decay = jnp.exp(jnp.where(causal, seg, -jnp.inf))
        wcat_ref[:, h * q:(h + 1) * q] = (gmat[g] * decay).astype(BF16)
        xbd_ref[h * q:(h + 1) * q, :] = jnp.where(col_head == h, xdt_bf, jnp.zeros_like(xdt_bf))
        ccat_ref[:, h * N_A:(h + 1) * N_A] = (cm[g] * e_acum[:, h:h + 1]).astype(BF16)
        bcat_ref[:, h * N_A:(h + 1) * N_A] = (bm[g] * w_s[:, h:h + 1]).astype(BF16)

    y = _dot(wcat_ref[...], xbd_ref[...])
    y = y + _dot_nt(ccat_ref[...], state_ref[...].astype(BF16))
    y = y + dskip_ref[...] * xs

    upd = _dot_tn(xdt_bf, bcat_ref[...])
    for h in range(H_A):
        rs, cs = slice(h * P_A, (h + 1) * P_A), slice(h * N_A, (h + 1) * N_A)
        state_ref[rs, cs] = e_last[:, h:h + 1] * state_ref[rs, cs] + upd[rs, cs]

    @pl.when(c == nchunks - 1)
    def _():
        for h in range(H_A):
            ssmnew_ref[0, h] = state_ref[h * P_A:(h + 1) * P_A, h * N_A:(h + 1) * N_A]

    yz = y * _silu(z_ref[0])
    ms = jnp.mean(yz * yz, axis=-1, keepdims=True)
    y_ref[0] = (yz * lax.rsqrt(ms + EPS) * nw_ref[...]).astype(BF16)


def mixer_ssd(proj, small, cbuf, ssm0, conv_w, conv_b, dt_bias, a_log, d_skip, norm_w):
    bsz, seq, _ = proj.shape
    q = min(seq, 256)
    nchunks = seq // q
    pad = lambda v: jnp.zeros((1, SMALL_W), F32).at[0, :v.shape[0]].set(v)
    kern = functools.partial(_ssd_kernel, q=q, nchunks=nchunks)
    col = lambda j: pl.BlockSpec((1, q, W_GROUP), lambda b, c: (b, c, j))
    full = lambda shape: pl.BlockSpec(shape, lambda b, c: (0,) * len(shape))
    return pl.pallas_call(
        kern,
        grid=(bsz, nchunks),
        in_specs=[col(0), col(1), col(2),
                  pl.BlockSpec((1, q, SMALL_W), lambda b, c: (b, c, 0)),
                  pl.BlockSpec((1, K_A - 1, CONV_DIM_A), lambda b, c: (b, 0, 0)),
                  pl.BlockSpec((1, H_A, P_A, N_A), lambda b, c: (b, 0, 0, 0)),
                  full((K_A, CONV_DIM_A)), full((1, CONV_DIM_A)), full((1, SMALL_W)), full((1, SMALL_W)),
                  full((1, W_GROUP)), full((1, W_GROUP))],
        out_specs=[pl.BlockSpec((1, q, W_GROUP), lambda b, c: (b, c, 0)),
                   pl.BlockSpec((1, K_A - 1, CONV_DIM_A), lambda b, c: (b, 0, 0)),
                   pl.BlockSpec((1, H_A, P_A, N_A), lambda b, c: (b, 0, 0, 0))],
        out_shape=[jax.ShapeDtypeStruct((bsz, seq, W_GROUP), BF16),
                   jax.ShapeDtypeStruct((bsz, K_A - 1, CONV_DIM_A), F32),
                   jax.ShapeDtypeStruct((bsz, H_A, P_A, N_A), F32)],
        scratch_shapes=[pltpu.VMEM((CONV_PAD + q, CONV_DIM_A), F32),
                        pltpu.VMEM((H_A * P_A, H_A * N_A), F32),
                        pltpu.VMEM((q, H_A * q), BF16),
                        pltpu.VMEM((H_A * q, W_GROUP), BF16),
                        pltpu.VMEM((q, H_A * N_A), BF16),
                        pltpu.VMEM((q, H_A * N_A), BF16)],
        compiler_params=_cparams(("arbitrary", "arbitrary")),
        name="mixer_ssd",
    )(proj, proj, proj, small, cbuf, ssm0, conv_w, conv_b.reshape(1, CONV_DIM_A), pad(dt_bias), pad(a_log),
      jnp.repeat(d_skip, P_A).reshape(1, W_GROUP), norm_w.reshape(1, W_GROUP))


def _mlstm_kernel(xc_ref, zc_ref, small_ref, cbuf_ref, c0_ref, n0_ref, m0_ref, cw_ref, cb_ref, wq_ref, wk_ref,
                  wv_ref, igb_ref, fgb_ref, nw_ref, skip_ref, y_ref, cnew_ref, cst_ref, nst_ref, mst_ref,
                  ext_ref, cs_ref, ns_ref, ms_ref, *, q, nchunks):
    c = pl.program_id(1)
    tail = K_C - 1

    @pl.when(c == 0)
    def _():
        ext_ref[CONV_PAD - tail:CONV_PAD, :] = cbuf_ref[0]
        cs_ref[...] = c0_ref[0]
        ns_ref[...] = n0_ref[0]
        ms_ref[...] = m0_ref[0]

    x_in = xc_ref[0]
    ext_ref[CONV_PAD:CONV_PAD + q, :] = x_in
    conv = cb_ref[...]
    for k in range(K_C):
        off = CONV_PAD - tail + k
        conv = conv + cw_ref[k:k + 1, :] * ext_ref[off:off + q, :]
    new_tail = ext_ref[CONV_PAD + q - tail:CONV_PAD + q, :]
    ext_ref[CONV_PAD - tail:CONV_PAD, :] = new_tail

    @pl.when(c == nchunks - 1)
    def _():
        cnew_ref[0] = new_tail

    xconv = _silu(conv)
    sm = small_ref[0]
    ipre = sm + igb_ref[...]
    fpre = sm + fgb_ref[...]
    logf = jnp.minimum(fpre, 0.0) - jnp.log(1.0 + jnp.exp(-jnp.abs(fpre)))
    tri = _lower_tri(q).astype(F32)
    bcum = _dot_exact(tri, logf)
    sel_i = (_iota((8, SMALL_W), 1) == _iota((8, SMALL_W), 0) + LANE_I).astype(F32)
    sel_f = (_iota((8, SMALL_W), 1) == _iota((8, SMALL_W), 0) + LANE_F).astype(F32)
    r_row = _dot_nt_exact(sel_i, ipre) - _dot_nt_exact(sel_f, bcum)
    causal = _lower_tri(q)
    scale_k = DH_C ** -0.5

    for h in range(H_C):
        hs = slice(h * DH_C, (h + 1) * DH_C)
        xh = xconv[:, hs].astype(BF16)
        qh = _dot(xh, wq_ref[h])
        kh = _dot(xh, wk_ref[h]) * scale_k
        vh = _dot(x_in[:, hs].astype(BF16), wv_ref[h])
        qb, kb, vb = qh.astype(BF16), kh.astype(BF16), vh.astype(BF16)
        b_col = bcum[:, LANE_F + h:LANE_F + h + 1]
        m_prev = ms_ref[:, h:h + 1]
        d = jnp.where(causal, b_col + r_row[h:h + 1, :], -jnp.inf)
        inter = b_col + m_prev
        m_t = jnp.maximum(inter, jnp.max(d, axis=-1, keepdims=True))
        w = jnp.exp(d - m_t) * _dot_nt(qb, kb)
        gq = jnp.exp(inter - m_t)
        c_old = cs_ref[h]
        n_old = ns_ref[h:h + 1, :]
        num = _dot(w.astype(BF16), vb) + gq * _dot_nt(qb, c_old.astype(BF16))
        nq = jnp.sum(w, axis=-1, keepdims=True) + gq * jnp.sum(qh * n_old, axis=-1, keepdims=True)
        hid = num / jnp.maximum(jnp.abs(nq), jnp.exp(-m_t))
        m_new = m_t[q - 1:q, :]
        b_last = b_col[q - 1:q, :]
        ws = jnp.exp(b_last - b_col + ipre[:, LANE_I + h:LANE_I + h + 1] - m_new)
        g_last = jnp.exp(b_last + m_prev - m_new)
        cs_ref[h] = g_last * c_old + _dot_tn((vh * ws).astype(BF16), kb)
        ns_ref[h:h + 1, :] = g_last * n_old + jnp.sum(ws * kh, axis=0, keepdims=True)
        ms_ref[:, h:h + 1] = m_new

        mu = jnp.mean(hid, axis=-1, keepdims=True)
        hc = hid - mu
        var = jnp.mean(hc * hc, axis=-1, keepdims=True)
        hn = hc * lax.rsqrt(var + EPS) * nw_ref[:, hs]
        yh = (hn + skip_ref[:, hs] * xconv[:, hs]) * _silu(zc_ref[0, :, hs])
        y_ref[0, :, hs] = yh.astype(BF16)

    @pl.when(c == nchunks - 1)
    def _():
        cst_ref[0] = cs_ref[...]
        nst_ref[0] = ns_ref[...]
        mst_ref[0] = ms_ref[...]


def mixer_mlstm(proj, small, cbuf, c0, n0, m0, conv_w, conv_b, wq, wk, wv, ig_bias, fg_bias, norm_w, skip):
    bsz, seq, _ = proj.shape
    q = min(seq, 256)
    nchunks = seq // q
    kern = functools.partial(_mlstm_kernel, q=q, nchunks=nchunks)
    pad_at = lambda v, lane: jnp.zeros((1, SMALL_W), F32).at[0, lane:lane + v.shape[0]].set(v)
    col = lambda j: pl.BlockSpec((1, q, W_GROUP), lambda b, c: (b, c, j))
    full = lambda shape: pl.BlockSpec(shape, lambda b, c: (0,) * len(shape))
    y, cnew, cst, nst, mst = pl.pallas_call(
        kern,
        grid=(bsz, nchunks),
        in_specs=[col(7), col(8),
                  pl.BlockSpec((1, q, SMALL_W), lambda b, c: (b, c, 0)),
                  pl.BlockSpec((1, K_C - 1, W_GROUP), lambda b, c: (b, 0, 0)),
                  pl.BlockSpec((1, H_C, DH_C, DH_C), lambda b, c: (b, 0, 0, 0)),
                  pl.BlockSpec((1, H_C, DH_C), lambda b, c: (b, 0, 0)),
                  pl.BlockSpec((1, 1, H_C), lambda b, c: (b, 0, 0)),
                  full((K_C, W_GROUP)), full((1, W_GROUP)),
                  full((H_C, DH_C, DH_C)), full((H_C, DH_C, DH_C)), full((H_C, DH_C, DH_C)),
                  full((1, SMALL_W)), full((1, SMALL_W)), full((1, W_GROUP)), full((1, W_GROUP))],
        out_specs=[pl.BlockSpec((1, q, W_GROUP), lambda b, c: (b, c, 0)),
                   pl.BlockSpec((1, K_C - 1, W_GROUP), lambda b, c: (b, 0, 0)),
                   pl.BlockSpec((1, H_C, DH_C, DH_C), lambda b, c: (b, 0, 0, 0)),
                   pl.BlockSpec((1, H_C, DH_C), lambda b, c: (b, 0, 0)),
                   pl.BlockSpec((1, 1, H_C), lambda b, c: (b, 0, 0))],
        out_shape=[jax.ShapeDtypeStruct((bsz, seq, W_GROUP), BF16),
                   jax.ShapeDtypeStruct((bsz, K_C - 1, W_GROUP), F32),
                   jax.ShapeDtypeStruct((bsz, H_C, DH_C, DH_C), F32),
                   jax.ShapeDtypeStruct((bsz, H_C, DH_C), F32),
                   jax.ShapeDtypeStruct((bsz, 1, H_C), F32)],
        scratch_shapes=[pltpu.VMEM((CONV_PAD + q, W_GROUP), F32),
                        pltpu.VMEM((H_C, DH_C, DH_C), F32),
                        pltpu.VMEM((H_C, DH_C), F32),
                        pltpu.VMEM((1, H_C), F32)],
        compiler_params=_cparams(("arbitrary", "arbitrary")),
        name="mixer_mlstm",
    )(proj, proj, small, cbuf, c0, n0, m0.reshape(bsz, 1, H_C), conv_w, conv_b.reshape(1, W_GROUP),
      wq.astype(BF16), wk.astype(BF16), wv.astype(BF16), pad_at(ig_bias, LANE_I), pad_at(fg_bias, LANE_F),
      norm_w.reshape(1, W_GROUP), skip.reshape(1, W_GROUP))
    return y, cnew, cst, nst, mst.reshape(bsz, H_C)


CONV_D_PAD = 32


def _conf_kernel(a_ref, b_ref, g_ref, cbuf_ref, cw_ref, cb_ref, lg_ref, lb_ref, y_ref, cnew_ref, ext_ref,
                 *, q, nchunks):
    c = pl.program_id(1)
    tail = K_D - 1

    @pl.when(c == 0)
    def _():
        ext_ref[CONV_D_PAD - tail:CONV_D_PAD, :] = cbuf_ref[0]

    ext_ref[CONV_D_PAD:CONV_D_PAD + q, :] = a_ref[0] * _sigmoid(b_ref[0])
    conv = cb_ref[...]
    for k in range(K_D):
        off = CONV_D_PAD - tail + k
        conv = conv + cw_ref[k:k + 1, :] * ext_ref[off:off + q, :]
    new_tail = ext_ref[CONV_D_PAD + q - tail:CONV_D_PAD + q, :]
    ext_ref[CONV_D_PAD - tail:CONV_D_PAD, :] = new_tail

    @pl.when(c == nchunks - 1)
    def _():
        cnew_ref[0] = new_tail

    mu = jnp.mean(conv, axis=-1, keepdims=True)
    cc = conv - mu
    var = jnp.mean(cc * cc, axis=-1, keepdims=True)
    v = cc * lax.rsqrt(var + EPS) * lg_ref[...] + lb_ref[...]
    y_ref[0] = (_silu(v) * _silu(g_ref[0])).astype(BF16)


def mixer_conformer(proj, cbuf, conv_w, conv_b, ln_g, ln_b):
    bsz, seq, _ = proj.shape
    q = min(seq, 256)
    nchunks = seq // q
    kern = functools.partial(_conf_kernel, q=q, nchunks=nchunks)
    col = lambda j: pl.BlockSpec((1, q, W_GROUP), lambda b, c: (b, c, j))
    full = lambda shape: pl.BlockSpec(shape, lambda b, c: (0,) * len(shape))
    return pl.pallas_call(
        kern,
        grid=(bsz, nchunks),
        in_specs=[col(9), col(10), col(11),
                  pl.BlockSpec((1, K_D - 1, W_GROUP), lambda b, c: (b, 0, 0)),
                  full((K_D, W_GROUP)), full((1, W_GROUP)), full((1, W_GROUP)), full((1, W_GROUP))],
        out_specs=[pl.BlockSpec((1, q, W_GROUP), lambda b, c: (b, c, 0)),
                   pl.BlockSpec((1, K_D - 1, W_GROUP), lambda b, c: (b, 0, 0))],
        out_shape=[jax.ShapeDtypeStruct((bsz, seq, W_GROUP), BF16),
                   jax.ShapeDtypeStruct((bsz, K_D - 1, W_GROUP), F32)],
        scratch_shapes=[pltpu.VMEM((CONV_D_PAD + q, W_GROUP), F32)],
        compiler_params=_cparams(("arbitrary", "arbitrary")),
        name="mixer_conformer",
    )(proj, proj, proj, cbuf, conv_w, conv_b.reshape(1, W_GROUP), ln_g.reshape(1, W_GROUP),
      ln_b.reshape(1, W_GROUP))


ATT_BLOCK = 256


def _stick_block(z, mask, carry, suffix_mat):
    lk = -_softplus(z)
    if mask is not None:
        lk = jnp.where(mask, lk, 0.0)
    hi = lk.astype(BF16)
    lo = (lk - hi.astype(F32)).astype(BF16)
    suffix = _dot(hi, suffix_mat) + _dot(lo, suffix_mat)
    w = jnp.exp(z + lk + suffix + carry)
    if mask is not None:
        w = jnp.where(mask, w, 0.0)
    total = suffix[:, 0:1] + lk[:, 0:1]
    return w, carry + total


def _attn_prompt_kernel(q_ref, k_ref, v_ref, g_ref, sm_ref, y_ref, kb_ref, vb_ref, *, t):
    qi = pl.program_id(2)
    heads = W_GROUP // 4 // D_B

    @pl.when(qi == 0)
    def _():
        for hh in range(heads):
            kb_ref[hh] = k_ref[0, :, hh * D_B:(hh + 1) * D_B].astype(BF16)
            vb_ref[hh] = v_ref[0, :, hh * D_B:(hh + 1) * D_B].astype(BF16)

    scale = D_B ** -0.5
    suffix_mat = sm_ref[...]
    diag_mask = _lower_tri(t, strict=True)
    outs = []
    for hh in range(heads):
        qh = (q_ref[0, :, hh * D_B:(hh + 1) * D_B] * scale).astype(BF16)
        r0 = pl.multiple_of(qi * t, t)
        z = _dot_nt(qh, kb_ref[hh, pl.ds(r0, t), :])
        w, carry = _stick_block(z, diag_mask, jnp.zeros((t, 1), F32), suffix_mat)
        acc = _dot(w.astype(BF16), vb_ref[hh, pl.ds(r0, t), :])

        def body(i, state, qh=qh, hh=hh):
            acc, carry = state
            r = pl.multiple_of((qi - 1 - i) * t, t)
            z = _dot_nt(qh, kb_ref[hh, pl.ds(r, t), :])
            w, carry = _stick_block(z, None, carry, suffix_mat)
            return acc + _dot(w.astype(BF16), vb_ref[hh, pl.ds(r, t), :]), carry
        acc, _ = lax.fori_loop(0, qi, body, (acc, carry))
        outs.append(acc)
    out = jnp.concatenate(outs, axis=-1)
    y_ref[0] = (out * _silu(g_ref[0])).astype(BF16)


def _suffix_matrix(n):
    return (jnp.arange(n)[:, None] > jnp.arange(n)[None, :]).astype(BF16)


def attention_prompt(proj):
    bsz, seq, _ = proj.shape
    t = min(seq, ATT_BLOCK)
    nq = seq // t
    lanes = 128
    per = W_GROUP // lanes
    kern = functools.partial(_attn_prompt_kernel, t=t)
    return pl.pallas_call(
        kern,
        grid=(bsz, per, nq),
        in_specs=[pl.BlockSpec((1, t, lanes), lambda b, p, i: (b, i, 3 * per + p)),
                  pl.BlockSpec((1, seq, lanes), lambda b, p, i: (b, 0, 4 * per + p)),
                  pl.BlockSpec((1, seq, lanes), lambda b, p, i: (b, 0, 5 * per + p)),
                  pl.BlockSpec((1, t, lanes), lambda b, p, i: (b, i, 6 * per + p)),
                  pl.BlockSpec((t, t), lambda b, p, i: (0, 0))],
        out_specs=pl.BlockSpec((1, t, lanes), lambda b, p, i: (b, i, p)),
        out_shape=jax.ShapeDtypeStruct((bsz, seq, W_GROUP), BF16),
        scratch_shapes=[pltpu.VMEM((lanes // D_B, seq, D_B), BF16),
                        pltpu.VMEM((lanes // D_B, seq, D_B), BF16)],
        compiler_params=_cparams(("arbitrary", "arbitrary", "arbitrary")),
        name="attention_prompt",
    )(proj, proj, proj, proj, _suffix_matrix(t))


CACHE_BLOCK = 1024


def _attn_sample_kernel(q_ref, kn_ref, vn_ref, g_ref, kc_ref, vc_ref, sm_ref, smn_ref, y_ref,
                        qbd_ref, acc_ref, carry_ref, *, lq, nkb):
    j = pl.program_id(1)
    rows = H_B * lq
    row_head = _iota((rows, 1), 0) // lq
    col_head = _iota((1, W_GROUP), 1) // D_B
    head_mask = row_head == col_head

    @pl.when(j == 0)
    def _():
        qs = (q_ref[0] * (D_B ** -0.5)).astype(BF16)
        qt = jnp.concatenate([qs] * H_B, axis=0)
        qbd = jnp.where(head_mask, qt, jnp.zeros_like(qt))
        qbd_ref[...] = qbd
        z = _dot_nt(qbd, kn_ref[0].astype(BF16))
        t_idx = _iota((rows, lq), 0) % lq
        mask = _iota((rows, lq), 1) < t_idx
        w, carry = _stick_block(z, mask, jnp.zeros((rows, 1), F32), smn_ref[...])
        acc_ref[...] = _dot(w.astype(BF16), vn_ref[0].astype(BF16))
        carry_ref[...] = carry

    qbd = qbd_ref[...]
    suffix_mat = sm_ref[...]
    nsub = CACHE_BLOCK // ATT_BLOCK

    def body(i, state):
        acc, carry = state
        r = pl.multiple_of((nsub - 1 - i) * ATT_BLOCK, ATT_BLOCK)
        z = _dot_nt(qbd, kc_ref[0, pl.ds(r, ATT_BLOCK), :].astype(BF16))
        w, carry = _stick_block(z, None, carry, suffix_mat)
        return acc + _dot(w.astype(BF16), vc_ref[0, pl.ds(r, ATT_BLOCK), :].astype(BF16)), carry
    acc, carry = lax.fori_loop(0, nsub, body, (acc_ref[...], carry_ref[...]))
    acc_ref[...] = acc
    carry_ref[...] = carry

    @pl.when(j == nkb - 1)
    def _():
        masked = jnp.where(head_mask, acc, 0.0)
        out = masked[0:lq]
        for h in range(1, H_B):
            out = out + masked[h * lq:(h + 1) * lq]
        y_ref[0] = (out * _silu(g_ref[0])).astype(BF16)


def attention_sample(proj, cache_k, cache_v):
    bsz, lq, _ = proj.shape
    past = cache_k.shape[1]
    nkb = past // CACHE_BLOCK
    kern = functools.partial(_attn_sample_kernel, lq=lq, nkb=nkb)
    col = lambda c: pl.BlockSpec((1, lq, W_GROUP), lambda b, j: (b, 0, c))
    cache = pl.BlockSpec((1, CACHE_BLOCK, W_GROUP), lambda b, j: (b, nkb - 1 - j, 0))
    return pl.pallas_call(
        kern,
        grid=(bsz, nkb),
        in_specs=[col(3), col(4), col(5), col(6), cache, cache,
                  pl.BlockSpec((ATT_BLOCK, ATT_BLOCK), lambda b, j: (0, 0)),
                  pl.BlockSpec((lq, lq), lambda b, j: (0, 0))],
        out_specs=pl.BlockSpec((1, lq, W_GROUP), lambda b, j: (b, 0, 0)),
        out_shape=jax.ShapeDtypeStruct((bsz, lq, W_GROUP), BF16),
        scratch_shapes=[pltpu.VMEM((H_B * lq, W_GROUP), BF16),
                        pltpu.VMEM((H_B * lq, W_GROUP), F32),
                        pltpu.VMEM((H_B * lq, 1), F32)],
        compiler_params=_cparams(("arbitrary", "arbitrary")),
        name="attention_sample",
    )(proj, proj, proj, proj, cache_k, cache_v, _suffix_matrix(ATT_BLOCK), _suffix_matrix(lq))


def _repack_w_in(w_in):
    o_dt = W_GROUP + CONV_DIM_A
    o_q = o_dt + H_A
    o_i = o_q + 4 * W_GROUP + 2 * W_GROUP
    o_ad = o_i + 2 * H_C
    main = jnp.concatenate([w_in[..., :o_dt], w_in[..., o_q:o_i], w_in[..., o_ad:]], axis=-1)
    small = jnp.concatenate([w_in[..., o_dt:o_q], w_in[..., o_i:o_ad]], axis=-1)
    small = jnp.pad(small, [(0, 0)] * (small.ndim - 1) + [(0, SMALL_W - small.shape[-1])])
    return main.astype(BF16), small.astype(BF16)


def _mixer_layer(x, mod, kv_past, conv_a_buf, ssm0, conv_c_buf, mc0, mn0, mm0, conv_d_buf, lw):
    bsz, seq, _ = x.shape
    proj, small = in_projection(x, mod, lw["w_main"], lw["w_small"])
    y_a, conv_a_new, ssm_new = mixer_ssd(proj, small, conv_a_buf, ssm0, lw["conv_a_w"], lw["conv_a_b"],
                                         lw["dt_bias"], lw["a_log"], lw["d_skip"], lw["norm_a_w"])
    if kv_past is None:
        y_b = attention_prompt(proj)
    else:
        y_b = attention_sample(proj, kv_past[0], kv_past[1])
    y_c, conv_c_new, mc_new, mn_new, mm_new = mixer_mlstm(
        proj, small, conv_c_buf, mc0, mn0, mm0, lw["conv_c_w"], lw["conv_c_b"], lw["wq_c"], lw["wk_c"],
        lw["wv_c"], lw["ig_bias"], lw["fg_bias"], lw["norm_c_w"], lw["skip_c"])
    y_d, conv_d_new = mixer_conformer(proj, conv_d_buf, lw["conv_d_w"], lw["conv_d_b"], lw["ln_d_g"],
                                      lw["ln_d_b"])
    x_new = out_projection((y_a, y_b, y_c, y_d), x, mod, lw["w_out"], lw["ln_g"], lw["ln_b"])
    k_new = proj[..., 4 * W_GROUP:5 * W_GROUP].reshape(bsz, seq, H_B, D_B)
    v_new = proj[..., 5 * W_GROUP:6 * W_GROUP].reshape(bsz, seq, H_B, D_B)
    return x_new, (k_new, v_new, conv_a_new, ssm_new, conv_c_new, mc_new, mn_new, mm_new, conv_d_new)


def _run_trunk(x, mods, cache_k, cache_v, st_conv_a, st_ssm, st_conv_c, st_mc, st_mn, st_mm, st_conv_d, weights):
    outs = [[] for _ in range(9)]
    for l in range(DEPTH):
        kv = None
        if cache_k is not None:
            bsz, past = cache_k.shape[1], cache_k.shape[2]
            kv = (cache_k[l].reshape(bsz, past, W_GROUP), cache_v[l].reshape(bsz, past, W_GROUP))
        lw = {name: w[l] for name, w in weights.items()}
        x, new = _mixer_layer(x, mods[l], kv, st_conv_a[l], st_ssm[l], st_conv_c[l], st_mc[l], st_mn[l],
                              st_mm[l], st_conv_d[l], lw)
        for o, t in zip(outs, new):
            o.append(t)
    return x, [jnp.stack(o) for o in outs]


def kernel(x_prompt, x_sample, cache_k, cache_v, state_conv_a, state_ssm, state_conv_c, state_mlstm_c,
           state_mlstm_n, state_mlstm_m, state_conv_d, c_prompt, c_sample, w_mod, b_mod, w_in, conv_a_w,
           conv_a_b, dt_bias, a_log, d_skip, norm_a_w, conv_c_w, conv_c_b, wq_c, wk_c, wv_c, ig_bias, fg_bias,
           norm_c_w, skip_c, conv_d_w, conv_d_b, ln_d_g, ln_d_b, w_out, ln_g, ln_b):
    batch, dec_batch = x_prompt.shape[0], x_sample.shape[0]
    w_main, w_small = _repack_w_in(w_in)
    weights = dict(w_main=w_main, w_small=w_small, conv_a_w=conv_a_w, conv_a_b=conv_a_b, dt_bias=dt_bias,
                   a_log=a_log, d_skip=d_skip, norm_a_w=norm_a_w, conv_c_w=conv_c_w, conv_c_b=conv_c_b,
                   wq_c=wq_c, wk_c=wk_c, wv_c=wv_c, ig_bias=ig_bias, fg_bias=fg_bias, norm_c_w=norm_c_w,
                   skip_c=skip_c, conv_d_w=conv_d_w, conv_d_b=conv_d_b, ln_d_g=ln_d_g, ln_d_b=ln_d_b,
                   w_out=w_out.reshape(DEPTH, 4, W_GROUP, D_MODEL).astype(BF16), ln_g=ln_g, ln_b=ln_b)

    rows = batch + dec_batch
    rows_pad = -(-rows // 8) * 8
    c_all = jnp.concatenate([c_prompt, c_sample, jnp.zeros((rows_pad - rows, D_MODEL), F32)], axis=0)
    mod_all = modulation(c_all, w_mod, b_mod)
    mods_p = mod_all[:, :batch].reshape(DEPTH, batch, 1, 3 * D_MODEL)
    mods_s = mod_all[:, batch:rows].reshape(DEPTH, dec_batch, 1, 3 * D_MODEL)

    def zeros(*shape):
        return jnp.zeros((DEPTH, batch) + shape, F32)

    y_prompt, sp = _run_trunk(x_prompt, mods_p, None, None,
                              zeros(K_A - 1, CONV_DIM_A), zeros(H_A, P_A, N_A), zeros(K_C - 1, W_GROUP),
                              zeros(H_C, DH_C, DH_C), zeros(H_C, DH_C), zeros(H_C), zeros(K_D - 1, W_GROUP),
                              weights)
    y_sample, ss = _run_trunk(x_sample, mods_s, cache_k, cache_v, state_conv_a, state_ssm, state_conv_c,
                              state_mlstm_c, state_mlstm_n, state_mlstm_m, state_conv_d, weights)
    return (y_prompt, y_sample, *sp, *ss)
```

```python
import functools
import math

import jax
import jax.numpy as jnp
from jax import lax
from jax.experimental import pallas as pl
from jax.experimental.pallas import tpu as pltpu

D_MODEL = 2048
DEPTH = 4
W_GROUP = 512
H_A, P_A, N_A, G_A, K_A = 8, 64, 128, 2, 4
CONV_DIM_A = W_GROUP + 2 * G_A * N_A
H_B, D_B = 8, 64
H_C, DH_C, K_C = 4, 128, 4
K_D = 31
ALPHA = (2 * DEPTH) ** 0.25
EPS = 1e-5
N_MAIN_GROUPS = 12
G_ZA, G_XA, G_BCA, G_Q, G_GB, G_XC, G_ZC, G_AD, G_BD, G_GD, G_K, G_V = range(N_MAIN_GROUPS)
SMALL_W = 128
LANE_DT, LANE_I, LANE_F = 0, 8, 12

F32 = jnp.float32
BF16 = jnp.bfloat16
HIGHEST = lax.Precision.HIGHEST
VMEM_LIMIT = 56 * 1024 * 1024


def _cparams(sem):
    return pltpu.CompilerParams(dimension_semantics=sem, vmem_limit_bytes=VMEM_LIMIT)


def _dot(a, b):
    return jnp.dot(a, b, preferred_element_type=F32)


def _dot_nt(a, b):
    return lax.dot_general(a, b, (((1,), (1,)), ((), ())), preferred_element_type=F32)


def _dot_tn(a, b):
    return lax.dot_general(a, b, (((0,), (0,)), ((), ())), preferred_element_type=F32)


def _split3(a):
    hi = a.astype(BF16)
    r = a - hi.astype(F32)
    mid = r.astype(BF16)
    lo = (r - mid.astype(F32)).astype(BF16)
    return jnp.concatenate([hi, mid, lo], axis=1)


def _sum3(p):
    n = p.shape[1] // 3
    return (p[:, 2 * n:] + p[:, n:2 * n]) + p[:, :n]


def _select_left(sel01, a):
    return _sum3(_dot(sel01.astype(BF16), _split3(a)))


def _select_right(a, sel01):
    s = sel01.astype(BF16)
    return _dot(_split3(a), jnp.concatenate([s, s, s], axis=0))


def _select_nt(sel01, a):
    s = sel01.astype(BF16)
    return _dot_nt(jnp.concatenate([s, s, s], axis=1), _split3(a))


def _sigmoid(x):
    return 1.0 / (1.0 + jnp.exp(-x))


def _silu(x):
    return x * _sigmoid(x)


def _softplus(x):
    return jnp.maximum(x, 0.0) + jnp.log(1.0 + jnp.exp(-jnp.abs(x)))


def _iota(shape, dim):
    return lax.broadcasted_iota(jnp.int32, shape, dim)


def _lower_tri(n, strict=False):
    r, c = _iota((n, n), 0), _iota((n, n), 1)
    return (c < r) if strict else (c <= r)


MOD_TN = 1024


def _mod_kernel(c_ref, w_ref, b_ref, o_ref):
    o_ref[...] = _dot(c_ref[...].astype(BF16), w_ref[...].astype(BF16)) + b_ref[...]


def modulation(c_all, w_mod, b_mod):
    rows = c_all.shape[0]
    n = w_mod.shape[-1]
    return pl.pallas_call(
        _mod_kernel,
        grid=(DEPTH, n // MOD_TN),
        in_specs=[pl.BlockSpec((rows, D_MODEL), lambda l, j: (0, 0)),
                  pl.BlockSpec((None, D_MODEL, MOD_TN), lambda l, j: (l, 0, j)),
                  pl.BlockSpec((None, 1, MOD_TN), lambda l, j: (l, 0, j))],
        out_specs=pl.BlockSpec((None, rows, MOD_TN), lambda l, j: (l, 0, j)),
        out_shape=jax.ShapeDtypeStruct((DEPTH, rows, n), F32),
        compiler_params=_cparams(("arbitrary", "arbitrary")),
        name="modulation",
    )(c_all, w_mod, b_mod.reshape(DEPTH, 1, n))


MIXER_BATCH_ROWS = 8


def _batch_rows(bsz, nchunks):
    if nchunks > 1 or bsz % MIXER_BATCH_ROWS:
        return 1
    return MIXER_BATCH_ROWS


def _over_batch(inner, bb, batched):
    def kern(*refs):
        def one(bi):
            inner(*[r.at[bi] if flag else r for r, flag in zip(refs, batched)])
        if bb == 1:
            one(0)
        else:
            def body(bi, carry):
                one(bi)
                return carry
            lax.fori_loop(0, bb, body, 0)
    return kern


def _row_tiling(bsz, seq, target):
    if seq >= target:
        return 1, target
    return min(bsz, target // seq), seq


def _inproj_kernel(*refs, bb, lt, cl, transpose_kv):
    if transpose_kv:
        x_ref, shift_ref, scale_ref, w_ref, ws_ref, _, _, proj_ref, small_ref, kt_ref, vt_ref, u_ref = refs
    else:
        x_ref, shift_ref, scale_ref, w_ref, ws_ref, proj_ref, small_ref, u_ref = refs
    j = pl.program_id(2)
    n_l = lt // cl

    @pl.when(j == 0)
    def _():
        def body(it, carry):
            bi = it // n_l
            r0 = pl.multiple_of((it % n_l) * cl, cl)
            x = x_ref[bi, pl.ds(r0, cl), :]
            mu = jnp.mean(x, axis=-1, keepdims=True)
            xc = x - mu
            var = jnp.mean(xc * xc, axis=-1, keepdims=True)
            u = xc * lax.rsqrt(var + EPS) * (1.0 + scale_ref[bi]) + shift_ref[bi]
            u_ref[pl.ds(pl.multiple_of(it * cl, cl), cl), :] = u.astype(BF16)
            return carry
        lax.fori_loop(0, bb * n_l, body, 0)
        small_ref[...] = _dot_nt(u_ref[...], ws_ref[...]).reshape(bb, lt, SMALL_W)

    if transpose_kv:
        @pl.when(j < G_K)
        def _():
            proj_ref[...] = _dot_nt(u_ref[...], w_ref[...]).reshape(bb, lt, W_GROUP)

        @pl.when(j == G_K)
        def _():
            kt_ref[...] = _dot_nt(w_ref[...], u_ref[...]).reshape(H_B, D_B, lt)

        @pl.when(j == G_V)
        def _():
            vt_ref[...] = _dot_nt(w_ref[...], u_ref[...]).reshape(H_B, D_B, lt)
    else:
        proj_ref[...] = _dot_nt(u_ref[...], w_ref[...]).reshape(bb, lt, W_GROUP)


def in_projection(x, mod, wt_main, wt_small, kv_t=None, layer=0):
    bsz, seq, _ = x.shape
    bb, lt = _row_tiling(bsz, seq, 1024)
    cl = min(lt, 128)
    tm = bb * lt
    transpose_kv = kv_t is not None
    kern = functools.partial(_inproj_kernel, bb=bb, lt=lt, cl=cl, transpose_kv=transpose_kv)
    in_specs = [pl.BlockSpec((bb, lt, D_MODEL), lambda b, l, j: (b, l, 0)),
                pl.BlockSpec((bb, 1, D_MODEL), lambda b, l, j: (b, 0, 0)),
                pl.BlockSpec((bb, 1, D_MODEL), lambda b, l, j: (b, 0, 1)),
                pl.BlockSpec((W_GROUP, D_MODEL), lambda b, l, j: (j, 0)),
                pl.BlockSpec((SMALL_W, D_MODEL), lambda b, l, j: (0, 0))]
    small_spec = pl.BlockSpec((bb, lt, SMALL_W), lambda b, l, j: (b, l, 0))
    small_shape = jax.ShapeDtypeStruct((bsz, seq, SMALL_W), F32)
    scratch = [pltpu.VMEM((tm, D_MODEL), BF16)]
    sem = ("arbitrary", "arbitrary", "arbitrary")
    grid = (bsz // bb, seq // lt, N_MAIN_GROUPS)
    if not transpose_kv:
        return pl.pallas_call(
            kern, grid=grid, in_specs=in_specs,
            out_specs=[pl.BlockSpec((bb, lt, W_GROUP), lambda b, l, j: (b, l, j)), small_spec],
            out_shape=[jax.ShapeDtypeStruct((bsz, seq, N_MAIN_GROUPS * W_GROUP), F32), small_shape],
            scratch_shapes=scratch, compiler_params=_cparams(sem), name="in_projection",
        )(x, mod, mod, wt_main, wt_small)
    assert bb == 1
    any_spec = pl.BlockSpec(memory_space=pl.ANY)
    t_spec = pl.BlockSpec((None, None, H_B, D_B, lt), lambda b, l, j: (layer, b, 0, 0, l))
    t_shape = jax.ShapeDtypeStruct(kv_t[0].shape, F32)
    return pl.pallas_call(
        kern, grid=grid, in_specs=in_specs + [any_spec, any_spec],
        out_specs=[pl.BlockSpec((bb, lt, W_GROUP), lambda b, l, j: (b, l, jnp.minimum(j, G_K - 1))),
                   small_spec, t_spec, t_spec],
        out_shape=[jax.ShapeDtypeStruct((bsz, seq, G_K * W_GROUP), F32), small_shape, t_shape, t_shape],
        input_output_aliases={5: 2, 6: 3},
        scratch_shapes=scratch, compiler_params=_cparams(sem), name="in_projection_kvt",
    )(x, mod, mod, wt_main, wt_small, kv_t[0], kv_t[1])


def _outproj_kernel(ya_ref, yb_ref, yc_ref, yd_ref, x_ref, gate_ref, w_ref, g_ref, b_ref, o_ref, acc_ref,
                    *, bb, lt, cl):
    tm = bb * lt
    acc = _dot(ya_ref[...].reshape(tm, W_GROUP), w_ref[0])
    acc += _dot(yb_ref[...].reshape(tm, W_GROUP), w_ref[1])
    acc += _dot(yc_ref[...].reshape(tm, W_GROUP), w_ref[2])
    acc += _dot(yd_ref[...].reshape(tm, W_GROUP), w_ref[3])
    acc_ref[...] = acc
    n_l = lt // cl

    def body(it, carry):
        bi = it // n_l
        r0 = pl.multiple_of((it % n_l) * cl, cl)
        o = acc_ref[pl.ds(pl.multiple_of(it * cl, cl), cl), :]
        v = ALPHA * x_ref[bi, pl.ds(r0, cl), :] + (1.0 + gate_ref[bi]) * o
        mu = jnp.mean(v, axis=-1, keepdims=True)
        vc = v - mu
        var = jnp.mean(vc * vc, axis=-1, keepdims=True)
        o_ref[bi, pl.ds(r0, cl), :] = vc * lax.rsqrt(var + EPS) * g_ref[...] + b_ref[...]
        return carry
    lax.fori_loop(0, bb * n_l, body, 0)


def out_projection(ys, x, mod, w_out, ln_g, ln_b):
    bsz, seq, _ = x.shape
    bb, lt = _row_tiling(bsz, seq, 512)
    cl = min(lt, 128)
    kern = functools.partial(_outproj_kernel, bb=bb, lt=lt, cl=cl)
    yspec = pl.BlockSpec((bb, lt, W_GROUP), lambda b, l: (b, l, 0))
    return pl.pallas_call(
        kern,
        grid=(bsz // bb, seq // lt),
        in_specs=[yspec, yspec, yspec, yspec,
                  pl.BlockSpec((bb, lt, D_MODEL), lambda b, l: (b, l, 0)),
                  pl.BlockSpec((bb, 1, D_MODEL), lambda b, l: (b, 0, 2)),
                  pl.BlockSpec((4, W_GROUP, D_MODEL), lambda b, l: (0, 0, 0)),
                  pl.BlockSpec((1, D_MODEL), lambda b, l: (0, 0)),
                  pl.BlockSpec((1, D_MODEL), lambda b, l: (0, 0))],
        out_specs=pl.BlockSpec((bb, lt, D_MODEL), lambda b, l: (b, l, 0)),
        out_shape=jax.ShapeDtypeStruct((bsz, seq, D_MODEL), F32),
        scratch_shapes=[pltpu.VMEM((bb * lt, D_MODEL), F32)],
        compiler_params=_cparams(("arbitrary", "arbitrary")),
        name="out_projection",
    )(*ys, x, mod, w_out, ln_g.reshape(1, D_MODEL), ln_b.reshape(1, D_MODEL))


CONV_PAD = 8


def _ssd_kernel(z_ref, x_ref, bc_ref, small_ref, cbuf_ref, ssm0_ref, cw_ref, cb_ref, dtb_ref, alog_ref,
                dskip_ref, nw_ref, y_ref, cnew_ref, ssmnew_ref, ext_ref, state_ref, wcat_ref, xbd_ref,
                ccat_ref, bcat_ref, *, q, nchunks):
    c = pl.program_id(1)
    tail = K_A - 1

    @pl.when(c == 0)
    def _():
        ext_ref[CONV_PAD - tail:CONV_PAD, :] = cbuf_ref[...]
        state_ref[...] = jnp.zeros_like(state_ref)
        for h in range(H_A):
            state_ref[h * P_A:(h + 1) * P_A, h * N_A:(h + 1) * N_A] = ssm0_ref[h]

    ext_ref[CONV_PAD:CONV_PAD + q, 0:W_GROUP] = x_ref[...]
    ext_ref[CONV_PAD:CONV_PAD + q, W_GROUP:] = bc_ref[...]
    conv = cb_ref[...]
    for k in range(K_A):
        off = CONV_PAD - tail + k
        conv = conv + cw_ref[k:k + 1, :] * ext_ref[off:off + q, :]
    new_tail = ext_ref[CONV_PAD + q - tail:CONV_PAD + q, :]
    ext_ref[CONV_PAD - tail:CONV_PAD, :] = new_tail

    @pl.when(c == nchunks - 1)
    def _():
        cnew_ref[...] = new_tail

    xbc = _silu(conv)
    xs = xbc[:, :W_GROUP]
    bm = [xbc[:, W_GROUP + g * N_A:W_GROUP + (g + 1) * N_A] for g in range(G_A)]
    cm = [xbc[:, W_GROUP + (G_A + g) * N_A:W_GROUP + (G_A + g + 1) * N_A] for g in range(G_A)]

    lane = _iota((1, SMALL_W), 1)
    head_lanes = lane < H_A
    dt = jnp.where(head_lanes, _softplus(small_ref[...] + dtb_ref[...]), 0.0)
    a = -jnp.exp(alog_ref[...])
    da = dt * a
    acum = _select_left(_lower_tri(q), da)
    eye8 = _iota((8, SMALL_W), 0) == _iota((8, SMALL_W), 1)
    acum_row = _select_nt(eye8, acum)
    last = acum[q - 1:q, :]
    e_acum = jnp.exp(acum)
    w_s = jnp.exp(last - acum)
    e_last = jnp.exp(last)

    expand = _iota((SMALL_W, W_GROUP), 1) // P_A == _iota((SMALL_W, W_GROUP), 0)
    dt_wide = _select_right(dt, expand)
    xdt = xs * dt_wide
    xdt_bf = xdt.astype(BF16)
    col_head = _iota((1, W_GROUP), 1) // P_A
    causal = _lower_tri(q)

    gmat = [_dot_nt(cm[g].astype(BF16), bm[g].astype(BF16)) for g in range(G_A)]
    for h in range(H_A):
        g = h // (H_A // G_A)
        seg = acum[:, h:h + 1] - acum_row[h:h + 1, :]
        decay = jnp.exp(jnp.where(causal, seg, -jnp.inf))
        wcat_ref[:, h * q:(h + 1) * q] = (gmat[g] * decay).astype(BF16)
        xbd_ref[h * q:(h + 1) * q, :] = jnp.where(col_head == h, xdt, 0.0).astype(BF16)
        ccat_ref[:, h * N_A:(h + 1) * N_A] = (cm[g] * e_acum[:, h:h + 1]).astype(BF16)
        bcat_ref[:, h * N_A:(h + 1) * N_A] = (bm[g] * w_s[:, h:h + 1]).astype(BF16)

    y = _dot(wcat_ref[...], xbd_ref[...])
    y = y + _dot_nt(ccat_ref[...], state_ref[...].astype(BF16))
    y = y + dskip_ref[...] * xs

    upd = _dot_tn(xdt_bf, bcat_ref[...])
    for h in range(H_A):
        rs, cs = slice(h * P_A, (h + 1) * P_A), slice(h * N_A, (h + 1) * N_A)
        state_ref[rs, cs] = e_last[:, h:h + 1] * state_ref[rs, cs] + upd[rs, cs]

    @pl.when(c == nchunks - 1)
    def _():
        for h in range(H_A):
            ssmnew_ref[h] = state_ref[h * P_A:(h + 1) * P_A, h * N_A:(h + 1) * N_A]

    yz = y * _silu(z_ref[...])
    ms = jnp.mean(yz * yz, axis=-1, keepdims=True)
    y_ref[...] = (yz * lax.rsqrt(ms + EPS) * nw_ref[...]).astype(BF16)


def mixer_ssd(proj, small, cbuf, ssm0, conv_w, conv_b, dt_bias, a_log, d_skip, norm_w):
    bsz, seq, _ = proj.shape
    q = min(seq, 256)
    nchunks = seq // q
    pad = lambda v: jnp.zeros((1, SMALL_W), F32).at[0, :v.shape[0]].set(v)
    bb = _batch_rows(bsz, nchunks)
    kern = _over_batch(functools.partial(_ssd_kernel, q=q, nchunks=nchunks), bb,
                       (True,) * 6 + (False,) * 6 + (True,) * 3 + (False,) * 6)
    col = lambda j: pl.BlockSpec((bb, q, W_GROUP), lambda b, c: (b, c, j))
    full = lambda shape: pl.BlockSpec(shape, lambda b, c: (0,) * len(shape))
    return pl.pallas_call(
        kern,
        grid=(bsz // bb, nchunks),
        in_specs=[col(G_ZA), col(G_XA), col(G_BCA),
                  pl.BlockSpec((bb, q, SMALL_W), lambda b, c: (b, c, 0)),
                  pl.BlockSpec((bb, K_A - 1, CONV_DIM_A), lambda b, c: (b, 0, 0)),
                  pl.BlockSpec((bb, H_A, P_A, N_A), lambda b, c: (b, 0, 0, 0)),
                  full((K_A, CONV_DIM_A)), full((1, CONV_DIM_A)), full((1, SMALL_W)), full((1, SMALL_W)),
                  full((1, W_GROUP)), full((1, W_GROUP))],
        out_specs=[pl.BlockSpec((bb, q, W_GROUP), lambda b, c: (b, c, 0)),
                   pl.BlockSpec((bb, K_A - 1, CONV_DIM_A), lambda b, c: (b, 0, 0)),
                   pl.BlockSpec((bb, H_A, P_A, N_A), lambda b, c: (b, 0, 0, 0))],
        out_shape=[jax.ShapeDtypeStruct((bsz, seq, W_GROUP), BF16),
                   jax.ShapeDtypeStruct((bsz, K_A - 1, CONV_DIM_A), F32),
                   jax.ShapeDtypeStruct((bsz, H_A, P_A, N_A), F32)],
        scratch_shapes=[pltpu.VMEM((CONV_PAD + q, CONV_DIM_A), F32),
                        pltpu.VMEM((H_A * P_A, H_A * N_A), F32),
                        pltpu.VMEM((q, H_A * q), BF16),
                        pltpu.VMEM((H_A * q, W_GROUP), BF16),
                        pltpu.VMEM((q, H_A * N_A), BF16),
                        pltpu.VMEM((q, H_A * N_A), BF16)],
        compiler_params=_cparams(("arbitrary", "arbitrary")),
        name="mixer_ssd",
    )(proj, proj, proj, small, cbuf, ssm0, conv_w, conv_b.reshape(1, CONV_DIM_A), pad(dt_bias), pad(a_log),
      jnp.repeat(d_skip, P_A).reshape(1, W_GROUP), norm_w.reshape(1, W_GROUP))


def _mlstm_kernel(xc_ref, zc_ref, small_ref, cbuf_ref, c0_ref, n0_ref, m0_ref, cw_ref, cb_ref, wq_ref, wk_ref,
                  wv_ref, igb_ref, fgb_ref, nw_ref, skip_ref, y_ref, cnew_ref, cst_ref, nst_ref, mst_ref,
                  ext_ref, cs_ref, ns_ref, ms_ref, *, q, nchunks):
    c = pl.program_id(1)
    tail = K_C - 1

    @pl.when(c == 0)
    def _():
        ext_ref[CONV_PAD - tail:CONV_PAD, :] = cbuf_ref[...]
        cs_ref[...] = c0_ref[...]
        ns_ref[...] = n0_ref[...]
        ms_ref[...] = m0_ref[...]

    x_in = xc_ref[...]
    ext_ref[CONV_PAD:CONV_PAD + q, :] = x_in
    conv = cb_ref[...]
    for k in range(K_C):
        off = CONV_PAD - tail + k
        conv = conv + cw_ref[k:k + 1, :] * ext_ref[off:off + q, :]
    new_tail = ext_ref[CONV_PAD + q - tail:CONV_PAD + q, :]
    ext_ref[CONV_PAD - tail:CONV_PAD, :] = new_tail

    @pl.when(c == nchunks - 1)
    def _():
        cnew_ref[...] = new_tail

    xconv = _silu(conv)
    sm = small_ref[...]
    ipre = sm + igb_ref[...]
    fpre = sm + fgb_ref[...]
    logf = jnp.minimum(fpre, 0.0) - jnp.log(1.0 + jnp.exp(-jnp.abs(fpre)))
    bcum = _select_left(_lower_tri(q), logf)
    sel_i = _iota((8, SMALL_W), 1) == _iota((8, SMALL_W), 0) + LANE_I
    sel_f = _iota((8, SMALL_W), 1) == _iota((8, SMALL_W), 0) + LANE_F
    r_row = _select_nt(sel_i, ipre) - _select_nt(sel_f, bcum)
    causal = _lower_tri(q)
    scale_k = DH_C ** -0.5
    heads = range(H_C)
    hs = [slice(h * DH_C, (h + 1) * DH_C) for h in heads]

    xh = [xconv[:, hs[h]].astype(BF16) for h in heads]
    qh = [_dot(xh[h], wq_ref[h]) for h in heads]
    kh = [_dot(xh[h], wk_ref[h]) * scale_k for h in heads]
    vh = [_dot(x_in[:, hs[h]].astype(BF16), wv_ref[h]) for h in heads]
    qb = [t.astype(BF16) for t in qh]
    kb = [t.astype(BF16) for t in kh]
    vb = [t.astype(BF16) for t in vh]
    qk = [_dot_nt(qb[h], kb[h]) for h in heads]
    c_old = [cs_ref[h] for h in heads]
    n_old = [ns_ref[h:h + 1, :] for h in heads]
    qc = [_dot_nt(qb[h], c_old[h].astype(BF16)) for h in heads]

    b_col = [bcum[:, LANE_F + h:LANE_F + h + 1] for h in heads]
    m_prev = [ms_ref[:, h:h + 1] for h in heads]
    d = [jnp.where(causal, b_col[h] + r_row[h:h + 1, :], -jnp.inf) for h in heads]
    inter = [b_col[h] + m_prev[h] for h in heads]
    m_t = [jnp.maximum(inter[h], jnp.max(d[h], axis=-1, keepdims=True)) for h in heads]
    w = [jnp.exp(d[h] - m_t[h]) * qk[h] for h in heads]
    gq = [jnp.exp(inter[h] - m_t[h]) for h in heads]
    m_new = [m_t[h][q - 1:q, :] for h in heads]
    b_last = [b_col[h][q - 1:q, :] for h in heads]
    ws = [jnp.exp(b_last[h] - b_col[h] + ipre[:, LANE_I + h:LANE_I + h + 1] - m_new[h]) for h in heads]
    g_last = [jnp.exp(b_last[h] + m_prev[h] - m_new[h]) for h in heads]
    wv = [_dot(w[h].astype(BF16), vb[h]) for h in heads]
    c_upd = [_dot_tn((vh[h] * ws[h]).astype(BF16), kb[h]) for h in heads]

    for h in heads:
        num = wv[h] + gq[h] * qc[h]
        nq = jnp.sum(w[h], axis=-1, keepdims=True) + gq[h] * jnp.sum(qh[h] * n_old[h], axis=-1, keepdims=True)
        hid = num / jnp.maximum(jnp.abs(nq), jnp.exp(-m_t[h]))
        cs_ref[h] = g_last[h] * c_old[h] + c_upd[h]
        ns_ref[h:h + 1, :] = g_last[h] * n_old[h] + jnp.sum(ws[h] * kh[h], axis=0, keepdims=True)
        ms_ref[:, h:h + 1] = m_new[h]

        mu = jnp.mean(hid, axis=-1, keepdims=True)
        hc = hid - mu
        var = jnp.mean(hc * hc, axis=-1, keepdims=True)
        hn = hc * lax.rsqrt(var + EPS) * nw_ref[:, hs[h]]
        yh = (hn + skip_ref[:, hs[h]] * xconv[:, hs[h]]) * _silu(zc_ref[:, hs[h]])
        y_ref[:, hs[h]] = yh.astype(BF16)

    @pl.when(c == nchunks - 1)
    def _():
        cst_ref[...] = cs_ref[...]
        nst_ref[...] = ns_ref[...]
        mst_ref[...] = ms_ref[...]


def mixer_mlstm(proj, small, cbuf, c0, n0, m0, conv_w, conv_b, wq, wk, wv, ig_bias, fg_bias, norm_w, skip):
    bsz, seq, _ = proj.shape
    q = min(seq, 256)
    nchunks = seq // q
    bb = _batch_rows(bsz, nchunks)
    kern = _over_batch(functools.partial(_mlstm_kernel, q=q, nchunks=nchunks), bb,
                       (True,) * 7 + (False,) * 9 + (True,) * 5 + (False,) * 4)
    pad_at = lambda v, lane: jnp.zeros((1, SMALL_W), F32).at[0, lane:lane + v.shape[0]].set(v)
    col = lambda j: pl.BlockSpec((bb, q, W_GROUP), lambda b, c: (b, c, j))
    full = lambda shape: pl.BlockSpec(shape, lambda b, c: (0,) * len(shape))
    y, cnew, cst, nst, mst = pl.pallas_call(
        kern,
        grid=(bsz // bb, nchunks),
        in_specs=[col(G_XC), col(G_ZC),
                  pl.BlockSpec((bb, q, SMALL_W), lambda b, c: (b, c, 0)),
                  pl.BlockSpec((bb, K_C - 1, W_GROUP), lambda b, c: (b, 0, 0)),
                  pl.BlockSpec((bb, H_C, DH_C, DH_C), lambda b, c: (b, 0, 0, 0)),
                  pl.BlockSpec((bb, H_C, DH_C), lambda b, c: (b, 0, 0)),
                  pl.BlockSpec((bb, 1, H_C), lambda b, c: (b, 0, 0)),
                  full((K_C, W_GROUP)), full((1, W_GROUP)),
                  full((H_C, DH_C, DH_C)), full((H_C, DH_C, DH_C)), full((H_C, DH_C, DH_C)),
                  full((1, SMALL_W)), full((1, SMALL_W)), full((1, W_GROUP)), full((1, W_GROUP))],
        out_specs=[pl.BlockSpec((bb, q, W_GROUP), lambda b, c: (b, c, 0)),
                   pl.BlockSpec((bb, K_C - 1, W_GROUP), lambda b, c: (b, 0, 0)),
                   pl.BlockSpec((bb, H_C, DH_C, DH_C), lambda b, c: (b, 0, 0, 0)),
                   pl.BlockSpec((bb, H_C, DH_C), lambda b, c: (b, 0, 0)),
                   pl.BlockSpec((bb, 1, H_C), lambda b, c: (b, 0, 0))],
        out_shape=[jax.ShapeDtypeStruct((bsz, seq, W_GROUP), BF16),
                   jax.ShapeDtypeStruct((bsz, K_C - 1, W_GROUP), F32),
                   jax.ShapeDtypeStruct((bsz, H_C, DH_C, DH_C), F32),
                   jax.ShapeDtypeStruct((bsz, H_C, DH_C), F32),
                   jax.ShapeDtypeStruct((bsz, 1, H_C), F32)],
        scratch_shapes=[pltpu.VMEM((CONV_PAD + q, W_GROUP), F32),
                        pltpu.VMEM((H_C, DH_C, DH_C), F32),
                        pltpu.VMEM((H_C, DH_C), F32),
                        pltpu.VMEM((1, H_C), F32)],
        compiler_params=_cparams(("arbitrary", "arbitrary")),
        name="mixer_mlstm",
    )(proj, proj, small, cbuf, c0, n0, m0.reshape(bsz, 1, H_C), conv_w, conv_b.reshape(1, W_GROUP),
      wq.astype(BF16), wk.astype(BF16), wv.astype(BF16), pad_at(ig_bias, LANE_I), pad_at(fg_bias, LANE_F),
      norm_w.reshape(1, W_GROUP), skip.reshape(1, W_GROUP))
    return y, cnew, cst, nst, mst.reshape(bsz, H_C)


CONV_D_PAD = 32


def _conf_kernel(a_ref, b_ref, g_ref, cbuf_ref, cw_ref, cb_ref, lg_ref, lb_ref, y_ref, cnew_ref, ext_ref,
                 win_ref, *, q, nchunks):
    c = pl.program_id(1)
    tail = K_D - 1

    @pl.when(c == 0)
    def _():
        ext_ref[CONV_D_PAD - tail:CONV_D_PAD, :] = cbuf_ref[...]

    ext_ref[CONV_D_PAD:CONV_D_PAD + q, :] = a_ref[...] * _sigmoid(b_ref[...])
    conv = cb_ref[...]
    offsets = [CONV_D_PAD - tail + k for k in range(K_D)]
    for r in range(8):
        taps = [k for k in range(K_D) if offsets[k] % 8 == r]
        if not taps:
            continue
        span = max(offsets[k] for k in taps) - r
        if r == 0:
            window_ref = ext_ref
        else:
            window_ref = win_ref.at[r - 1]
            window_ref[0:span + q, :] = ext_ref[r:r + span + q, :]
        for k in taps:
            a = offsets[k] - r
            conv = conv + cw_ref[k:k + 1, :] * window_ref[a:a + q, :]
    new_tail = ext_ref[CONV_D_PAD + q - tail:CONV_D_PAD + q, :]
    ext_ref[CONV_D_PAD - tail:CONV_D_PAD, :] = new_tail

    @pl.when(c == nchunks - 1)
    def _():
        cnew_ref[...] = new_tail

    mu = jnp.mean(conv, axis=-1, keepdims=True)
    cc = conv - mu
    var = jnp.mean(cc * cc, axis=-1, keepdims=True)
    v = cc * lax.rsqrt(var + EPS) * lg_ref[...] + lb_ref[...]
    y_ref[...] = (_silu(v) * _silu(g_ref[...])).astype(BF16)


def mixer_conformer(proj, cbuf, conv_w, conv_b, ln_g, ln_b):
    bsz, seq, _ = proj.shape
    q = min(seq, 256)
    nchunks = seq // q
    bb = _batch_rows(bsz, nchunks)
    kern = _over_batch(functools.partial(_conf_kernel, q=q, nchunks=nchunks), bb,
                       (True,) * 4 + (False,) * 4 + (True,) * 2 + (False,) * 2)
    col = lambda j: pl.BlockSpec((bb, q, W_GROUP), lambda b, c: (b, c, j))
    full = lambda shape: pl.BlockSpec(shape, lambda b, c: (0,) * len(shape))
    return pl.pallas_call(
        kern,
        grid=(bsz // bb, nchunks),
        in_specs=[col(G_AD), col(G_BD), col(G_GD),
                  pl.BlockSpec((bb, K_D - 1, W_GROUP), lambda b, c: (b, 0, 0)),
                  full((K_D, W_GROUP)), full((1, W_GROUP)), full((1, W_GROUP)), full((1, W_GROUP))],
        out_specs=[pl.BlockSpec((bb, q, W_GROUP), lambda b, c: (b, c, 0)),
                   pl.BlockSpec((bb, K_D - 1, W_GROUP), lambda b, c: (b, 0, 0))],
        out_shape=[jax.ShapeDtypeStruct((bsz, seq, W_GROUP), BF16),
                   jax.ShapeDtypeStruct((bsz, K_D - 1, W_GROUP), F32)],
        scratch_shapes=[pltpu.VMEM((CONV_D_PAD + q, W_GROUP), F32),
                        pltpu.VMEM((7, CONV_D_PAD + q, W_GROUP), F32)],
        compiler_params=_cparams(("arbitrary", "arbitrary")),
        name="mixer_conformer",
    )(proj, proj, proj, cbuf, conv_w, conv_b.reshape(1, W_GROUP), ln_g.reshape(1, W_GROUP),
      ln_b.reshape(1, W_GROUP))


ATT_BLOCK = 256


LOG2E = math.log2(math.e)
Q_SCALE = D_B ** -0.5 * LOG2E


def _stick_blocks(z2s, mask, carries, suffix_mat2):
    masks = mask if isinstance(mask, (list, tuple)) else [mask] * len(z2s)
    sps = [jnp.maximum(z2, 0.0) + jnp.log2(1.0 + jnp.exp2(-jnp.abs(z2))) for z2 in z2s]
    if mask is not None:
        sps = [jnp.where(m, sp, 0.0) for m, sp in zip(masks, sps)]
    his = [sp.astype(BF16) for sp in sps]
    los = [(sp - hi.astype(F32)).astype(BF16) for sp, hi in zip(sps, his)]
    suffixes = [_dot(jnp.concatenate([hi, lo], axis=1), suffix_mat2) for hi, lo in zip(his, los)]
    ws = [jnp.exp2(z2 - sp - suffix - carry) for z2, sp, suffix, carry in zip(z2s, sps, suffixes, carries)]
    if mask is not None:
        ws = [jnp.where(m, w, 0.0) for m, w in zip(masks, ws)]
    totals = [suffix[:, 0:1] + sp[:, 0:1] for suffix, sp in zip(suffixes, sps)]
    return ws, [carry + total for carry, total in zip(carries, totals)]


def _suffix_matrix(n):
    m = (jnp.arange(n)[:, None] > jnp.arange(n)[None, :]).astype(BF16)
    return jnp.concatenate([m, m], axis=0)


def _attn_prompt_kernel(q_ref, kt_ref, vt_ref, g_ref, sm_ref, y_ref, kb_ref, vb_ref, z_ref, w_ref, acc_ref,
                        carry_ref, *, t, nblk):
    qi = pl.program_id(2)
    heads = range(kb_ref.shape[0])

    @pl.when(qi == 0)
    def _():
        for hh in heads:
            for blk in range(nblk):
                kb_ref[hh, blk] = kt_ref[hh, :, blk * t:(blk + 1) * t].astype(BF16)
                vb_ref[hh, blk] = vt_ref[hh, :, blk * t:(blk + 1) * t].astype(BF16)

    suffix_mat2 = sm_ref[...]
    qs = [(q_ref[0, :, hh * D_B:(hh + 1) * D_B] * Q_SCALE).astype(BF16) for hh in heads]

    def form_scores(blk, slot):
        for hh in heads:
            z_ref[slot, hh] = _dot(qs[hh], kb_ref[hh, blk])

    def add_values(blk):
        for hh in heads:
            acc_ref[hh] += _dot_nt(w_ref[hh], vb_ref[hh, blk])

    halves = (slice(0, t // 2), slice(t // 2, t))

    def form_weights(z2s, mask):
        tiles = [(hh, rs) for hh in heads for rs in halves]
        masks = None if mask is None else [mask[rs] for _, rs in tiles]
        ws, carries = _stick_blocks([z2s[hh][rs] for hh, rs in tiles], masks,
                                    [carry_ref[hh, rs] for hh, rs in tiles], suffix_mat2)
        for (hh, rs), w, carry in zip(tiles, ws, carries):
            w_ref[hh, rs] = w.astype(BF16)
            carry_ref[hh, rs] = carry

    acc_ref[...] = jnp.zeros_like(acc_ref)
    carry_ref[...] = jnp.zeros_like(carry_ref)
    form_scores(qi, 0)
    form_scores(jnp.maximum(qi - 1, 0), 1)
    form_weights([z_ref[0, hh] for hh in heads], _lower_tri(t, strict=True))

    def body(i, carry):
        blk = qi - 1 - i
        z2s = [z_ref[(i + 1) % 2, hh] for hh in heads]
        add_values(blk + 1)
        form_scores(jnp.maximum(blk - 1, 0), i % 2)
        form_weights(z2s, None)
        return carry
    lax.fori_loop(0, qi, body, 0)
    add_values(0)
    out = jnp.concatenate([acc_ref[hh] for hh in heads], axis=-1)
    y_ref[0] = (out * _silu(g_ref[0])).astype(BF16)


def attention_prompt(proj, kt_all, vt_all, layer):
    bsz, seq, _ = proj.shape
    t = min(seq, ATT_BLOCK)
    nq = seq // t
    lanes = 128
    per = W_GROUP // lanes
    hp = lanes // D_B
    kern = functools.partial(_attn_prompt_kernel, t=t, nblk=nq)
    t_spec = pl.BlockSpec((None, None, hp, D_B, seq), lambda b, p, i: (layer, b, p, 0, 0))
    return pl.pallas_call(
        kern,
        grid=(bsz, per, nq),
        in_specs=[pl.BlockSpec((1, t, lanes), lambda b, p, i: (b, i, G_Q * per + p)),
                  t_spec, t_spec,
                  pl.BlockSpec((1, t, lanes), lambda b, p, i: (b, i, G_GB * per + p)),
                  pl.BlockSpec((2 * t, t), lambda b, p, i: (0, 0))],
        out_specs=pl.BlockSpec((1, t, lanes), lambda b, p, i: (b, i, p)),
        out_shape=jax.ShapeDtypeStruct((bsz, seq, W_GROUP), BF16),
        scratch_shapes=[pltpu.VMEM((hp, nq, D_B, t), BF16),
                        pltpu.VMEM((hp, nq, D_B, t), BF16),
                        pltpu.VMEM((2, hp, t, t), F32),
                        pltpu.VMEM((hp, t, t), BF16),
                        pltpu.VMEM((hp, t, D_B), F32),
                        pltpu.VMEM((hp, t, 1), F32)],
        compiler_params=_cparams(("arbitrary", "arbitrary", "arbitrary")),
        name="attention_prompt",
    )(proj, kt_all, vt_all, proj, _suffix_matrix(t))


CACHE_BLOCK = 1024


def _attn_sample_kernel(q_ref, kn_ref, vn_ref, g_ref, kc_ref, vc_ref, sm_ref, smn_ref, _, __,
                        y_ref, knew_ref, vnew_ref, qb_ref, acc_ref, carry_ref, *, lq, nkb):
    j = pl.program_id(1)
    rows = H_B * lq
    heads = range(H_B)
    hl = lambda h: slice(h * D_B, (h + 1) * D_B)

    @pl.when(j == 0)
    def _():
        kn, vn = kn_ref[0], vn_ref[0]
        q2 = (q_ref[0] * Q_SCALE).astype(BF16)
        knb, vnb = kn.astype(BF16), vn.astype(BF16)
        for h in heads:
            qb_ref[h] = q2[:, hl(h)]
            knew_ref[:, h, :] = kn[:, hl(h)]
            vnew_ref[:, h, :] = vn[:, hl(h)]
        z2 = jnp.concatenate([_dot_nt(q2[:, hl(h)], knb[:, hl(h)]) for h in heads], axis=0)
        mask = _iota((rows, lq), 1) < _iota((rows, lq), 0) % lq
        (w,), (carry,) = _stick_blocks([z2], mask, [jnp.zeros((rows, 1), F32)], smn_ref[...])
        wb = w.astype(BF16)
        for h in heads:
            acc_ref[h] = _dot(wb[h * lq:(h + 1) * lq], vnb[:, hl(h)])
        carry_ref[...] = carry

    suffix_mat2 = sm_ref[...]
    subs = [slice(s * ATT_BLOCK, (s + 1) * ATT_BLOCK) for s in reversed(range(CACHE_BLOCK // ATT_BLOCK))]
    z2s = [jnp.concatenate([_dot(qb_ref[h], kc_ref[h, :, ks].astype(BF16)) for h in heads], axis=0)
           for ks in subs]
    zero = jnp.zeros((rows, 1), F32)
    ws, totals = _stick_blocks(z2s, None, [zero] * len(subs), suffix_mat2)
    carry = carry_ref[...]
    accs = [acc_ref[h] for h in heads]
    for ks, w, total in zip(subs, ws, totals):
        wb = (w * jnp.exp2(-carry)).astype(BF16)
        accs = [accs[h] + _dot_nt(wb[h * lq:(h + 1) * lq], vc_ref[h, :, ks].astype(BF16)) for h in heads]
        carry = carry + total
    for h in heads:
        acc_ref[h] = accs[h]
    carry_ref[...] = carry

    @pl.when(j == nkb - 1)
    def _():
        out = jnp.concatenate(accs, axis=-1)
        y_ref[0] = (out * _silu(g_ref[0])).astype(BF16)


def attention_sample(proj, ktc, vtc, knew_all, vnew_all, layer):
    bsz, lq, _ = proj.shape
    past = ktc.shape[-1]
    nkb = past // CACHE_BLOCK
    kern = functools.partial(_attn_sample_kernel, lq=lq, nkb=nkb)
    col = lambda c: pl.BlockSpec((1, lq, W_GROUP), lambda b, j: (b, 0, c))
    cache = pl.BlockSpec((None, None, H_B, D_B, CACHE_BLOCK), lambda b, j: (layer, b, 0, 0, nkb - 1 - j))
    any_spec = pl.BlockSpec(memory_space=pl.ANY)
    new_spec = pl.BlockSpec((None, None, lq, H_B, D_B), lambda b, j: (layer, b, 0, 0, 0))
    new_shape = jax.ShapeDtypeStruct(knew_all.shape, F32)
    return pl.pallas_call(
        kern,
        grid=(bsz, nkb),
        in_specs=[col(G_Q), col(G_K), col(G_V), col(G_GB), cache, cache,
                  pl.BlockSpec((2 * ATT_BLOCK, ATT_BLOCK), lambda b, j: (0, 0)),
                  pl.BlockSpec((2 * lq, lq), lambda b, j: (0, 0)),
                  any_spec, any_spec],
        out_specs=[pl.BlockSpec((1, lq, W_GROUP), lambda b, j: (b, 0, 0)), new_spec, new_spec],
        out_shape=[jax.ShapeDtypeStruct((bsz, lq, W_GROUP), BF16), new_shape, new_shape],
        input_output_aliases={8: 1, 9: 2},
        scratch_shapes=[pltpu.VMEM((H_B, lq, D_B), BF16),
                        pltpu.VMEM((H_B, lq, D_B), F32),
                        pltpu.VMEM((H_B * lq, 1), F32)],
        compiler_params=_cparams(("arbitrary", "arbitrary")),
        name="attention_sample",
    )(proj, proj, proj, proj, ktc, vtc, _suffix_matrix(ATT_BLOCK), _suffix_matrix(lq), knew_all, vnew_all)


def _repack_w_in(w_in):
    wt = jnp.swapaxes(w_in, -1, -2)
    o_dt = W_GROUP + CONV_DIM_A
    o_q = o_dt + H_A
    o_k, o_v, o_gb, o_xc = o_q + W_GROUP, o_q + 2 * W_GROUP, o_q + 3 * W_GROUP, o_q + 4 * W_GROUP
    o_i = o_xc + 2 * W_GROUP
    o_ad = o_i + 2 * H_C
    rows = lambda a, b: wt[..., a:b, :]
    main = jnp.concatenate([rows(0, o_dt), rows(o_q, o_k), rows(o_gb, o_i), rows(o_ad, o_ad + 3 * W_GROUP),
                            rows(o_k, o_gb)], axis=-2)
    small = jnp.concatenate([rows(o_dt, o_q), rows(o_i, o_ad)], axis=-2)
    small = jnp.pad(small, [(0, 0)] * (small.ndim - 2) + [(0, SMALL_W - small.shape[-2]), (0, 0)])
    return main.astype(BF16), small.astype(BF16)


def _mixer_layer(x, mod, layer, kv_t, kv_cache_t, conv_a_buf, ssm0, conv_c_buf, mc0, mn0, mm0, conv_d_buf, lw):
    if kv_cache_t is None:
        proj, small, kt_all, vt_all = in_projection(x, mod, lw["w_main"], lw["w_small"], kv_t, layer)
        y_b = attention_prompt(proj, kt_all, vt_all, layer)
    else:
        proj, small = in_projection(x, mod, lw["w_main"], lw["w_small"])
        y_b, kt_all, vt_all = attention_sample(proj, kv_cache_t[0], kv_cache_t[1], kv_t[0], kv_t[1], layer)
    y_a, conv_a_new, ssm_new = mixer_ssd(proj, small, conv_a_buf, ssm0, lw["conv_a_w"], lw["conv_a_b"],
                                         lw["dt_bias"], lw["a_log"], lw["d_skip"], lw["norm_a_w"])
    y_c, conv_c_new, mc_new, mn_new, mm_new = mixer_mlstm(
        proj, small, conv_c_buf, mc0, mn0, mm0, lw["conv_c_w"], lw["conv_c_b"], lw["wq_c"], lw["wk_c"],
        lw["wv_c"], lw["ig_bias"], lw["fg_bias"], lw["norm_c_w"], lw["skip_c"])
    y_d, conv_d_new = mixer_conformer(proj, conv_d_buf, lw["conv_d_w"], lw["conv_d_b"], lw["ln_d_g"],
                                      lw["ln_d_b"])
    x_new = out_projection((y_a, y_b, y_c, y_d), x, mod, lw["w_out"], lw["ln_g"], lw["ln_b"])
    return x_new, (kt_all, vt_all), (conv_a_new, ssm_new, conv_c_new, mc_new, mn_new, mm_new, conv_d_new)


def _run_trunk(x, mods, cache_k, cache_v, st_conv_a, st_ssm, st_conv_c, st_mc, st_mn, st_mm, st_conv_d, weights):
    bsz, seq, _ = x.shape
    outs = [[] for _ in range(7)]
    if cache_k is None:
        kv_cache_t = None
        kv_t = (jnp.zeros((DEPTH, bsz, H_B, D_B, seq), F32),) * 2
    else:
        kv_cache_t = (jnp.transpose(cache_k, (0, 1, 3, 4, 2)), jnp.transpose(cache_v, (0, 1, 3, 4, 2)))
        kv_t = (jnp.zeros((DEPTH, bsz, seq, H_B, D_B), F32),) * 2
    for l in range(DEPTH):
        lw = {name: w[l] for name, w in weights.items()}
        x, kv_t, new = _mixer_layer(x, mods[l], l, kv_t, kv_cache_t, st_conv_a[l], st_ssm[l], st_conv_c[l],
                                    st_mc[l], st_mn[l], st_mm[l], st_conv_d[l], lw)
        for o, t in zip(outs, new):
            o.append(t)
    if cache_k is None:
        kv_new = [jnp.transpose(t, (0, 1, 4, 2, 3)) for t in kv_t]
    else:
        kv_new = list(kv_t)
    return x, kv_new + [jnp.stack(o) for o in outs]


def kernel(x_prompt, x_sample, cache_k, cache_v, state_conv_a, state_ssm, state_conv_c, state_mlstm_c,
           state_mlstm_n, state_mlstm_m, state_conv_d, c_prompt, c_sample, w_mod, b_mod, w_in, conv_a_w,
           conv_a_b, dt_bias, a_log, d_skip, norm_a_w, conv_c_w, conv_c_b, wq_c, wk_c, wv_c, ig_bias, fg_bias,
           norm_c_w, skip_c, conv_d_w, conv_d_b, ln_d_g, ln_d_b, w_out, ln_g, ln_b):
    batch, dec_batch = x_prompt.shape[0], x_sample.shape[0]
    w_main, w_small = _repack_w_in(w_in)
    weights = dict(w_main=w_main, w_small=w_small, conv_a_w=conv_a_w, conv_a_b=conv_a_b, dt_bias=dt_bias,
                   a_log=a_log, d_skip=d_skip, norm_a_w=norm_a_w, conv_c_w=conv_c_w, conv_c_b=conv_c_b,
                   wq_c=wq_c, wk_c=wk_c, wv_c=wv_c, ig_bias=ig_bias, fg_bias=fg_bias, norm_c_w=norm_c_w,
                   skip_c=skip_c, conv_d_w=conv_d_w, conv_d_b=conv_d_b, ln_d_g=ln_d_g, ln_d_b=ln_d_b,
                   w_out=w_out.reshape(DEPTH, 4, W_GROUP, D_MODEL).astype(BF16), ln_g=ln_g, ln_b=ln_b)

    rows = batch + dec_batch
    rows_pad = -(-rows // 8) * 8
    c_all = jnp.concatenate([c_prompt, c_sample, jnp.zeros((rows_pad - rows, D_MODEL), F32)], axis=0)
    mod_all = modulation(c_all, w_mod, b_mod)
    mods_p = mod_all[:, :batch].reshape(DEPTH, batch, 1, 3 * D_MODEL)
    mods_s = mod_all[:, batch:rows].reshape(DEPTH, dec_batch, 1, 3 * D_MODEL)

    def zeros(*shape):
        return jnp.zeros((DEPTH, batch) + shape, F32)

    y_prompt, sp = _run_trunk(x_prompt, mods_p, None, None,
                              zeros(K_A - 1, CONV_DIM_A), zeros(H_A, P_A, N_A), zeros(K_C - 1, W_GROUP),
                              zeros(H_C, DH_C, DH_C), zeros(H_C, DH_C), zeros(H_C), zeros(K_D - 1, W_GROUP),
                              weights)
    y_sample, ss = _run_trunk(x_sample, mods_s, cache_k, cache_v, state_conv_a, state_ssm, state_conv_c,
                              state_mlstm_c, state_mlstm_n, state_mlstm_m, state_conv_d, weights)
    return (y_prompt, y_sample, *sp, *ss)
```

```python
import functools
import math

import jax
import jax.numpy as jnp
from jax import lax
from jax.experimental import pallas as pl
from jax.experimental.pallas import tpu as pltpu

D_MODEL = 2048
DEPTH = 4
W_GROUP = 512
H_A, P_A, N_A, G_A, K_A = 8, 64, 128, 2, 4
CONV_DIM_A = W_GROUP + 2 * G_A * N_A
H_B, D_B = 8, 64
H_C, DH_C, K_C = 4, 128, 4
K_D = 31
ALPHA = (2 * DEPTH) ** 0.25
EPS = 1e-5
N_MAIN_GROUPS = 12
G_ZA, G_XA, G_BCA, G_Q, G_GB, G_XC, G_ZC, G_AD, G_BD, G_GD, G_K, G_V = range(N_MAIN_GROUPS)
SMALL_W = 128
LANE_DT, LANE_I, LANE_F = 0, 8, 12

F32 = jnp.float32
BF16 = jnp.bfloat16
HIGHEST = lax.Precision.HIGHEST
VMEM_LIMIT = 56 * 1024 * 1024


def _cparams(sem):
    return pltpu.CompilerParams(dimension_semantics=sem, vmem_limit_bytes=VMEM_LIMIT)


def _dot(a, b):
    return jnp.dot(a, b, preferred_element_type=F32)


def _dot_nt(a, b):
    return lax.dot_general(a, b, (((1,), (1,)), ((), ())), preferred_element_type=F32)


def _dot_tn(a, b):
    return lax.dot_general(a, b, (((0,), (0,)), ((), ())), preferred_element_type=F32)


def _split3(a):
    hi = a.astype(BF16)
    r = a - hi.astype(F32)
    mid = r.astype(BF16)
    lo = (r - mid.astype(F32)).astype(BF16)
    return jnp.concatenate([hi, mid, lo], axis=1)


def _sum3(p):
    n = p.shape[1] // 3
    return (p[:, 2 * n:] + p[:, n:2 * n]) + p[:, :n]


def _select_left(sel01, a):
    return _sum3(_dot(sel01.astype(BF16), _split3(a)))


def _select_right(a, sel01):
    s = sel01.astype(BF16)
    return _dot(_split3(a), jnp.concatenate([s, s, s], axis=0))


def _select_nt(sel01, a):
    s = sel01.astype(BF16)
    return _dot_nt(jnp.concatenate([s, s, s], axis=1), _split3(a))


def _sigmoid(x):
    return 1.0 / (1.0 + jnp.exp(-x))


def _silu(x):
    return x * _sigmoid(x)


def _softplus(x):
    return jnp.maximum(x, 0.0) + jnp.log(1.0 + jnp.exp(-jnp.abs(x)))


def _iota(shape, dim):
    return lax.broadcasted_iota(jnp.int32, shape, dim)


def _lower_tri(n, strict=False):
    r, c = _iota((n, n), 0), _iota((n, n), 1)
    return (c < r) if strict else (c <= r)


MOD_TN = 1024


def _mod_kernel(c_ref, w_ref, b_ref, o_ref):
    o_ref[...] = _dot(c_ref[...].astype(BF16), w_ref[...].astype(BF16)) + b_ref[...]


def modulation(c_all, w_mod, b_mod):
    rows = c_all.shape[0]
    n = w_mod.shape[-1]
    return pl.pallas_call(
        _mod_kernel,
        grid=(DEPTH, n // MOD_TN),
        in_specs=[pl.BlockSpec((rows, D_MODEL), lambda l, j: (0, 0)),
                  pl.BlockSpec((None, D_MODEL, MOD_TN), lambda l, j: (l, 0, j)),
                  pl.BlockSpec((None, 1, MOD_TN), lambda l, j: (l, 0, j))],
        out_specs=pl.BlockSpec((None, rows, MOD_TN), lambda l, j: (l, 0, j)),
        out_shape=jax.ShapeDtypeStruct((DEPTH, rows, n), F32),
        compiler_params=_cparams(("arbitrary", "arbitrary")),
        name="modulation",
    )(c_all, w_mod, b_mod.reshape(DEPTH, 1, n))


MIXER_BATCH_ROWS = 8


def _batch_rows(bsz, nchunks):
    if nchunks > 1 or bsz % MIXER_BATCH_ROWS:
        return 1
    return MIXER_BATCH_ROWS


def _over_batch(inner, bb, batched):
    def kern(*refs):
        def one(bi):
            inner(*[r.at[bi] if flag else r for r, flag in zip(refs, batched)])
        if bb == 1:
            one(0)
        else:
            def body(bi, carry):
                one(bi)
                return carry
            lax.fori_loop(0, bb, body, 0)
    return kern


def _row_tiling(bsz, seq, target):
    if seq >= target:
        return 1, target
    return min(bsz, target // seq), seq


def _inproj_kernel(*refs, bb, lt, cl, transpose_kv):
    if transpose_kv:
        x_ref, shift_ref, scale_ref, w_ref, ws_ref, _, _, proj_ref, small_ref, kt_ref, vt_ref, u_ref = refs
    else:
        x_ref, shift_ref, scale_ref, w_ref, ws_ref, proj_ref, small_ref, u_ref = refs
    j = pl.program_id(2)
    n_l = lt // cl

    @pl.when(j == 0)
    def _():
        def body(it, carry):
            bi = it // n_l
            r0 = pl.multiple_of((it % n_l) * cl, cl)
            x = x_ref[bi, pl.ds(r0, cl), :]
            mu = jnp.mean(x, axis=-1, keepdims=True)
            xc = x - mu
            var = jnp.mean(xc * xc, axis=-1, keepdims=True)
            u = xc * lax.rsqrt(var + EPS) * (1.0 + scale_ref[bi]) + shift_ref[bi]
            u_ref[pl.ds(pl.multiple_of(it * cl, cl), cl), :] = u.astype(BF16)
            return carry
        lax.fori_loop(0, bb * n_l, body, 0)
        small_ref[...] = _dot_nt(u_ref[...], ws_ref[...]).reshape(bb, lt, SMALL_W)

    if transpose_kv:
        @pl.when(j < G_K)
        def _():
            proj_ref[...] = _dot_nt(u_ref[...], w_ref[...]).reshape(bb, lt, W_GROUP)

        @pl.when(j == G_K)
        def _():
            kt_ref[...] = _dot_nt(w_ref[...], u_ref[...]).reshape(H_B, D_B, lt)

        @pl.when(j == G_V)
        def _():
            vt_ref[...] = _dot_nt(w_ref[...], u_ref[...]).reshape(H_B, D_B, lt)
    else:
        proj_ref[...] = _dot_nt(u_ref[...], w_ref[...]).reshape(bb, lt, W_GROUP)


def in_projection(x, mod, wt_main, wt_small, kv_t=None, layer=0):
    bsz, seq, _ = x.shape
    bb, lt = _row_tiling(bsz, seq, 1024)
    cl = min(lt, 128)
    tm = bb * lt
    transpose_kv = kv_t is not None
    kern = functools.partial(_inproj_kernel, bb=bb, lt=lt, cl=cl, transpose_kv=transpose_kv)
    in_specs = [pl.BlockSpec((bb, lt, D_MODEL), lambda b, l, j: (b, l, 0)),
                pl.BlockSpec((bb, 1, D_MODEL), lambda b, l, j: (b, 0, 0)),
                pl.BlockSpec((bb, 1, D_MODEL), lambda b, l, j: (b, 0, 1)),
                pl.BlockSpec((W_GROUP, D_MODEL), lambda b, l, j: (j, 0)),
                pl.BlockSpec((SMALL_W, D_MODEL), lambda b, l, j: (0, 0))]
    small_spec = pl.BlockSpec((bb, lt, SMALL_W), lambda b, l, j: (b, l, 0))
    small_shape = jax.ShapeDtypeStruct((bsz, seq, SMALL_W), F32)
    scratch = [pltpu.VMEM((tm, D_MODEL), BF16)]
    sem = ("arbitrary", "arbitrary", "arbitrary")
    grid = (bsz // bb, seq // lt, N_MAIN_GROUPS)
    if not transpose_kv:
        return pl.pallas_call(
            kern, grid=grid, in_specs=in_specs,
            out_specs=[pl.BlockSpec((bb, lt, W_GROUP), lambda b, l, j: (b, l, j)), small_spec],
            out_shape=[jax.ShapeDtypeStruct((bsz, seq, N_MAIN_GROUPS * W_GROUP), F32), small_shape],
            scratch_shapes=scratch, compiler_params=_cparams(sem), name="in_projection",
        )(x, mod, mod, wt_main, wt_small)
    assert bb == 1
    any_spec = pl.BlockSpec(memory_space=pl.ANY)
    t_spec = pl.BlockSpec((None, None, H_B, D_B, lt), lambda b, l, j: (layer, b, 0, 0, l))
    t_shape = jax.ShapeDtypeStruct(kv_t[0].shape, F32)
    return pl.pallas_call(
        kern, grid=grid, in_specs=in_specs + [any_spec, any_spec],
        out_specs=[pl.BlockSpec((bb, lt, W_GROUP), lambda b, l, j: (b, l, jnp.minimum(j, G_K - 1))),
                   small_spec, t_spec, t_spec],
        out_shape=[jax.ShapeDtypeStruct((bsz, seq, G_K * W_GROUP), F32), small_shape, t_shape, t_shape],
        input_output_aliases={5: 2, 6: 3},
        scratch_shapes=scratch, compiler_params=_cparams(sem), name="in_projection_kvt",
    )(x, mod, mod, wt_main, wt_small, kv_t[0], kv_t[1])


def _outproj_kernel(ya_ref, yb_ref, yc_ref, yd_ref, x_ref, gate_ref, w_ref, g_ref, b_ref, o_ref, acc_ref,
                    *, bb, lt, cl):
    tm = bb * lt
    n_l = lt // cl
    n_chunks = bb * n_l
    halves = 2 if n_chunks % 2 == 0 else 1
    ys = [r[...].reshape(tm, W_GROUP) for r in (ya_ref, yb_ref, yc_ref, yd_ref)]
    hr = tm // halves
    for half in range(halves):
        rows = slice(half * hr, (half + 1) * hr)
        acc = _dot(ys[0][rows], w_ref[0])
        for g in range(1, 4):
            acc += _dot(ys[g][rows], w_ref[g])
        acc_ref[rows, :] = acc

    for it in range(n_chunks):
        bi, r0 = it // n_l, (it % n_l) * cl
        o = acc_ref[it * cl:(it + 1) * cl, :]
        v = ALPHA * x_ref[bi, r0:r0 + cl, :] + (1.0 + gate_ref[bi]) * o
        mu = jnp.mean(v, axis=-1, keepdims=True)
        vc = v - mu
        var = jnp.mean(vc * vc, axis=-1, keepdims=True)
        o_ref[bi, r0:r0 + cl, :] = vc * lax.rsqrt(var + EPS) * g_ref[...] + b_ref[...]


def out_projection(ys, x, mod, w_out, ln_g, ln_b):
    bsz, seq, _ = x.shape
    bb, lt = _row_tiling(bsz, seq, 512)
    cl = min(lt, 128)
    kern = functools.partial(_outproj_kernel, bb=bb, lt=lt, cl=cl)
    yspec = pl.BlockSpec((bb, lt, W_GROUP), lambda b, l: (b, l, 0))
    return pl.pallas_call(
        kern,
        grid=(bsz // bb, seq // lt),
        in_specs=[yspec, yspec, yspec, yspec,
                  pl.BlockSpec((bb, lt, D_MODEL), lambda b, l: (b, l, 0)),
                  pl.BlockSpec((bb, 1, D_MODEL), lambda b, l: (b, 0, 2)),
                  pl.BlockSpec((4, W_GROUP, D_MODEL), lambda b, l: (0, 0, 0)),
                  pl.BlockSpec((1, D_MODEL), lambda b, l: (0, 0)),
                  pl.BlockSpec((1, D_MODEL), lambda b, l: (0, 0))],
        out_specs=pl.BlockSpec((bb, lt, D_MODEL), lambda b, l: (b, l, 0)),
        out_shape=jax.ShapeDtypeStruct((bsz, seq, D_MODEL), F32),
        scratch_shapes=[pltpu.VMEM((bb * lt, D_MODEL), F32)],
        compiler_params=_cparams(("arbitrary", "arbitrary")),
        name="out_projection",
    )(*ys, x, mod, w_out, ln_g.reshape(1, D_MODEL), ln_b.reshape(1, D_MODEL))


CONV_PAD = 8


def _ssd_kernel(z_ref, x_ref, bc_ref, small_ref, cbuf_ref, ssm0_ref, cw_ref, cb_ref, dtb_ref, alog_ref,
                dskip_ref, nw_ref, y_ref, cnew_ref, ssmnew_ref, ext_ref, state_ref, wcat_ref, xbd_ref,
                ccat_ref, bcat_ref, *, q, nchunks):
    c = pl.program_id(1)
    tail = K_A - 1

    @pl.when(c == 0)
    def _():
        ext_ref[CONV_PAD - tail:CONV_PAD, :] = cbuf_ref[...]
        state_ref[...] = jnp.zeros_like(state_ref)
        for h in range(H_A):
            state_ref[h * P_A:(h + 1) * P_A, h * N_A:(h + 1) * N_A] = ssm0_ref[h]

    ext_ref[CONV_PAD:CONV_PAD + q, 0:W_GROUP] = x_ref[...]
    ext_ref[CONV_PAD:CONV_PAD + q, W_GROUP:] = bc_ref[...]
    conv = cb_ref[...]
    for k in range(K_A):
        off = CONV_PAD - tail + k
        conv = conv + cw_ref[k:k + 1, :] * ext_ref[off:off + q, :]
    new_tail = ext_ref[CONV_PAD + q - tail:CONV_PAD + q, :]
    ext_ref[CONV_PAD - tail:CONV_PAD, :] = new_tail

    @pl.when(c == nchunks - 1)
    def _():
        cnew_ref[...] = new_tail

    xbc = _silu(conv)
    xs = xbc[:, :W_GROUP]
    bm = [xbc[:, W_GROUP + g * N_A:W_GROUP + (g + 1) * N_A] for g in range(G_A)]
    cm = [xbc[:, W_GROUP + (G_A + g) * N_A:W_GROUP + (G_A + g + 1) * N_A] for g in range(G_A)]

    lane = _iota((1, SMALL_W), 1)
    head_lanes = lane < H_A
    dt = jnp.where(head_lanes, _softplus(small_ref[...] + dtb_ref[...]), 0.0)
    a = -jnp.exp(alog_ref[...])
    da = dt * a
    acum = _select_left(_lower_tri(q), da)
    eye8 = _iota((8, SMALL_W), 0) == _iota((8, SMALL_W), 1)
    acum_row = _select_nt(eye8, acum)
    last = acum[q - 1:q, :]
    e_acum = jnp.exp(acum)
    w_s = jnp.exp(last - acum)
    e_last = jnp.exp(last)

    expand = _iota((SMALL_W, W_GROUP), 1) // P_A == _iota((SMALL_W, W_GROUP), 0)
    dt_wide = _select_right(dt, expand)
    xdt = xs * dt_wide
    xdt_bf = xdt.astype(BF16)
    col_head = _iota((1, W_GROUP), 1) // P_A
    causal = _lower_tri(q)

    gmat = [_dot_nt(cm[g].astype(BF16), bm[g].astype(BF16)) for g in range(G_A)]
    for h in range(H_A):
        g = h // (H_A // G_A)
        seg = acum[:, h:h + 1] - acum_row[h:h + 1, :]
        decay = jnp.exp(jnp.where(causal, seg, -jnp.inf))
        wcat_ref[:, h * q:(h + 1) * q] = (gmat[g] * decay).astype(BF16)
        xbd_ref[h * q:(h + 1) * q, :] = jnp.where(col_head == h, xdt, 0.0).astype(BF16)
        ccat_ref[:, h * N_A:(h + 1) * N_A] = (cm[g] * e_acum[:, h:h + 1]).astype(BF16)
        bcat_ref[:, h * N_A:(h + 1) * N_A] = (bm[g] * w_s[:, h:h + 1]).astype(BF16)

    y = _dot(wcat_ref[...], xbd_ref[...])
    y = y + _dot_nt(ccat_ref[...], state_ref[...].astype(BF16))
    y = y + dskip_ref[...] * xs

    upd = _dot_tn(xdt_bf, bcat_ref[...])
    for h in range(H_A):
        rs, cs = slice(h * P_A, (h + 1) * P_A), slice(h * N_A, (h + 1) * N_A)
        state_ref[rs, cs] = e_last[:, h:h + 1] * state_ref[rs, cs] + upd[rs, cs]

    @pl.when(c == nchunks - 1)
    def _():
        for h in range(H_A):
            ssmnew_ref[h] = state_ref[h * P_A:(h + 1) * P_A, h * N_A:(h + 1) * N_A]

    yz = y * _silu(z_ref[...])
    ms = jnp.mean(yz * yz, axis=-1, keepdims=True)
    y_ref[...] = (yz * lax.rsqrt(ms + EPS) * nw_ref[...]).astype(BF16)


def mixer_ssd(proj, small, cbuf, ssm0, conv_w, conv_b, dt_bias, a_log, d_skip, norm_w):
    bsz, seq, _ = proj.shape
    q = min(seq, 256)
    nchunks = seq // q
    pad = lambda v: jnp.zeros((1, SMALL_W), F32).at[0, :v.shape[0]].set(v)
    bb = _batch_rows(bsz, nchunks)
    kern = _over_batch(functools.partial(_ssd_kernel, q=q, nchunks=nchunks), bb,
                       (True,) * 6 + (False,) * 6 + (True,) * 3 + (False,) * 6)
    col = lambda j: pl.BlockSpec((bb, q, W_GROUP), lambda b, c: (b, c, j))
    full = lambda shape: pl.BlockSpec(shape, lambda b, c: (0,) * len(shape))
    return pl.pallas_call(
        kern,
        grid=(bsz // bb, nchunks),
        in_specs=[col(G_ZA), col(G_XA), col(G_BCA),
                  pl.BlockSpec((bb, q, SMALL_W), lambda b, c: (b, c, 0)),
                  pl.BlockSpec((bb, K_A - 1, CONV_DIM_A), lambda b, c: (b, 0, 0)),
                  pl.BlockSpec((bb, H_A, P_A, N_A), lambda b, c: (b, 0, 0, 0)),
                  full((K_A, CONV_DIM_A)), full((1, CONV_DIM_A)), full((1, SMALL_W)), full((1, SMALL_W)),
                  full((1, W_GROUP)), full((1, W_GROUP))],
        out_specs=[pl.BlockSpec((bb, q, W_GROUP), lambda b, c: (b, c, 0)),
                   pl.BlockSpec((bb, K_A - 1, CONV_DIM_A), lambda b, c: (b, 0, 0)),
                   pl.BlockSpec((bb, H_A, P_A, N_A), lambda b, c: (b, 0, 0, 0))],
        out_shape=[jax.ShapeDtypeStruct((bsz, seq, W_GROUP), BF16),
                   jax.ShapeDtypeStruct((bsz, K_A - 1, CONV_DIM_A), F32),
                   jax.ShapeDtypeStruct((bsz, H_A, P_A, N_A), F32)],
        scratch_shapes=[pltpu.VMEM((CONV_PAD + q, CONV_DIM_A), F32),
                        pltpu.VMEM((H_A * P_A, H_A * N_A), F32),
                        pltpu.VMEM((q, H_A * q), BF16),
                        pltpu.VMEM((H_A * q, W_GROUP), BF16),
                        pltpu.VMEM((q, H_A * N_A), BF16),
                        pltpu.VMEM((q, H_A * N_A), BF16)],
        compiler_params=_cparams(("arbitrary", "arbitrary")),
        name="mixer_ssd",
    )(proj, proj, proj, small, cbuf, ssm0, conv_w, conv_b.reshape(1, CONV_DIM_A), pad(dt_bias), pad(a_log),
      jnp.repeat(d_skip, P_A).reshape(1, W_GROUP), norm_w.reshape(1, W_GROUP))


def _mlstm_kernel(xc_ref, zc_ref, small_ref, cbuf_ref, c0_ref, n0_ref, m0_ref, cw_ref, cb_ref, wq_ref, wk_ref,
                  wv_ref, igb_ref, fgb_ref, nw_ref, skip_ref, y_ref, cnew_ref, cst_ref, nst_ref, mst_ref,
                  ext_ref, cs_ref, ns_ref, ms_ref, *, q, nchunks):
    c = pl.program_id(1)
    tail = K_C - 1

    @pl.when(c == 0)
    def _():
        ext_ref[CONV_PAD - tail:CONV_PAD, :] = cbuf_ref[...]
        cs_ref[...] = c0_ref[...]
        ns_ref[...] = n0_ref[...]
        ms_ref[...] = m0_ref[...]

    x_in = xc_ref[...]
    ext_ref[CONV_PAD:CONV_PAD + q, :] = x_in
    conv = cb_ref[...]
    for k in range(K_C):
        off = CONV_PAD - tail + k
        conv = conv + cw_ref[k:k + 1, :] * ext_ref[off:off + q, :]
    new_tail = ext_ref[CONV_PAD + q - tail:CONV_PAD + q, :]
    ext_ref[CONV_PAD - tail:CONV_PAD, :] = new_tail

    @pl.when(c == nchunks - 1)
    def _():
        cnew_ref[...] = new_tail

    xconv = _silu(conv)
    sm = small_ref[...]
    ipre = sm + igb_ref[...]
    fpre = sm + fgb_ref[...]
    logf = jnp.minimum(fpre, 0.0) - jnp.log(1.0 + jnp.exp(-jnp.abs(fpre)))
    bcum = _select_left(_lower_tri(q), logf)
    sel_i = _iota((8, SMALL_W), 1) == _iota((8, SMALL_W), 0) + LANE_I
    sel_f = _iota((8, SMALL_W), 1) == _iota((8, SMALL_W), 0) + LANE_F
    r_row = _select_nt(sel_i, ipre) - _select_nt(sel_f, bcum)
    causal = _lower_tri(q)
    scale_k = DH_C ** -0.5
    heads = range(H_C)
    hs = [slice(h * DH_C, (h + 1) * DH_C) for h in heads]

    xh = [xconv[:, hs[h]].astype(BF16) for h in heads]
    qh = [_dot(xh[h], wq_ref[h]) for h in heads]
    kh = [_dot(xh[h], wk_ref[h]) * scale_k for h in heads]
    vh = [_dot(x_in[:, hs[h]].astype(BF16), wv_ref[h]) for h in heads]
    qb = [t.astype(BF16) for t in qh]
    kb = [t.astype(BF16) for t in kh]
    vb = [t.astype(BF16) for t in vh]
    qk = [_dot_nt(qb[h], kb[h]) for h in heads]
    c_old = [cs_ref[h] for h in heads]
    n_old = [ns_ref[h:h + 1, :] for h in heads]
    qc = [_dot_nt(qb[h], c_old[h].astype(BF16)) for h in heads]

    b_col = [bcum[:, LANE_F + h:LANE_F + h + 1] for h in heads]
    m_prev = [ms_ref[:, h:h + 1] for h in heads]
    d = [jnp.where(causal, b_col[h] + r_row[h:h + 1, :], -jnp.inf) for h in heads]
    inter = [b_col[h] + m_prev[h] for h in heads]
    m_t = [jnp.maximum(inter[h], jnp.max(d[h], axis=-1, keepdims=True)) for h in heads]
    w = [jnp.exp(d[h] - m_t[h]) * qk[h] for h in heads]
    gq = [jnp.exp(inter[h] - m_t[h]) for h in heads]
    m_new = [m_t[h][q - 1:q, :] for h in heads]
    b_last = [b_col[h][q - 1:q, :] for h in heads]
    ws = [jnp.exp(b_last[h] - b_col[h] + ipre[:, LANE_I + h:LANE_I + h + 1] - m_new[h]) for h in heads]
    g_last = [jnp.exp(b_last[h] + m_prev[h] - m_new[h]) for h in heads]
    wv = [_dot(w[h].astype(BF16), vb[h]) for h in heads]
    c_upd = [_dot_tn((vh[h] * ws[h]).astype(BF16), kb[h]) for h in heads]

    for h in heads:
        num = wv[h] + gq[h] * qc[h]
        nq = jnp.sum(w[h], axis=-1, keepdims=True) + gq[h] * jnp.sum(qh[h] * n_old[h], axis=-1, keepdims=True)
        hid = num / jnp.maximum(jnp.abs(nq), jnp.exp(-m_t[h]))
        cs_ref[h] = g_last[h] * c_old[h] + c_upd[h]
        ns_ref[h:h + 1, :] = g_last[h] * n_old[h] + jnp.sum(ws[h] * kh[h], axis=0, keepdims=True)
        ms_ref[:, h:h + 1] = m_new[h]

        mu = jnp.mean(hid, axis=-1, keepdims=True)
        hc = hid - mu
        var = jnp.mean(hc * hc, axis=-1, keepdims=True)
        hn = hc * lax.rsqrt(var + EPS) * nw_ref[:, hs[h]]
        yh = (hn + skip_ref[:, hs[h]] * xconv[:, hs[h]]) * _silu(zc_ref[:, hs[h]])
        y_ref[:, hs[h]] = yh.astype(BF16)

    @pl.when(c == nchunks - 1)
    def _():
        cst_ref[...] = cs_ref[...]
        nst_ref[...] = ns_ref[...]
        mst_ref[...] = ms_ref[...]


def mixer_mlstm(proj, small, cbuf, c0, n0, m0, conv_w, conv_b, wq, wk, wv, ig_bias, fg_bias, norm_w, skip):
    bsz, seq, _ = proj.shape
    q = min(seq, 256)
    nchunks = seq // q
    bb = _batch_rows(bsz, nchunks)
    kern = _over_batch(functools.partial(_mlstm_kernel, q=q, nchunks=nchunks), bb,
                       (True,) * 7 + (False,) * 9 + (True,) * 5 + (False,) * 4)
    pad_at = lambda v, lane: jnp.zeros((1, SMALL_W), F32).at[0, lane:lane + v.shape[0]].set(v)
    col = lambda j: pl.BlockSpec((bb, q, W_GROUP), lambda b, c: (b, c, j))
    full = lambda shape: pl.BlockSpec(shape, lambda b, c: (0,) * len(shape))
    y, cnew, cst, nst, mst = pl.pallas_call(
        kern,
        grid=(bsz // bb, nchunks),
        in_specs=[col(G_XC), col(G_ZC),
                  pl.BlockSpec((bb, q, SMALL_W), lambda b, c: (b, c, 0)),
                  pl.BlockSpec((bb, K_C - 1, W_GROUP), lambda b, c: (b, 0, 0)),
                  pl.BlockSpec((bb, H_C, DH_C, DH_C), lambda b, c: (b, 0, 0, 0)),
                  pl.BlockSpec((bb, H_C, DH_C), lambda b, c: (b, 0, 0)),
                  pl.BlockSpec((bb, 1, H_C), lambda b, c: (b, 0, 0)),
                  full((K_C, W_GROUP)), full((1, W_GROUP)),
                  full((H_C, DH_C, DH_C)), full((H_C, DH_C, DH_C)), full((H_C, DH_C, DH_C)),
                  full((1, SMALL_W)), full((1, SMALL_W)), full((1, W_GROUP)), full((1, W_GROUP))],
        out_specs=[pl.BlockSpec((bb, q, W_GROUP), lambda b, c: (b, c, 0)),
                   pl.BlockSpec((bb, K_C - 1, W_GROUP), lambda b, c: (b, 0, 0)),
                   pl.BlockSpec((bb, H_C, DH_C, DH_C), lambda b, c: (b, 0, 0, 0)),
                   pl.BlockSpec((bb, H_C, DH_C), lambda b, c: (b, 0, 0)),
                   pl.BlockSpec((bb, 1, H_C), lambda b, c: (b, 0, 0))],
        out_shape=[jax.ShapeDtypeStruct((bsz, seq, W_GROUP), BF16),
                   jax.ShapeDtypeStruct((bsz, K_C - 1, W_GROUP), F32),
                   jax.ShapeDtypeStruct((bsz, H_C, DH_C, DH_C), F32),
                   jax.ShapeDtypeStruct((bsz, H_C, DH_C), F32),
                   jax.ShapeDtypeStruct((bsz, 1, H_C), F32)],
        scratch_shapes=[pltpu.VMEM((CONV_PAD + q, W_GROUP), F32),
                        pltpu.VMEM((H_C, DH_C, DH_C), F32),
                        pltpu.VMEM((H_C, DH_C), F32),
                        pltpu.VMEM((1, H_C), F32)],
        compiler_params=_cparams(("arbitrary", "arbitrary")),
        name="mixer_mlstm",
    )(proj, proj, small, cbuf, c0, n0, m0.reshape(bsz, 1, H_C), conv_w, conv_b.reshape(1, W_GROUP),
      wq.astype(BF16), wk.astype(BF16), wv.astype(BF16), pad_at(ig_bias, LANE_I), pad_at(fg_bias, LANE_F),
      norm_w.reshape(1, W_GROUP), skip.reshape(1, W_GROUP))
    return y, cnew, cst, nst, mst.reshape(bsz, H_C)


CONV_D_PAD = 32


def _conf_kernel(a_ref, b_ref, g_ref, cbuf_ref, cw_ref, cb_ref, lg_ref, lb_ref, y_ref, cnew_ref, ext_ref,
                 win_ref, *, q, nchunks):
    c = pl.program_id(1)
    tail = K_D - 1

    @pl.when(c == 0)
    def _():
        ext_ref[CONV_D_PAD - tail:CONV_D_PAD, :] = cbuf_ref[...]

    ext_ref[CONV_D_PAD:CONV_D_PAD + q, :] = a_ref[...] * _sigmoid(b_ref[...])
    conv = cb_ref[...]
    offsets = [CONV_D_PAD - tail + k for k in range(K_D)]
    for r in range(8):
        taps = [k for k in range(K_D) if offsets[k] % 8 == r]
        if not taps:
            continue
        span = max(offsets[k] for k in taps) - r
        if r == 0:
            window_ref = ext_ref
        else:
            window_ref = win_ref.at[r - 1]
            window_ref[0:span + q, :] = ext_ref[r:r + span + q, :]
        for k in taps:
            a = offsets[k] - r
            conv = conv + cw_ref[k:k + 1, :] * window_ref[a:a + q, :]
    new_tail = ext_ref[CONV_D_PAD + q - tail:CONV_D_PAD + q, :]
    ext_ref[CONV_D_PAD - tail:CONV_D_PAD, :] = new_tail

    @pl.when(c == nchunks - 1)
    def _():
        cnew_ref[...] = new_tail

    mu = jnp.mean(conv, axis=-1, keepdims=True)
    cc = conv - mu
    var = jnp.mean(cc * cc, axis=-1, keepdims=True)
    v = cc * lax.rsqrt(var + EPS) * lg_ref[...] + lb_ref[...]
    y_ref[...] = (_silu(v) * _silu(g_ref[...])).astype(BF16)


def mixer_conformer(proj, cbuf, conv_w, conv_b, ln_g, ln_b):
    bsz, seq, _ = proj.shape
    q = min(seq, 256)
    nchunks = seq // q
    bb = _batch_rows(bsz, nchunks)
    kern = _over_batch(functools.partial(_conf_kernel, q=q, nchunks=nchunks), bb,
                       (True,) * 4 + (False,) * 4 + (True,) * 2 + (False,) * 2)
    col = lambda j: pl.BlockSpec((bb, q, W_GROUP), lambda b, c: (b, c, j))
    full = lambda shape: pl.BlockSpec(shape, lambda b, c: (0,) * len(shape))
    return pl.pallas_call(
        kern,
        grid=(bsz // bb, nchunks),
        in_specs=[col(G_AD), col(G_BD), col(G_GD),
                  pl.BlockSpec((bb, K_D - 1, W_GROUP), lambda b, c: (b, 0, 0)),
                  full((K_D, W_GROUP)), full((1, W_GROUP)), full((1, W_GROUP)), full((1, W_GROUP))],
        out_specs=[pl.BlockSpec((bb, q, W_GROUP), lambda b, c: (b, c, 0)),
                   pl.BlockSpec((bb, K_D - 1, W_GROUP), lambda b, c: (b, 0, 0))],
        out_shape=[jax.ShapeDtypeStruct((bsz, seq, W_GROUP), BF16),
                   jax.ShapeDtypeStruct((bsz, K_D - 1, W_GROUP), F32)],
        scratch_shapes=[pltpu.VMEM((CONV_D_PAD + q, W_GROUP), F32),
                        pltpu.VMEM((7, CONV_D_PAD + q, W_GROUP), F32)],
        compiler_params=_cparams(("arbitrary", "arbitrary")),
        name="mixer_conformer",
    )(proj, proj, proj, cbuf, conv_w, conv_b.reshape(1, W_GROUP), ln_g.reshape(1, W_GROUP),
      ln_b.reshape(1, W_GROUP))


ATT_BLOCK = 256


LOG2E = math.log2(math.e)
Q_SCALE = D_B ** -0.5 * LOG2E


def _stick_blocks(z2s, mask, carries, suffix_mat):
    masks = mask if isinstance(mask, (list, tuple)) else [mask] * len(z2s)
    sps = [jnp.maximum(z2, 0.0) + jnp.log2(1.0 + jnp.exp2(-jnp.abs(z2))) for z2 in z2s]
    if mask is not None:
        sps = [jnp.where(m, sp, 0.0) for m, sp in zip(masks, sps)]
    suffixes = [_dot(sp.astype(BF16), suffix_mat) for sp in sps]
    ws = [jnp.exp2(z2 - sp - suffix - carry) for z2, sp, suffix, carry in zip(z2s, sps, suffixes, carries)]
    if mask is not None:
        ws = [jnp.where(m, w, 0.0) for m, w in zip(masks, ws)]
    totals = [suffix[:, 0:1] + sp[:, 0:1] for suffix, sp in zip(suffixes, sps)]
    return ws, [carry + total for carry, total in zip(carries, totals)]


def _suffix_matrix(n):
    return (jnp.arange(n)[:, None] > jnp.arange(n)[None, :]).astype(BF16)


def _attn_prompt_kernel(q_ref, kt_ref, vt_ref, g_ref, sm_ref, y_ref, kb_ref, vb_ref, z_ref, w_ref, acc_ref,
                        carry_ref, *, t, nblk):
    qi = pl.program_id(2)
    heads = range(kb_ref.shape[0])

    @pl.when(qi == 0)
    def _():
        for hh in heads:
            for blk in range(nblk):
                kb_ref[hh, blk] = kt_ref[hh, :, blk * t:(blk + 1) * t].astype(BF16)
                vb_ref[hh, blk] = vt_ref[hh, :, blk * t:(blk + 1) * t].astype(BF16)

    suffix_mat = sm_ref[...]
    qs = [(q_ref[0, :, hh * D_B:(hh + 1) * D_B] * Q_SCALE).astype(BF16) for hh in heads]

    def form_scores(blk, slot):
        for hh in heads:
            z_ref[slot, hh] = _dot(qs[hh], kb_ref[hh, blk])

    def add_values(blk):
        for hh in heads:
            acc_ref[hh] += _dot_nt(w_ref[hh], vb_ref[hh, blk])

    halves = (slice(0, t // 2), slice(t // 2, t))

    def form_weights(z2s, mask):
        tiles = [(hh, rs) for hh in heads for rs in halves]
        masks = None if mask is None else [mask[rs] for _, rs in tiles]
        ws, carries = _stick_blocks([z2s[hh][rs] for hh, rs in tiles], masks,
                                    [carry_ref[hh, rs] for hh, rs in tiles], suffix_mat)
        for (hh, rs), w, carry in zip(tiles, ws, carries):
            w_ref[hh, rs] = w.astype(BF16)
            carry_ref[hh, rs] = carry

    acc_ref[...] = jnp.zeros_like(acc_ref)
    carry_ref[...] = jnp.zeros_like(carry_ref)
    form_scores(qi, 0)
    form_scores(jnp.maximum(qi - 1, 0), 1)
    form_weights([z_ref[0, hh] for hh in heads], _lower_tri(t, strict=True))

    def body(i, carry):
        blk = qi - 1 - i
        z2s = [z_ref[(i + 1) % 2, hh] for hh in heads]
        add_values(blk + 1)
        form_scores(jnp.maximum(blk - 1, 0), i % 2)
        form_weights(z2s, None)
        return carry
    lax.fori_loop(0, qi, body, 0)
    add_values(0)
    out = jnp.concatenate([acc_ref[hh] for hh in heads], axis=-1)
    y_ref[0] = (out * _silu(g_ref[0])).astype(BF16)


def attention_prompt(proj, kt_all, vt_all, layer):
    bsz, seq, _ = proj.shape
    t = min(seq, ATT_BLOCK)
    nq = seq // t
    lanes = 128
    per = W_GROUP // lanes
    hp = lanes // D_B
    kern = functools.partial(_attn_prompt_kernel, t=t, nblk=nq)
    t_spec = pl.BlockSpec((None, None, hp, D_B, seq), lambda b, p, i: (layer, b, p, 0, 0))
    return pl.pallas_call(
        kern,
        grid=(bsz, per, nq),
        in_specs=[pl.BlockSpec((1, t, lanes), lambda b, p, i: (b, i, G_Q * per + p)),
                  t_spec, t_spec,
                  pl.BlockSpec((1, t, lanes), lambda b, p, i: (b, i, G_GB * per + p)),
                  pl.BlockSpec((t, t), lambda b, p, i: (0, 0))],
        out_specs=pl.BlockSpec((1, t, lanes), lambda b, p, i: (b, i, p)),
        out_shape=jax.ShapeDtypeStruct((bsz, seq, W_GROUP), BF16),
        scratch_shapes=[pltpu.VMEM((hp, nq, D_B, t), BF16),
                        pltpu.VMEM((hp, nq, D_B, t), BF16),
                        pltpu.VMEM((2, hp, t, t), F32),
                        pltpu.VMEM((hp, t, t), BF16),
                        pltpu.VMEM((hp, t, D_B), F32),
                        pltpu.VMEM((hp, t, 1), F32)],
        compiler_params=_cparams(("arbitrary", "arbitrary", "arbitrary")),
        name="attention_prompt",
    )(proj, kt_all, vt_all, proj, _suffix_matrix(t))


CACHE_BLOCK = 1024


def _attn_sample_kernel(q_ref, kn_ref, vn_ref, g_ref, kc_ref, vc_ref, sm_ref, smn_ref, _, __,
                        y_ref, knew_ref, vnew_ref, qb_ref, acc_ref, carry_ref, *, lq, nkb):
    j = pl.program_id(1)
    rows = H_B * lq
    heads = range(H_B)
    hl = lambda h: slice(h * D_B, (h + 1) * D_B)

    @pl.when(j == 0)
    def _():
        kn, vn = kn_ref[0], vn_ref[0]
        q2 = (q_ref[0] * Q_SCALE).astype(BF16)
        knb, vnb = kn.astype(BF16), vn.astype(BF16)
        for h in heads:
            qb_ref[h] = q2[:, hl(h)]
            knew_ref[:, h, :] = kn[:, hl(h)]
            vnew_ref[:, h, :] = vn[:, hl(h)]
        z2 = jnp.concatenate([_dot_nt(q2[:, hl(h)], knb[:, hl(h)]) for h in heads], axis=0)
        mask = _iota((rows, lq), 1) < _iota((rows, lq), 0) % lq
        (w,), (carry,) = _stick_blocks([z2], mask, [jnp.zeros((rows, 1), F32)], smn_ref[...])
        wb = w.astype(BF16)
        for h in heads:
            acc_ref[h] = _dot(wb[h * lq:(h + 1) * lq], vnb[:, hl(h)])
        carry_ref[...] = carry

    suffix_mat = sm_ref[...]
    subs = [slice(s * ATT_BLOCK, (s + 1) * ATT_BLOCK) for s in reversed(range(CACHE_BLOCK // ATT_BLOCK))]
    z2s = [jnp.concatenate([_dot(qb_ref[h], kc_ref[h, :, ks].astype(BF16)) for h in heads], axis=0)
           for ks in subs]
    zero = jnp.zeros((rows, 1), F32)
    ws, totals = _stick_blocks(z2s, None, [zero] * len(subs), suffix_mat)
    carry = carry_ref[...]
    accs = [acc_ref[h] for h in heads]
    for ks, w, total in zip(subs, ws, totals):
        wb = (w * jnp.exp2(-carry)).astype(BF16)
        accs = [accs[h] + _dot_nt(wb[h * lq:(h + 1) * lq], vc_ref[h, :, ks].astype(BF16)) for h in heads]
        carry = carry + total
    for h in heads:
        acc_ref[h] = accs[h]
    carry_ref[...] = carry

    @pl.when(j == nkb - 1)
    def _():
        out = jnp.concatenate(accs, axis=-1)
        y_ref[0] = (out * _silu(g_ref[0])).astype(BF16)


def attention_sample(proj, ktc, vtc, knew_all, vnew_all, layer):
    bsz, lq, _ = proj.shape
    past = ktc.shape[-1]
    nkb = past // CACHE_BLOCK
    kern = functools.partial(_attn_sample_kernel, lq=lq, nkb=nkb)
    col = lambda c: pl.BlockSpec((1, lq, W_GROUP), lambda b, j: (b, 0, c))
    cache = pl.BlockSpec((None, None, H_B, D_B, CACHE_BLOCK), lambda b, j: (layer, b, 0, 0, nkb - 1 - j))
    any_spec = pl.BlockSpec(memory_space=pl.ANY)
    new_spec = pl.BlockSpec((None, None, lq, H_B, D_B), lambda b, j: (layer, b, 0, 0, 0))
    new_shape = jax.ShapeDtypeStruct(knew_all.shape, F32)
    return pl.pallas_call(
        kern,
        grid=(bsz, nkb),
        in_specs=[col(G_Q), col(G_K), col(G_V), col(G_GB), cache, cache,
                  pl.BlockSpec((ATT_BLOCK, ATT_BLOCK), lambda b, j: (0, 0)),
                  pl.BlockSpec((lq, lq), lambda b, j: (0, 0)),
                  any_spec, any_spec],
        out_specs=[pl.BlockSpec((1, lq, W_GROUP), lambda b, j: (b, 0, 0)), new_spec, new_spec],
        out_shape=[jax.ShapeDtypeStruct((bsz, lq, W_GROUP), BF16), new_shape, new_shape],
        input_output_aliases={8: 1, 9: 2},
        scratch_shapes=[pltpu.VMEM((H_B, lq, D_B), BF16),
                        pltpu.VMEM((H_B, lq, D_B), F32),
                        pltpu.VMEM((H_B * lq, 1), F32)],
        compiler_params=_cparams(("arbitrary", "arbitrary")),
        name="attention_sample",
    )(proj, proj, proj, proj, ktc, vtc, _suffix_matrix(ATT_BLOCK), _suffix_matrix(lq), knew_all, vnew_all)


SUBLANES = 8


def _repack_kernel(starts_ref, w_ref, o_ref):
    o_ref[...] = w_ref[0].astype(BF16)


def _repack_w_in(w_in):
    layers = w_in.shape[0]
    wt = jnp.swapaxes(w_in, -1, -2)
    o_dt = W_GROUP + CONV_DIM_A
    o_q = o_dt + H_A
    o_k, o_v, o_gb, o_xc = o_q + W_GROUP, o_q + 2 * W_GROUP, o_q + 3 * W_GROUP, o_q + 4 * W_GROUP
    o_i = o_xc + 2 * W_GROUP
    o_ad = o_i + 2 * H_C
    starts = [0, W_GROUP, 2 * W_GROUP, o_q, o_gb, o_xc, o_xc + W_GROUP, o_ad, o_ad + W_GROUP, o_ad + 2 * W_GROUP,
              o_k, o_v]
    assert all(s % SUBLANES == 0 for s in starts)
    tiles = jnp.asarray([s // SUBLANES for s in starts], jnp.int32)
    main = pl.pallas_call(
        _repack_kernel,
        grid_spec=pltpu.PrefetchScalarGridSpec(
            num_scalar_prefetch=1, grid=(layers, N_MAIN_GROUPS),
            in_specs=[pl.BlockSpec((pl.Element(1), pl.Element(W_GROUP), pl.Element(D_MODEL)),
                                   lambda l, g, tiles: (l, tiles[g] * SUBLANES, 0))],
            out_specs=pl.BlockSpec((None, W_GROUP, D_MODEL), lambda l, g, tiles: (l, g, 0))),
        out_shape=jax.ShapeDtypeStruct((layers, N_MAIN_GROUPS * W_GROUP, D_MODEL), BF16),
        compiler_params=_cparams(("arbitrary", "arbitrary")),
        name="repack_w_in",
    )(tiles, wt)
    small = jnp.concatenate([wt[:, o_dt:o_q], wt[:, o_i:o_ad]], axis=1)
    small = jnp.pad(small, [(0, 0), (0, SMALL_W - small.shape[1]), (0, 0)])
    return main, small.astype(BF16)


def _mixer_layer(x, mod, layer, kv_t, kv_cache_t, conv_a_buf, ssm0, conv_c_buf, mc0, mn0, mm0, conv_d_buf, lw):
    if kv_cache_t is None:
        proj, small, kt_all, vt_all = in_projection(x, mod, lw["w_main"], lw["w_small"], kv_t, layer)
        y_b = attention_prompt(proj, kt_all, vt_all, layer)
    else:
        proj, small = in_projection(x, mod, lw["w_main"], lw["w_small"])
        y_b, kt_all, vt_all = attention_sample(proj, kv_cache_t[0], kv_cache_t[1], kv_t[0], kv_t[1], layer)
    y_a, conv_a_new, ssm_new = mixer_ssd(proj, small, conv_a_buf, ssm0, lw["conv_a_w"], lw["conv_a_b"],
                                         lw["dt_bias"], lw["a_log"], lw["d_skip"], lw["norm_a_w"])
    y_c, conv_c_new, mc_new, mn_new, mm_new = mixer_mlstm(
        proj, small, conv_c_buf, mc0, mn0, mm0, lw["conv_c_w"], lw["conv_c_b"], lw["wq_c"], lw["wk_c"],
        lw["wv_c"], lw["ig_bias"], lw["fg_bias"], lw["norm_c_w"], lw["skip_c"])
    y_d, conv_d_new = mixer_conformer(proj, conv_d_buf, lw["conv_d_w"], lw["conv_d_b"], lw["ln_d_g"],
                                      lw["ln_d_b"])
    x_new = out_projection((y_a, y_b, y_c, y_d), x, mod, lw["w_out"], lw["ln_g"], lw["ln_b"])
    return x_new, (kt_all, vt_all), (conv_a_new, ssm_new, conv_c_new, mc_new, mn_new, mm_new, conv_d_new)


def _run_trunk(x, mods, cache_k, cache_v, st_conv_a, st_ssm, st_conv_c, st_mc, st_mn, st_mm, st_conv_d, weights):
    bsz, seq, _ = x.shape
    outs = [[] for _ in range(7)]
    if cache_k is None:
        kv_cache_t = None
        kv_t = (jnp.zeros((DEPTH, bsz, H_B, D_B, seq), F32),) * 2
    else:
        kv_cache_t = (jnp.transpose(cache_k, (0, 1, 3, 4, 2)), jnp.transpose(cache_v, (0, 1, 3, 4, 2)))
        kv_t = (jnp.zeros((DEPTH, bsz, seq, H_B, D_B), F32),) * 2
    for l in range(DEPTH):
        lw = {name: w[l] for name, w in weights.items()}
        x, kv_t, new = _mixer_layer(x, mods[l], l, kv_t, kv_cache_t, st_conv_a[l], st_ssm[l], st_conv_c[l],
                                    st_mc[l], st_mn[l], st_mm[l], st_conv_d[l], lw)
        for o, t in zip(outs, new):
            o.append(t)
    if cache_k is None:
        kv_new = [jnp.transpose(t, (0, 1, 4, 2, 3)) for t in kv_t]
    else:
        kv_new = list(kv_t)
    return x, kv_new + [jnp.stack(o) for o in outs]


def kernel(x_prompt, x_sample, cache_k, cache_v, state_conv_a, state_ssm, state_conv_c, state_mlstm_c,
           state_mlstm_n, state_mlstm_m, state_conv_d, c_prompt, c_sample, w_mod, b_mod, w_in, conv_a_w,
           conv_a_b, dt_bias, a_log, d_skip, norm_a_w, conv_c_w, conv_c_b, wq_c, wk_c, wv_c, ig_bias, fg_bias,
           norm_c_w, skip_c, conv_d_w, conv_d_b, ln_d_g, ln_d_b, w_out, ln_g, ln_b):
    batch, dec_batch = x_prompt.shape[0], x_sample.shape[0]
    w_main, w_small = _repack_w_in(w_in)
    weights = dict(w_main=w_main, w_small=w_small, conv_a_w=conv_a_w, conv_a_b=conv_a_b, dt_bias=dt_bias,
                   a_log=a_log, d_skip=d_skip, norm_a_w=norm_a_w, conv_c_w=conv_c_w, conv_c_b=conv_c_b,
                   wq_c=wq_c, wk_c=wk_c, wv_c=wv_c, ig_bias=ig_bias, fg_bias=fg_bias, norm_c_w=norm_c_w,
                   skip_c=skip_c, conv_d_w=conv_d_w, conv_d_b=conv_d_b, ln_d_g=ln_d_g, ln_d_b=ln_d_b,
                   w_out=w_out.reshape(DEPTH, 4, W_GROUP, D_MODEL).astype(BF16), ln_g=ln_g, ln_b=ln_b)

    rows = batch + dec_batch
    rows_pad = -(-rows // 8) * 8
    c_all = jnp.concatenate([c_prompt, c_sample, jnp.zeros((rows_pad - rows, D_MODEL), F32)], axis=0)
    mod_all = modulation(c_all, w_mod, b_mod)
    mods_p = mod_all[:, :batch].reshape(DEPTH, batch, 1, 3 * D_MODEL)
    mods_s = mod_all[:, batch:rows].reshape(DEPTH, dec_batch, 1, 3 * D_MODEL)

    def zeros(*shape):
        return jnp.zeros((DEPTH, batch) + shape, F32)

    y_prompt, sp = _run_trunk(x_prompt, mods_p, None, None,
                              zeros(K_A - 1, CONV_DIM_A), zeros(H_A, P_A, N_A), zeros(K_C - 1, W_GROUP),
                              zeros(H_C, DH_C, DH_C), zeros(H_C, DH_C), zeros(H_C), zeros(K_D - 1, W_GROUP),
                              weights)
    y_sample, ss = _run_trunk(x_sample, mods_s, cache_k, cache_v, state_conv_a, state_ssm, state_conv_c,
                              state_mlstm_c, state_mlstm_n, state_mlstm_m, state_conv_d, weights)
    return (y_prompt, y_sample, *sp, *ss)
```

```python
import functools
import math

import jax
import jax.numpy as jnp
from jax import lax
from jax.experimental import pallas as pl
from jax.experimental.pallas import tpu as pltpu

D_MODEL = 2048
DEPTH = 4
W_GROUP = 512
H_A, P_A, N_A, G_A, K_A = 8, 64, 128, 2, 4
CONV_DIM_A = W_GROUP + 2 * G_A * N_A
H_B, D_B = 8, 64
H_C, DH_C, K_C = 4, 128, 4
K_D = 31
ALPHA = (2 * DEPTH) ** 0.25
EPS = 1e-5
N_MAIN_GROUPS = 12
G_ZA, G_XA, G_BCA, G_Q, G_GB, G_XC, G_ZC, G_AD, G_BD, G_GD, G_K, G_V = range(N_MAIN_GROUPS)
SMALL_W = 128
LANE_DT, LANE_I, LANE_F = 0, 8, 12

F32 = jnp.float32
BF16 = jnp.bfloat16
HIGHEST = lax.Precision.HIGHEST
VMEM_LIMIT = 56 * 1024 * 1024


def _cparams(sem):
    return pltpu.CompilerParams(dimension_semantics=sem, vmem_limit_bytes=VMEM_LIMIT)


def _dot(a, b):
    return jnp.dot(a, b, preferred_element_type=F32)


def _dot_nt(a, b):
    return lax.dot_general(a, b, (((1,), (1,)), ((), ())), preferred_element_type=F32)


def _dot_tn(a, b):
    return lax.dot_general(a, b, (((0,), (0,)), ((), ())), preferred_element_type=F32)


def _split3(a):
    hi = a.astype(BF16)
    r = a - hi.astype(F32)
    mid = r.astype(BF16)
    lo = (r - mid.astype(F32)).astype(BF16)
    return jnp.concatenate([hi, mid, lo], axis=1)


def _sum3(p):
    n = p.shape[1] // 3
    return (p[:, 2 * n:] + p[:, n:2 * n]) + p[:, :n]


def _select_left(sel01, a):
    return _sum3(_dot(sel01.astype(BF16), _split3(a)))


def _select_right(a, sel01):
    s = sel01.astype(BF16)
    return _dot(_split3(a), jnp.concatenate([s, s, s], axis=0))


def _select_nt(sel01, a):
    s = sel01.astype(BF16)
    return _dot_nt(jnp.concatenate([s, s, s], axis=1), _split3(a))


def _sigmoid(x):
    return 1.0 / (1.0 + jnp.exp(-x))


def _silu(x):
    return x * _sigmoid(x)


def _softplus(x):
    return jnp.maximum(x, 0.0) + jnp.log(1.0 + jnp.exp(-jnp.abs(x)))


def _iota(shape, dim):
    return lax.broadcasted_iota(jnp.int32, shape, dim)


def _lower_tri(n, strict=False):
    r, c = _iota((n, n), 0), _iota((n, n), 1)
    return (c < r) if strict else (c <= r)


MOD_TN = 1024


def _mod_kernel(c_ref, w_ref, b_ref, o_ref):
    o_ref[...] = _dot(c_ref[...].astype(BF16), w_ref[...].astype(BF16)) + b_ref[...]


def modulation(c_all, w_mod, b_mod):
    rows = c_all.shape[0]
    n = w_mod.shape[-1]
    return pl.pallas_call(
        _mod_kernel,
        grid=(DEPTH, n // MOD_TN),
        in_specs=[pl.BlockSpec((rows, D_MODEL), lambda l, j: (0, 0)),
                  pl.BlockSpec((None, D_MODEL, MOD_TN), lambda l, j: (l, 0, j)),
                  pl.BlockSpec((None, 1, MOD_TN), lambda l, j: (l, 0, j))],
        out_specs=pl.BlockSpec((None, rows, MOD_TN), lambda l, j: (l, 0, j)),
        out_shape=jax.ShapeDtypeStruct((DEPTH, rows, n), F32),
        compiler_params=_cparams(("arbitrary", "arbitrary")),
        name="modulation",
    )(c_all, w_mod, b_mod.reshape(DEPTH, 1, n))


MIXER_BATCH_ROWS = 8


def _batch_rows(bsz, nchunks):
    if nchunks > 1 or bsz % MIXER_BATCH_ROWS:
        return 1
    return MIXER_BATCH_ROWS


def _over_batch(inner, bb, batched):
    def kern(*refs):
        def one(bi):
            inner(*[r.at[bi] if flag else r for r, flag in zip(refs, batched)])
        if bb == 1:
            one(0)
        else:
            def body(bi, carry):
                one(bi)
                return carry
            lax.fori_loop(0, bb, body, 0)
    return kern


def _row_tiling(bsz, seq, target):
    if seq >= target:
        return 1, target
    return min(bsz, target // seq), seq


PROJ_STEP_W = 2 * W_GROUP
PROJ_STEPS = N_MAIN_GROUPS * W_GROUP // PROJ_STEP_W
assert (G_K, G_V) == (N_MAIN_GROUPS - 2, N_MAIN_GROUPS - 1) and PROJ_STEP_W == 2 * W_GROUP


def _inproj_kernel(*refs, bb, lt, cl, transpose_kv):
    if transpose_kv:
        x_ref, shift_ref, scale_ref, w_ref, ws_ref, _, _, proj_ref, small_ref, kt_ref, vt_ref, u_ref = refs
    else:
        x_ref, shift_ref, scale_ref, w_ref, ws_ref, proj_ref, small_ref, u_ref = refs
    j = pl.program_id(2)
    n_l = lt // cl

    @pl.when(j == 0)
    def _():
        def body(it, carry):
            bi = it // n_l
            r0 = pl.multiple_of((it % n_l) * cl, cl)
            x = x_ref[bi, pl.ds(r0, cl), :]
            mu = jnp.mean(x, axis=-1, keepdims=True)
            xc = x - mu
            var = jnp.mean(xc * xc, axis=-1, keepdims=True)
            u = xc * lax.rsqrt(var + EPS) * (1.0 + scale_ref[bi]) + shift_ref[bi]
            u_ref[pl.ds(pl.multiple_of(it * cl, cl), cl), :] = u.astype(BF16)
            return carry
        lax.fori_loop(0, bb * n_l, body, 0)
        small_ref[...] = _dot_nt(u_ref[...], ws_ref[...]).reshape(bb, lt, SMALL_W)

    if transpose_kv:
        @pl.when(j < PROJ_STEPS - 1)
        def _():
            proj_ref[...] = _dot_nt(u_ref[...], w_ref[...]).reshape(bb, lt, PROJ_STEP_W)

        @pl.when(j == PROJ_STEPS - 1)
        def _():
            kv = _dot_nt(w_ref[...], u_ref[...])
            kt_ref[...] = kv[:W_GROUP].reshape(H_B, D_B, lt)
            vt_ref[...] = kv[W_GROUP:].reshape(H_B, D_B, lt)
    else:
        proj_ref[...] = _dot_nt(u_ref[...], w_ref[...]).reshape(bb, lt, PROJ_STEP_W)


def in_projection(x, mod, wt_main, wt_small, layer, kv_t=None):
    bsz, seq, _ = x.shape
    bb, lt = _row_tiling(bsz, seq, 1024)
    cl = min(lt, 128)
    tm = bb * lt
    transpose_kv = kv_t is not None
    kern = functools.partial(_inproj_kernel, bb=bb, lt=lt, cl=cl, transpose_kv=transpose_kv)
    in_specs = [pl.BlockSpec((bb, lt, D_MODEL), lambda b, l, j: (b, l, 0)),
                pl.BlockSpec((bb, 1, D_MODEL), lambda b, l, j: (b, 0, 0)),
                pl.BlockSpec((bb, 1, D_MODEL), lambda b, l, j: (b, 0, 1)),
                pl.BlockSpec((None, PROJ_STEP_W, D_MODEL), lambda b, l, j: (layer, j, 0)),
                pl.BlockSpec((None, SMALL_W, D_MODEL), lambda b, l, j: (layer, 0, 0))]
    small_spec = pl.BlockSpec((bb, lt, SMALL_W), lambda b, l, j: (b, l, 0))
    small_shape = jax.ShapeDtypeStruct((bsz, seq, SMALL_W), F32)
    scratch = [pltpu.VMEM((tm, D_MODEL), BF16)]
    sem = ("arbitrary", "arbitrary", "arbitrary")
    grid = (bsz // bb, seq // lt, PROJ_STEPS)
    if not transpose_kv:
        return pl.pallas_call(
            kern, grid=grid, in_specs=in_specs,
            out_specs=[pl.BlockSpec((bb, lt, PROJ_STEP_W), lambda b, l, j: (b, l, j)), small_spec],
            out_shape=[jax.ShapeDtypeStruct((bsz, seq, N_MAIN_GROUPS * W_GROUP), F32), small_shape],
            scratch_shapes=scratch, compiler_params=_cparams(sem), name="in_projection",
        )(x, mod, mod, wt_main, wt_small)
    assert bb == 1
    any_spec = pl.BlockSpec(memory_space=pl.ANY)
    t_spec = pl.BlockSpec((None, None, H_B, D_B, lt), lambda b, l, j: (layer, b, 0, 0, l))
    t_shape = jax.ShapeDtypeStruct(kv_t[0].shape, F32)
    return pl.pallas_call(
        kern, grid=grid, in_specs=in_specs + [any_spec, any_spec],
        out_specs=[pl.BlockSpec((bb, lt, PROJ_STEP_W), lambda b, l, j: (b, l, jnp.minimum(j, PROJ_STEPS - 2))),
                   small_spec, t_spec, t_spec],
        out_shape=[jax.ShapeDtypeStruct((bsz, seq, G_K * W_GROUP), F32), small_shape, t_shape, t_shape],
        input_output_aliases={5: 2, 6: 3},
        scratch_shapes=scratch, compiler_params=_cparams(sem), name="in_projection_kvt",
    )(x, mod, mod, wt_main, wt_small, kv_t[0], kv_t[1])


def _outproj_kernel(ya_ref, yb_ref, yc_ref, yd_ref, x_ref, gate_ref, w_ref, g_ref, b_ref, o_ref, acc_ref,
                    *, bb, lt, cl):
    tm = bb * lt
    n_l = lt // cl
    n_chunks = bb * n_l
    halves = 2 if n_chunks % 2 == 0 else 1
    ys = [r[...].reshape(tm, W_GROUP) for r in (ya_ref, yb_ref, yc_ref, yd_ref)]
    hr = tm // halves
    for half in range(halves):
        rows = slice(half * hr, (half + 1) * hr)
        acc = _dot(ys[0][rows], w_ref[0])
        for g in range(1, 4):
            acc += _dot(ys[g][rows], w_ref[g])
        acc_ref[rows, :] = acc

    for it in range(n_chunks):
        bi, r0 = it // n_l, (it % n_l) * cl
        o = acc_ref[it * cl:(it + 1) * cl, :]
        v = ALPHA * x_ref[bi, r0:r0 + cl, :] + (1.0 + gate_ref[bi]) * o
        mu = jnp.mean(v, axis=-1, keepdims=True)
        vc = v - mu
        var = jnp.mean(vc * vc, axis=-1, keepdims=True)
        o_ref[bi, r0:r0 + cl, :] = vc * lax.rsqrt(var + EPS) * g_ref[...] + b_ref[...]


def out_projection(ys, x, mod, w_out, ln_g, ln_b, layer):
    bsz, seq, _ = x.shape
    bb, lt = _row_tiling(bsz, seq, 512)
    cl = min(lt, 128)
    kern = functools.partial(_outproj_kernel, bb=bb, lt=lt, cl=cl)
    yspec = pl.BlockSpec((bb, lt, W_GROUP), lambda b, l: (b, l, 0))
    return pl.pallas_call(
        kern,
        grid=(bsz // bb, seq // lt),
        in_specs=[yspec, yspec, yspec, yspec,
                  pl.BlockSpec((bb, lt, D_MODEL), lambda b, l: (b, l, 0)),
                  pl.BlockSpec((bb, 1, D_MODEL), lambda b, l: (b, 0, 2)),
                  pl.BlockSpec((None, 4, W_GROUP, D_MODEL), lambda b, l: (layer, 0, 0, 0)),
                  pl.BlockSpec((1, D_MODEL), lambda b, l: (0, 0)),
                  pl.BlockSpec((1, D_MODEL), lambda b, l: (0, 0))],
        out_specs=pl.BlockSpec((bb, lt, D_MODEL), lambda b, l: (b, l, 0)),
        out_shape=jax.ShapeDtypeStruct((bsz, seq, D_MODEL), F32),
        scratch_shapes=[pltpu.VMEM((bb * lt, D_MODEL), F32)],
        compiler_params=_cparams(("arbitrary", "arbitrary")),
        name="out_projection",
    )(*ys, x, mod, w_out, ln_g.reshape(1, D_MODEL), ln_b.reshape(1, D_MODEL))


CONV_PAD = 8


def _ssd_kernel(z_ref, x_ref, bc_ref, small_ref, cbuf_ref, ssm0_ref, cw_ref, cb_ref, dtb_ref, alog_ref,
                dskip_ref, nw_ref, y_ref, cnew_ref, ssmnew_ref, ext_ref, state_ref, wcat_ref, xbd_ref,
                ccat_ref, bcat_ref, *, q, nchunks):
    c = pl.program_id(1)
    tail = K_A - 1

    @pl.when(c == 0)
    def _():
        ext_ref[CONV_PAD - tail:CONV_PAD, :] = cbuf_ref[...]
        state_ref[...] = jnp.zeros_like(state_ref)
        for h in range(H_A):
            state_ref[h * P_A:(h + 1) * P_A, h * N_A:(h + 1) * N_A] = ssm0_ref[h]

    ext_ref[CONV_PAD:CONV_PAD + q, 0:W_GROUP] = x_ref[...]
    ext_ref[CONV_PAD:CONV_PAD + q, W_GROUP:] = bc_ref[...]
    conv = cb_ref[...]
    for k in range(K_A):
        off = CONV_PAD - tail + k
        conv = conv + cw_ref[k:k + 1, :] * ext_ref[off:off + q, :]
    new_tail = ext_ref[CONV_PAD + q - tail:CONV_PAD + q, :]
    ext_ref[CONV_PAD - tail:CONV_PAD, :] = new_tail

    @pl.when(c == nchunks - 1)
    def _():
        cnew_ref[...] = new_tail

    xbc = _silu(conv)
    xs = xbc[:, :W_GROUP]
    bm = [xbc[:, W_GROUP + g * N_A:W_GROUP + (g + 1) * N_A] for g in range(G_A)]
    cm = [xbc[:, W_GROUP + (G_A + g) * N_A:W_GROUP + (G_A + g + 1) * N_A] for g in range(G_A)]

    lane = _iota((1, SMALL_W), 1)
    head_lanes = lane < H_A
    dt = jnp.where(head_lanes, _softplus(small_ref[...] + dtb_ref[...]), 0.0)
    a = -jnp.exp(alog_ref[...])
    da = dt * a
    acum = _select_left(_lower_tri(q), da)
    eye8 = _iota((8, SMALL_W), 0) == _iota((8, SMALL_W), 1)
    acum_row = _select_nt(eye8, acum)
    last = acum[q - 1:q, :]
    e_acum = jnp.exp(acum)
    w_s = jnp.exp(last - acum)
    e_last = jnp.exp(last)

    expand = _iota((SMALL_W, W_GROUP), 1) // P_A == _iota((SMALL_W, W_GROUP), 0)
    dt_wide = _select_right(dt, expand)
    xdt = xs * dt_wide
    xdt_bf = xdt.astype(BF16)
    col_head = _iota((1, W_GROUP), 1) // P_A
    causal = _lower_tri(q)

    gmat = [_dot_nt(cm[g].astype(BF16), bm[g].astype(BF16)) for g in range(G_A)]
    for h in range(H_A):
        g = h // (H_A // G_A)
        seg = acum[:, h:h + 1] - acum_row[h:h + 1, :]
        decay = jnp.exp(jnp.where(causal, seg, -jnp.inf))
        wcat_ref[:, h * q:(h + 1) * q] = (gmat[g] * decay).astype(BF16)
        xbd_ref[h * q:(h + 1) * q, :] = jnp.where(col_head == h, xdt, 0.0).astype(BF16)
        ccat_ref[:, h * N_A:(h + 1) * N_A] = (cm[g] * e_acum[:, h:h + 1]).astype(BF16)
        bcat_ref[:, h * N_A:(h + 1) * N_A] = (bm[g] * w_s[:, h:h + 1]).astype(BF16)

    y = _dot(wcat_ref[...], xbd_ref[...])
    y = y + _dot_nt(ccat_ref[...], state_ref[...].astype(BF16))
    y = y + dskip_ref[...] * xs

    upd = _dot_tn(xdt_bf, bcat_ref[...])
    for h in range(H_A):
        rs, cs = slice(h * P_A, (h + 1) * P_A), slice(h * N_A, (h + 1) * N_A)
        state_ref[rs, cs] = e_last[:, h:h + 1] * state_ref[rs, cs] + upd[rs, cs]

    @pl.when(c == nchunks - 1)
    def _():
        for h in range(H_A):
            ssmnew_ref[h] = state_ref[h * P_A:(h + 1) * P_A, h * N_A:(h + 1) * N_A]

    yz = y * _silu(z_ref[...])
    ms = jnp.mean(yz * yz, axis=-1, keepdims=True)
    y_ref[...] = (yz * lax.rsqrt(ms + EPS) * nw_ref[...]).astype(BF16)


def mixer_ssd(proj, small, cbuf, ssm0_all, layer, conv_w, conv_b, dt_bias, a_log, d_skip, norm_w):
    bsz, seq, _ = proj.shape
    q = min(seq, 256)
    nchunks = seq // q
    pad = lambda v: jnp.zeros((1, SMALL_W), F32).at[0, :v.shape[0]].set(v)
    bb = _batch_rows(bsz, nchunks)
    kern = _over_batch(functools.partial(_ssd_kernel, q=q, nchunks=nchunks), bb,
                       (True,) * 6 + (False,) * 6 + (True,) * 3 + (False,) * 6)
    col = lambda j: pl.BlockSpec((bb, q, W_GROUP), lambda b, c: (b, c, j))
    full = lambda shape: pl.BlockSpec(shape, lambda b, c: (0,) * len(shape))
    return pl.pallas_call(
        kern,
        grid=(bsz // bb, nchunks),
        in_specs=[col(G_ZA), col(G_XA), col(G_BCA),
                  pl.BlockSpec((bb, q, SMALL_W), lambda b, c: (b, c, 0)),
                  pl.BlockSpec((bb, K_A - 1, CONV_DIM_A), lambda b, c: (b, 0, 0)),
                  pl.BlockSpec((None, bb, H_A, P_A, N_A), lambda b, c: (layer, b, 0, 0, 0)),
                  full((K_A, CONV_DIM_A)), full((1, CONV_DIM_A)), full((1, SMALL_W)), full((1, SMALL_W)),
                  full((1, W_GROUP)), full((1, W_GROUP))],
        out_specs=[pl.BlockSpec((bb, q, W_GROUP), lambda b, c: (b, c, 0)),
                   pl.BlockSpec((bb, K_A - 1, CONV_DIM_A), lambda b, c: (b, 0, 0)),
                   pl.BlockSpec((bb, H_A, P_A, N_A), lambda b, c: (b, 0, 0, 0))],
        out_shape=[jax.ShapeDtypeStruct((bsz, seq, W_GROUP), BF16),
                   jax.ShapeDtypeStruct((bsz, K_A - 1, CONV_DIM_A), F32),
                   jax.ShapeDtypeStruct((bsz, H_A, P_A, N_A), F32)],
        scratch_shapes=[pltpu.VMEM((CONV_PAD + q, CONV_DIM_A), F32),
                        pltpu.VMEM((H_A * P_A, H_A * N_A), F32),
                        pltpu.VMEM((q, H_A * q), BF16),
                        pltpu.VMEM((H_A * q, W_GROUP), BF16),
                        pltpu.VMEM((q, H_A * N_A), BF16),
                        pltpu.VMEM((q, H_A * N_A), BF16)],
        compiler_params=_cparams(("arbitrary", "arbitrary")),
        name="mixer_ssd",
    )(proj, proj, proj, small, cbuf, ssm0_all, conv_w, conv_b.reshape(1, CONV_DIM_A), pad(dt_bias), pad(a_log),
      jnp.repeat(d_skip, P_A).reshape(1, W_GROUP), norm_w.reshape(1, W_GROUP))


def _mlstm_kernel(xc_ref, zc_ref, small_ref, cbuf_ref, c0_ref, n0_ref, m0_ref, cw_ref, cb_ref, wq_ref, wk_ref,
                  wv_ref, igb_ref, fgb_ref, nw_ref, skip_ref, y_ref, cnew_ref, cst_ref, nst_ref, mst_ref,
                  ext_ref, cs_ref, ns_ref, ms_ref, *, q, nchunks):
    c = pl.program_id(1)
    tail = K_C - 1

    @pl.when(c == 0)
    def _():
        ext_ref[CONV_PAD - tail:CONV_PAD, :] = cbuf_ref[...]
        cs_ref[...] = c0_ref[...]
        ns_ref[...] = n0_ref[...]
        ms_ref[...] = m0_ref[...]

    x_in = xc_ref[...]
    ext_ref[CONV_PAD:CONV_PAD + q, :] = x_in
    conv = cb_ref[...]
    for k in range(K_C):
        off = CONV_PAD - tail + k
        conv = conv + cw_ref[k:k + 1, :] * ext_ref[off:off + q, :]
    new_tail = ext_ref[CONV_PAD + q - tail:CONV_PAD + q, :]
    ext_ref[CONV_PAD - tail:CONV_PAD, :] = new_tail

    @pl.when(c == nchunks - 1)
    def _():
        cnew_ref[...] = new_tail

    xconv = _silu(conv)
    sm = small_ref[...]
    ipre = sm + igb_ref[...]
    fpre = sm + fgb_ref[...]
    logf = jnp.minimum(fpre, 0.0) - jnp.log(1.0 + jnp.exp(-jnp.abs(fpre)))
    bcum = _select_left(_lower_tri(q), logf)
    sel_i = _iota((8, SMALL_W), 1) == _iota((8, SMALL_W), 0) + LANE_I
    sel_f = _iota((8, SMALL_W), 1) == _iota((8, SMALL_W), 0) + LANE_F
    r_row = _select_nt(sel_i, ipre) - _select_nt(sel_f, bcum)
    causal = _lower_tri(q)
    scale_k = DH_C ** -0.5
    heads = range(H_C)
    hs = [slice(h * DH_C, (h + 1) * DH_C) for h in heads]

    xh = [xconv[:, hs[h]].astype(BF16) for h in heads]
    qh = [_dot(xh[h], wq_ref[h]) for h in heads]
    kh = [_dot(xh[h], wk_ref[h]) * scale_k for h in heads]
    vh = [_dot(x_in[:, hs[h]].astype(BF16), wv_ref[h]) for h in heads]
    qb = [t.astype(BF16) for t in qh]
    kb = [t.astype(BF16) for t in kh]
    vb = [t.astype(BF16) for t in vh]
    qk = [_dot_nt(qb[h], kb[h]) for h in heads]
    c_old = [cs_ref[h] for h in heads]
    n_old = [ns_ref[h:h + 1, :] for h in heads]
    qc = [_dot_nt(qb[h], c_old[h].astype(BF16)) for h in heads]

    b_col = [bcum[:, LANE_F + h:LANE_F + h + 1] for h in heads]
    m_prev = [ms_ref[:, h:h + 1] for h in heads]
    d = [jnp.where(causal, b_col[h] + r_row[h:h + 1, :], -jnp.inf) for h in heads]
    inter = [b_col[h] + m_prev[h] for h in heads]
    m_t = [jnp.maximum(inter[h], jnp.max(d[h], axis=-1, keepdims=True)) for h in heads]
    w = [jnp.exp(d[h] - m_t[h]) * qk[h] for h in heads]
    gq = [jnp.exp(inter[h] - m_t[h]) for h in heads]
    m_new = [m_t[h][q - 1:q, :] for h in heads]
    b_last = [b_col[h][q - 1:q, :] for h in heads]
    ws = [jnp.exp(b_last[h] - b_col[h] + ipre[:, LANE_I + h:LANE_I + h + 1] - m_new[h]) for h in heads]
    g_last = [jnp.exp(b_last[h] + m_prev[h] - m_new[h]) for h in heads]
    wv = [_dot(w[h].astype(BF16), vb[h]) for h in heads]
    c_upd = [_dot_tn((vh[h] * ws[h]).astype(BF16), kb[h]) for h in heads]

    for h in heads:
        num = wv[h] + gq[h] * qc[h]
        nq = jnp.sum(w[h], axis=-1, keepdims=True) + gq[h] * jnp.sum(qh[h] * n_old[h], axis=-1, keepdims=True)
        hid = num / jnp.maximum(jnp.abs(nq), jnp.exp(-m_t[h]))
        cs_ref[h] = g_last[h] * c_old[h] + c_upd[h]
        ns_ref[h:h + 1, :] = g_last[h] * n_old[h] + jnp.sum(ws[h] * kh[h], axis=0, keepdims=True)
        ms_ref[:, h:h + 1] = m_new[h]

        mu = jnp.mean(hid, axis=-1, keepdims=True)
        hc = hid - mu
        var = jnp.mean(hc * hc, axis=-1, keepdims=True)
        hn = hc * lax.rsqrt(var + EPS) * nw_ref[:, hs[h]]
        yh = (hn + skip_ref[:, hs[h]] * xconv[:, hs[h]]) * _silu(zc_ref[:, hs[h]])
        y_ref[:, hs[h]] = yh.astype(BF16)

    @pl.when(c == nchunks - 1)
    def _():
        cst_ref[...] = cs_ref[...]
        nst_ref[...] = ns_ref[...]
        mst_ref[...] = ms_ref[...]


def mixer_mlstm(proj, small, cbuf, c0_all, layer, n0, m0, conv_w, conv_b, wq, wk, wv, ig_bias, fg_bias, norm_w,
                skip):
    bsz, seq, _ = proj.shape
    q = min(seq, 256)
    nchunks = seq // q
    bb = _batch_rows(bsz, nchunks)
    kern = _over_batch(functools.partial(_mlstm_kernel, q=q, nchunks=nchunks), bb,
                       (True,) * 7 + (False,) * 9 + (True,) * 5 + (False,) * 4)
    pad_at = lambda v, lane: jnp.zeros((1, SMALL_W), F32).at[0, lane:lane + v.shape[0]].set(v)
    col = lambda j: pl.BlockSpec((bb, q, W_GROUP), lambda b, c: (b, c, j))
    full = lambda shape: pl.BlockSpec(shape, lambda b, c: (0,) * len(shape))
    y, cnew, cst, nst, mst = pl.pallas_call(
        kern,
        grid=(bsz // bb, nchunks),
        in_specs=[col(G_XC), col(G_ZC),
                  pl.BlockSpec((bb, q, SMALL_W), lambda b, c: (b, c, 0)),
                  pl.BlockSpec((bb, K_C - 1, W_GROUP), lambda b, c: (b, 0, 0)),
                  pl.BlockSpec((None, bb, H_C, DH_C, DH_C), lambda b, c: (layer, b, 0, 0, 0)),
                  pl.BlockSpec((bb, H_C, DH_C), lambda b, c: (b, 0, 0)),
                  pl.BlockSpec((bb, 1, H_C), lambda b, c: (b, 0, 0)),
                  full((K_C, W_GROUP)), full((1, W_GROUP)),
                  full((H_C, DH_C, DH_C)), full((H_C, DH_C, DH_C)), full((H_C, DH_C, DH_C)),
                  full((1, SMALL_W)), full((1, SMALL_W)), full((1, W_GROUP)), full((1, W_GROUP))],
        out_specs=[pl.BlockSpec((bb, q, W_GROUP), lambda b, c: (b, c, 0)),
                   pl.BlockSpec((bb, K_C - 1, W_GROUP), lambda b, c: (b, 0, 0)),
                   pl.BlockSpec((bb, H_C, DH_C, DH_C), lambda b, c: (b, 0, 0, 0)),
                   pl.BlockSpec((bb, H_C, DH_C), lambda b, c: (b, 0, 0)),
                   pl.BlockSpec((bb, 1, H_C), lambda b, c: (b, 0, 0))],
        out_shape=[jax.ShapeDtypeStruct((bsz, seq, W_GROUP), BF16),
                   jax.ShapeDtypeStruct((bsz, K_C - 1, W_GROUP), F32),
                   jax.ShapeDtypeStruct((bsz, H_C, DH_C, DH_C), F32),
                   jax.ShapeDtypeStruct((bsz, H_C, DH_C), F32),
                   jax.ShapeDtypeStruct((bsz, 1, H_C), F32)],
        scratch_shapes=[pltpu.VMEM((CONV_PAD + q, W_GROUP), F32),
                        pltpu.VMEM((H_C, DH_C, DH_C), F32),
                        pltpu.VMEM((H_C, DH_C), F32),
                        pltpu.VMEM((1, H_C), F32)],
        compiler_params=_cparams(("arbitrary", "arbitrary")),
        name="mixer_mlstm",
    )(proj, proj, small, cbuf, c0_all, n0, m0.reshape(bsz, 1, H_C), conv_w, conv_b.reshape(1, W_GROUP),
      wq.astype(BF16), wk.astype(BF16), wv.astype(BF16), pad_at(ig_bias, LANE_I), pad_at(fg_bias, LANE_F),
      norm_w.reshape(1, W_GROUP), skip.reshape(1, W_GROUP))
    return y, cnew, cst, nst, mst.reshape(bsz, H_C)


CONV_D_PAD = 32


def _conf_kernel(a_ref, b_ref, g_ref, cbuf_ref, cw_ref, cb_ref, lg_ref, lb_ref, y_ref, cnew_ref, ext_ref,
                 win_ref, *, q, nchunks):
    c = pl.program_id(1)
    tail = K_D - 1

    @pl.when(c == 0)
    def _():
        ext_ref[CONV_D_PAD - tail:CONV_D_PAD, :] = cbuf_ref[...]

    ext_ref[CONV_D_PAD:CONV_D_PAD + q, :] = a_ref[...] * _sigmoid(b_ref[...])
    conv = cb_ref[...]
    offsets = [CONV_D_PAD - tail + k for k in range(K_D)]
    for r in range(8):
        taps = [k for k in range(K_D) if offsets[k] % 8 == r]
        if not taps:
            continue
        span = max(offsets[k] for k in taps) - r
        if r == 0:
            window_ref = ext_ref
        else:
            window_ref = win_ref.at[r - 1]
            window_ref[0:span + q, :] = ext_ref[r:r + span + q, :]
        for k in taps:
            a = offsets[k] - r
            conv = conv + cw_ref[k:k + 1, :] * window_ref[a:a + q, :]
    new_tail = ext_ref[CONV_D_PAD + q - tail:CONV_D_PAD + q, :]
    ext_ref[CONV_D_PAD - tail:CONV_D_PAD, :] = new_tail

    @pl.when(c == nchunks - 1)
    def _():
        cnew_ref[...] = new_tail

    mu = jnp.mean(conv, axis=-1, keepdims=True)
    cc = conv - mu
    var = jnp.mean(cc * cc, axis=-1, keepdims=True)
    v = cc * lax.rsqrt(var + EPS) * lg_ref[...] + lb_ref[...]
    y_ref[...] = (_silu(v) * _silu(g_ref[...])).astype(BF16)


def mixer_conformer(proj, cbuf, conv_w, conv_b, ln_g, ln_b):
    bsz, seq, _ = proj.shape
    q = min(seq, 256)
    nchunks = seq // q
    bb = _batch_rows(bsz, nchunks)
    kern = _over_batch(functools.partial(_conf_kernel, q=q, nchunks=nchunks), bb,
                       (True,) * 4 + (False,) * 4 + (True,) * 2 + (False,) * 2)
    col = lambda j: pl.BlockSpec((bb, q, W_GROUP), lambda b, c: (b, c, j))
    full = lambda shape: pl.BlockSpec(shape, lambda b, c: (0,) * len(shape))
    return pl.pallas_call(
        kern,
        grid=(bsz // bb, nchunks),
        in_specs=[col(G_AD), col(G_BD), col(G_GD),
                  pl.BlockSpec((bb, K_D - 1, W_GROUP), lambda b, c: (b, 0, 0)),
                  full((K_D, W_GROUP)), full((1, W_GROUP)), full((1, W_GROUP)), full((1, W_GROUP))],
        out_specs=[pl.BlockSpec((bb, q, W_GROUP), lambda b, c: (b, c, 0)),
                   pl.BlockSpec((bb, K_D - 1, W_GROUP), lambda b, c: (b, 0, 0))],
        out_shape=[jax.ShapeDtypeStruct((bsz, seq, W_GROUP), BF16),
                   jax.ShapeDtypeStruct((bsz, K_D - 1, W_GROUP), F32)],
        scratch_shapes=[pltpu.VMEM((CONV_D_PAD + q, W_GROUP), F32),
                        pltpu.VMEM((7, CONV_D_PAD + q, W_GROUP), F32)],
        compiler_params=_cparams(("arbitrary", "arbitrary")),
        name="mixer_conformer",
    )(proj, proj, proj, cbuf, conv_w, conv_b.reshape(1, W_GROUP), ln_g.reshape(1, W_GROUP),
      ln_b.reshape(1, W_GROUP))


ATT_BLOCK = 256


LOG2E = math.log2(math.e)
Q_SCALE = D_B ** -0.5 * LOG2E


def _stick_blocks(z2s, mask, carries, suffix_mat):
    masks = mask if isinstance(mask, (list, tuple)) else [mask] * len(z2s)
    sps = [jnp.maximum(z2, 0.0) + jnp.log2(1.0 + jnp.exp2(-jnp.abs(z2))) for z2 in z2s]
    if mask is not None:
        sps = [jnp.where(m, sp, 0.0) for m, sp in zip(masks, sps)]
    suffixes = [_dot(sp.astype(BF16), suffix_mat) for sp in sps]
    ws = [jnp.exp2(z2 - sp - suffix - carry) for z2, sp, suffix, carry in zip(z2s, sps, suffixes, carries)]
    if mask is not None:
        ws = [jnp.where(m, w, 0.0) for m, w in zip(masks, ws)]
    totals = [suffix[:, 0:1] + sp[:, 0:1] for suffix, sp in zip(suffixes, sps)]
    return ws, [carry + total for carry, total in zip(carries, totals)]


def _suffix_matrix(n):
    return (jnp.arange(n)[:, None] > jnp.arange(n)[None, :]).astype(BF16)


def _attn_prompt_kernel(q_ref, kt_ref, vt_ref, g_ref, sm_ref, y_ref, kb_ref, vb_ref, z_ref, w_ref, acc_ref,
                        carry_ref, *, t, nblk):
    qi = pl.program_id(2)
    heads = range(kb_ref.shape[0])

    @pl.when(qi == 0)
    def _():
        for hh in heads:
            for blk in range(nblk):
                kb_ref[hh, blk] = kt_ref[hh, :, blk * t:(blk + 1) * t].astype(BF16)
                vb_ref[hh, blk] = vt_ref[hh, :, blk * t:(blk + 1) * t].astype(BF16)

    suffix_mat = sm_ref[...]
    qs = [(q_ref[0, :, hh * D_B:(hh + 1) * D_B] * Q_SCALE).astype(BF16) for hh in heads]

    def form_scores(blk, slot):
        for hh in heads:
            z_ref[slot, hh] = _dot(qs[hh], kb_ref[hh, blk])

    def add_values(blk):
        for hh in heads:
            acc_ref[hh] += _dot_nt(w_ref[hh], vb_ref[hh, blk])

    halves = (slice(0, t // 2), slice(t // 2, t))

    def form_weights(z2s, mask):
        tiles = [(hh, rs) for hh in heads for rs in halves]
        masks = None if mask is None else [mask[rs] for _, rs in tiles]
        ws, carries = _stick_blocks([z2s[hh][rs] for hh, rs in tiles], masks,
                                    [carry_ref[hh, rs] for hh, rs in tiles], suffix_mat)
        for (hh, rs), w, carry in zip(tiles, ws, carries):
            w_ref[hh, rs] = w.astype(BF16)
            carry_ref[hh, rs] = carry

    acc_ref[...] = jnp.zeros_like(acc_ref)
    carry_ref[...] = jnp.zeros_like(carry_ref)
    form_scores(qi, 0)
    form_scores(jnp.maximum(qi - 1, 0), 1)
    form_weights([z_ref[0, hh] for hh in heads], _lower_tri(t, strict=True))

    def body(i, carry):
        blk = qi - 1 - i
        z2s = [z_ref[(i + 1) % 2, hh] for hh in heads]
        add_values(blk + 1)
        form_scores(jnp.maximum(blk - 1, 0), i % 2)
        form_weights(z2s, None)
        return carry
    lax.fori_loop(0, qi, body, 0)
    add_values(0)
    out = jnp.concatenate([acc_ref[hh] for hh in heads], axis=-1)
    y_ref[0] = (out * _silu(g_ref[0])).astype(BF16)


def attention_prompt(proj, kt_all, vt_all, layer):
    bsz, seq, _ = proj.shape
    t = min(seq, ATT_BLOCK)
    nq = seq // t
    lanes = 128
    per = W_GROUP // lanes
    hp = lanes // D_B
    kern = functools.partial(_attn_prompt_kernel, t=t, nblk=nq)
    t_spec = pl.BlockSpec((None, None, hp, D_B, seq), lambda b, p, i: (layer, b, p, 0, 0))
    return pl.pallas_call(
        kern,
        grid=(bsz, per, nq),
        in_specs=[pl.BlockSpec((1, t, lanes), lambda b, p, i: (b, i, G_Q * per + p)),
                  t_spec, t_spec,
                  pl.BlockSpec((1, t, lanes), lambda b, p, i: (b, i, G_GB * per + p)),
                  pl.BlockSpec((t, t), lambda b, p, i: (0, 0))],
        out_specs=pl.BlockSpec((1, t, lanes), lambda b, p, i: (b, i, p)),
        out_shape=jax.ShapeDtypeStruct((bsz, seq, W_GROUP), BF16),
        scratch_shapes=[pltpu.VMEM((hp, nq, D_B, t), BF16),
                        pltpu.VMEM((hp, nq, D_B, t), BF16),
                        pltpu.VMEM((2, hp, t, t), F32),
                        pltpu.VMEM((hp, t, t), BF16),
                        pltpu.VMEM((hp, t, D_B), F32),
                        pltpu.VMEM((hp, t, 1), F32)],
        compiler_params=_cparams(("arbitrary", "arbitrary", "arbitrary")),
        name="attention_prompt",
    )(proj, kt_all, vt_all, proj, _suffix_matrix(t))


CACHE_BLOCK = 2048


def _attn_sample_kernel(q_ref, kn_ref, vn_ref, g_ref, kc_ref, vc_ref, sm_ref, smn_ref, _, __,
                        y_ref, knew_ref, vnew_ref, qb_ref, acc_ref, carry_ref, *, lq, nkb):
    j = pl.program_id(1)
    rows = H_B * lq
    heads = range(H_B)
    hl = lambda h: slice(h * D_B, (h + 1) * D_B)

    @pl.when(j == 0)
    def _():
        kn, vn = kn_ref[0], vn_ref[0]
        q2 = (q_ref[0] * Q_SCALE).astype(BF16)
        knb, vnb = kn.astype(BF16), vn.astype(BF16)
        for h in heads:
            qb_ref[h] = q2[:, hl(h)]
            knew_ref[:, h, :] = kn[:, hl(h)]
            vnew_ref[:, h, :] = vn[:, hl(h)]
        z2 = jnp.concatenate([_dot_nt(q2[:, hl(h)], knb[:, hl(h)]) for h in heads], axis=0)
        mask = _iota((rows, lq), 1) < _iota((rows, lq), 0) % lq
        (w,), (carry,) = _stick_blocks([z2], mask, [jnp.zeros((rows, 1), F32)], smn_ref[...])
        wb = w.astype(BF16)
        for h in heads:
            acc_ref[h] = _dot(wb[h * lq:(h + 1) * lq], vnb[:, hl(h)])
        carry_ref[...] = carry

    suffix_mat = sm_ref[...]
    subs = [slice(s * ATT_BLOCK, (s + 1) * ATT_BLOCK) for s in reversed(range(CACHE_BLOCK // ATT_BLOCK))]
    z2s = [jnp.concatenate([_dot(qb_ref[h], kc_ref[h, :, ks].astype(BF16)) for h in heads], axis=0)
           for ks in subs]
    zero = jnp.zeros((rows, 1), F32)
    ws, totals = _stick_blocks(z2s, None, [zero] * len(subs), suffix_mat)
    carry = carry_ref[...]
    accs = [acc_ref[h] for h in heads]
    for ks, w, total in zip(subs, ws, totals):
        wb = (w * jnp.exp2(-carry)).astype(BF16)
        accs = [accs[h] + _dot_nt(wb[h * lq:(h + 1) * lq], vc_ref[h, :, ks].astype(BF16)) for h in heads]
        carry = carry + total
    for h in heads:
        acc_ref[h] = accs[h]
    carry_ref[...] = carry

    @pl.when(j == nkb - 1)
    def _():
        out = jnp.concatenate(accs, axis=-1)
        y_ref[0] = (out * _silu(g_ref[0])).astype(BF16)


def attention_sample(proj, ktc, vtc, knew_all, vnew_all, layer):
    bsz, lq, _ = proj.shape
    past = ktc.shape[-1]
    nkb = past // CACHE_BLOCK
    kern = functools.partial(_attn_sample_kernel, lq=lq, nkb=nkb)
    col = lambda c: pl.BlockSpec((1, lq, W_GROUP), lambda b, j: (b, 0, c))
    cache = pl.BlockSpec((None, None, H_B, D_B, CACHE_BLOCK), lambda b, j: (layer, b, 0, 0, nkb - 1 - j))
    any_spec = pl.BlockSpec(memory_space=pl.ANY)
    new_spec = pl.BlockSpec((None, None, lq, H_B, D_B), lambda b, j: (layer, b, 0, 0, 0))
    new_shape = jax.ShapeDtypeStruct(knew_all.shape, F32)
    return pl.pallas_call(
        kern,
        grid=(bsz, nkb),
        in_specs=[col(G_Q), col(G_K), col(G_V), col(G_GB), cache, cache,
                  pl.BlockSpec((ATT_BLOCK, ATT_BLOCK), lambda b, j: (0, 0)),
                  pl.BlockSpec((lq, lq), lambda b, j: (0, 0)),
                  any_spec, any_spec],
        out_specs=[pl.BlockSpec((1, lq, W_GROUP), lambda b, j: (b, 0, 0)), new_spec, new_spec],
        out_shape=[jax.ShapeDtypeStruct((bsz, lq, W_GROUP), BF16), new_shape, new_shape],
        input_output_aliases={8: 1, 9: 2},
        scratch_shapes=[pltpu.VMEM((H_B, lq, D_B), BF16),
                        pltpu.VMEM((H_B, lq, D_B), F32),
                        pltpu.VMEM((H_B * lq, 1), F32)],
        compiler_params=_cparams(("arbitrary", "arbitrary")),
        name="attention_sample",
    )(proj, proj, proj, proj, ktc, vtc, _suffix_matrix(ATT_BLOCK), _suffix_matrix(lq), knew_all, vnew_all)


SUBLANES = 8


def _repack_kernel(starts_ref, w_ref, dt_ref, if_ref, o_ref, small_ref):
    o_ref[...] = w_ref[0].astype(BF16)

    @pl.when(pl.program_id(1) == 0)
    def _():
        rows = jnp.concatenate([dt_ref[0], if_ref[0]], axis=0)
        pad = jnp.zeros((SMALL_W - rows.shape[0], D_MODEL), F32)
        small_ref[...] = jnp.concatenate([rows, pad], axis=0).astype(BF16)


def _repack_w_in(w_in):
    layers = w_in.shape[0]
    wt = jnp.swapaxes(w_in, -1, -2)
    o_dt = W_GROUP + CONV_DIM_A
    o_q = o_dt + H_A
    o_k, o_v, o_gb, o_xc = o_q + W_GROUP, o_q + 2 * W_GROUP, o_q + 3 * W_GROUP, o_q + 4 * W_GROUP
    o_i = o_xc + 2 * W_GROUP
    o_ad = o_i + 2 * H_C
    starts = [0, W_GROUP, 2 * W_GROUP, o_q, o_gb, o_xc, o_xc + W_GROUP, o_ad, o_ad + W_GROUP, o_ad + 2 * W_GROUP,
              o_k, o_v]
    assert all(s % SUBLANES == 0 for s in starts + [o_dt, o_i]) and o_q - o_dt == o_ad - o_i == SUBLANES
    tiles = jnp.asarray([s // SUBLANES for s in starts], jnp.int32)
    rows8 = lambda start: pl.BlockSpec((pl.Element(1), pl.Element(SUBLANES), pl.Element(D_MODEL)),
                                       lambda l, g, tiles: (l, start, 0))
    return pl.pallas_call(
        _repack_kernel,
        grid_spec=pltpu.PrefetchScalarGridSpec(
            num_scalar_prefetch=1, grid=(layers, N_MAIN_GROUPS),
            in_specs=[pl.BlockSpec((pl.Element(1), pl.Element(W_GROUP), pl.Element(D_MODEL)),
                                   lambda l, g, tiles: (l, tiles[g] * SUBLANES, 0)),
                      rows8(o_dt), rows8(o_i)],
            out_specs=[pl.BlockSpec((None, W_GROUP, D_MODEL), lambda l, g, tiles: (l, g, 0)),
                       pl.BlockSpec((None, SMALL_W, D_MODEL), lambda l, g, tiles: (l, 0, 0))]),
        out_shape=[jax.ShapeDtypeStruct((layers, N_MAIN_GROUPS * W_GROUP, D_MODEL), BF16),
                   jax.ShapeDtypeStruct((layers, SMALL_W, D_MODEL), BF16)],
        compiler_params=_cparams(("arbitrary", "arbitrary")),
        name="repack_w_in",
    )(tiles, wt, wt, wt)


STACKED_WEIGHTS = ("w_main", "w_small", "w_out")


def _mixer_layer(x, mod, layer, kv_t, kv_cache_t, conv_a_buf, ssm0_all, conv_c_buf, mc0_all, mn0, mm0, conv_d_buf,
                 lw):
    if kv_cache_t is None:
        proj, small, kt_all, vt_all = in_projection(x, mod, lw["w_main"], lw["w_small"], layer, kv_t)
        y_b = attention_prompt(proj, kt_all, vt_all, layer)
    else:
        proj, small = in_projection(x, mod, lw["w_main"], lw["w_small"], layer)
        y_b, kt_all, vt_all = attention_sample(proj, kv_cache_t[0], kv_cache_t[1], kv_t[0], kv_t[1], layer)
    y_a, conv_a_new, ssm_new = mixer_ssd(proj, small, conv_a_buf, ssm0_all, layer, lw["conv_a_w"], lw["conv_a_b"],
                                         lw["dt_bias"], lw["a_log"], lw["d_skip"], lw["norm_a_w"])
    y_c, conv_c_new, mc_new, mn_new, mm_new = mixer_mlstm(
        proj, small, conv_c_buf, mc0_all, layer, mn0, mm0, lw["conv_c_w"], lw["conv_c_b"], lw["wq_c"], lw["wk_c"],
        lw["wv_c"], lw["ig_bias"], lw["fg_bias"], lw["norm_c_w"], lw["skip_c"])
    y_d, conv_d_new = mixer_conformer(proj, conv_d_buf, lw["conv_d_w"], lw["conv_d_b"], lw["ln_d_g"],
                                      lw["ln_d_b"])
    x_new = out_projection((y_a, y_b, y_c, y_d), x, mod, lw["w_out"], lw["ln_g"], lw["ln_b"], layer)
    return x_new, (kt_all, vt_all), (conv_a_new, ssm_new, conv_c_new, mc_new, mn_new, mm_new, conv_d_new)


def _run_trunk(x, mods, cache_k, cache_v, st_conv_a, st_ssm, st_conv_c, st_mc, st_mn, st_mm, st_conv_d, weights):
    bsz, seq, _ = x.shape
    outs = [[] for _ in range(7)]
    if cache_k is None:
        kv_cache_t = None
        kv_t = (jnp.zeros((DEPTH, bsz, H_B, D_B, seq), F32),) * 2
    else:
        kv_cache_t = (jnp.transpose(cache_k, (0, 1, 3, 4, 2)), jnp.transpose(cache_v, (0, 1, 3, 4, 2)))
        kv_t = (jnp.zeros((DEPTH, bsz, seq, H_B, D_B), F32),) * 2
    for l in range(DEPTH):
        lw = {name: (w if name in STACKED_WEIGHTS else w[l]) for name, w in weights.items()}
        x, kv_t, new = _mixer_layer(x, mods[l], l, kv_t, kv_cache_t, st_conv_a[l], st_ssm, st_conv_c[l],
                                    st_mc, st_mn[l], st_mm[l], st_conv_d[l], lw)
        for o, t in zip(outs, new):
            o.append(t)
    if cache_k is None:
        kv_new = [jnp.transpose(t, (0, 1, 4, 2, 3)) for t in kv_t]
    else:
        kv_new = list(kv_t)
    return x, kv_new + [jnp.stack(o) for o in outs]


def kernel(x_prompt, x_sample, cache_k, cache_v, state_conv_a, state_ssm, state_conv_c, state_mlstm_c,
           state_mlstm_n, state_mlstm_m, state_conv_d, c_prompt, c_sample, w_mod, b_mod, w_in, conv_a_w,
           conv_a_b, dt_bias, a_log, d_skip, norm_a_w, conv_c_w, conv_c_b, wq_c, wk_c, wv_c, ig_bias, fg_bias,
           norm_c_w, skip_c, conv_d_w, conv_d_b, ln_d_g, ln_d_b, w_out, ln_g, ln_b):
    batch, dec_batch = x_prompt.shape[0], x_sample.shape[0]
    w_main, w_small = _repack_w_in(w_in)
    weights = dict(w_main=w_main, w_small=w_small, conv_a_w=conv_a_w, conv_a_b=conv_a_b, dt_bias=dt_bias,
                   a_log=a_log, d_skip=d_skip, norm_a_w=norm_a_w, conv_c_w=conv_c_w, conv_c_b=conv_c_b,
                   wq_c=wq_c, wk_c=wk_c, wv_c=wv_c, ig_bias=ig_bias, fg_bias=fg_bias, norm_c_w=norm_c_w,
                   skip_c=skip_c, conv_d_w=conv_d_w, conv_d_b=conv_d_b, ln_d_g=ln_d_g, ln_d_b=ln_d_b,
                   w_out=w_out.reshape(DEPTH, 4, W_GROUP, D_MODEL).astype(BF16), ln_g=ln_g, ln_b=ln_b)

    rows = batch + dec_batch
    rows_pad = -(-rows // 8) * 8
    c_all = jnp.concatenate([c_prompt, c_sample, jnp.zeros((rows_pad - rows, D_MODEL), F32)], axis=0)
    mod_all = modulation(c_all, w_mod, b_mod)
    mods_p = mod_all[:, :batch].reshape(DEPTH, batch, 1, 3 * D_MODEL)
    mods_s = mod_all[:, batch:rows].reshape(DEPTH, dec_batch, 1, 3 * D_MODEL)

    def zeros(*shape):
        return jnp.zeros((DEPTH, batch) + shape, F32)

    y_prompt, sp = _run_trunk(x_prompt, mods_p, None, None,
                              zeros(K_A - 1, CONV_DIM_A), zeros(H_A, P_A, N_A), zeros(K_C - 1, W_GROUP),
                              zeros(H_C, DH_C, DH_C), zeros(H_C, DH_C), zeros(H_C), zeros(K_D - 1, W_GROUP),
                              weights)
    y_sample, ss = _run_trunk(x_sample, mods_s, cache_k, cache_v, state_conv_a, state_ssm, state_conv_c,
                              state_mlstm_c, state_mlstm_n, state_mlstm_m, state_conv_d, weights)
    return (y_prompt, y_sample, *sp, *ss)
```

```python
import functools
import math

import jax
import jax.numpy as jnp
from jax import lax
from jax.experimental import pallas as pl
from jax.experimental.pallas import tpu as pltpu

D_MODEL = 2048
DEPTH = 4
W_GROUP = 512
H_A, P_A, N_A, G_A, K_A = 8, 64, 128, 2, 4
CONV_DIM_A = W_GROUP + 2 * G_A * N_A
H_B, D_B = 8, 64
H_C, DH_C, K_C = 4, 128, 4
K_D = 31
ALPHA = (2 * DEPTH) ** 0.25
EPS = 1e-5
N_MAIN_GROUPS = 12
G_ZA, G_XA, G_BCA, G_Q, G_GB, G_XC, G_ZC, G_AD, G_BD, G_GD, G_K, G_V = range(N_MAIN_GROUPS)
SMALL_W = 128
LANE_DT, LANE_I, LANE_F = 0, 8, 12

F32 = jnp.float32
BF16 = jnp.bfloat16
HIGHEST = lax.Precision.HIGHEST
VMEM_LIMIT = 56 * 1024 * 1024


def _cparams(sem):
    return pltpu.CompilerParams(dimension_semantics=sem, vmem_limit_bytes=VMEM_LIMIT)


def _dot(a, b):
    return jnp.dot(a, b, preferred_element_type=F32)


def _dot_nt(a, b):
    return lax.dot_general(a, b, (((1,), (1,)), ((), ())), preferred_element_type=F32)


def _dot_tn(a, b):
    return lax.dot_general(a, b, (((0,), (0,)), ((), ())), preferred_element_type=F32)


def _split3(a):
    hi = a.astype(BF16)
    r = a - hi.astype(F32)
    mid = r.astype(BF16)
    lo = (r - mid.astype(F32)).astype(BF16)
    return jnp.concatenate([hi, mid, lo], axis=1)


def _sum3(p):
    n = p.shape[1] // 3
    return (p[:, 2 * n:] + p[:, n:2 * n]) + p[:, :n]


def _select_left(sel01, a):
    return _sum3(_dot(sel01.astype(BF16), _split3(a)))


def _select_right(a, sel01):
    s = sel01.astype(BF16)
    return _dot(_split3(a), jnp.concatenate([s, s, s], axis=0))


def _select_nt(sel01, a):
    s = sel01.astype(BF16)
    return _dot_nt(jnp.concatenate([s, s, s], axis=1), _split3(a))


def _sigmoid(x):
    return 1.0 / (1.0 + jnp.exp(-x))


def _silu(x):
    return x * _sigmoid(x)


def _softplus(x):
    return jnp.maximum(x, 0.0) + jnp.log(1.0 + jnp.exp(-jnp.abs(x)))


def _iota(shape, dim):
    return lax.broadcasted_iota(jnp.int32, shape, dim)


def _lower_tri(n, strict=False):
    r, c = _iota((n, n), 0), _iota((n, n), 1)
    return (c < r) if strict else (c <= r)


MOD_TN = 1024


def _mod_kernel(c_ref, w_ref, b_ref, o_ref):
    o_ref[...] = _dot(c_ref[...].astype(BF16), w_ref[...].astype(BF16)) + b_ref[...]


def modulation(c_all, w_mod, b_mod):
    rows = c_all.shape[0]
    n = w_mod.shape[-1]
    return pl.pallas_call(
        _mod_kernel,
        grid=(DEPTH, n // MOD_TN),
        in_specs=[pl.BlockSpec((rows, D_MODEL), lambda l, j: (0, 0)),
                  pl.BlockSpec((None, D_MODEL, MOD_TN), lambda l, j: (l, 0, j)),
                  pl.BlockSpec((None, 1, MOD_TN), lambda l, j: (l, 0, j))],
        out_specs=pl.BlockSpec((None, rows, MOD_TN), lambda l, j: (l, 0, j)),
        out_shape=jax.ShapeDtypeStruct((DEPTH, rows, n), F32),
        compiler_params=_cparams(("arbitrary", "arbitrary")),
        name="modulation",
    )(c_all, w_mod, b_mod.reshape(DEPTH, 1, n))


MIXER_BATCH_ROWS = 8


def _batch_rows(bsz, nchunks):
    if nchunks > 1 or bsz % MIXER_BATCH_ROWS:
        return 1
    return MIXER_BATCH_ROWS


def _over_batch(inner, bb, batched):
    def kern(*refs):
        def one(bi):
            inner(*[r.at[bi] if flag else r for r, flag in zip(refs, batched)])
        if bb == 1:
            one(0)
        else:
            def body(bi, carry):
                one(bi)
                return carry
            lax.fori_loop(0, bb, body, 0)
    return kern


def _row_tiling(bsz, seq, target):
    if seq >= target:
        return 1, target
    return min(bsz, target // seq), seq


PROJ_STEP_W = 2 * W_GROUP
PROJ_STEPS = N_MAIN_GROUPS * W_GROUP // PROJ_STEP_W
assert (G_K, G_V) == (N_MAIN_GROUPS - 2, N_MAIN_GROUPS - 1) and PROJ_STEP_W == 2 * W_GROUP


def _inproj_kernel(*refs, bb, lt, cl, transpose_kv, look_ahead):
    if transpose_kv:
        x_ref, shift_ref, scale_ref, w_ref, ws_ref, _, _, proj_ref, small_ref, kt_ref, vt_ref, u_ref = refs
    else:
        x_ref, shift_ref, scale_ref, w_ref, ws_ref, proj_ref, small_ref, u_ref = refs
    j = pl.program_id(2)
    n_l = lt // cl
    last = PROJ_STEPS - 1

    def normalise(slot):
        for it in range(bb * n_l):
            bi, r0 = it // n_l, (it % n_l) * cl
            x = x_ref[bi, r0:r0 + cl, :]
            mu = jnp.mean(x, axis=-1, keepdims=True)
            xc = x - mu
            var = jnp.mean(xc * xc, axis=-1, keepdims=True)
            u = xc * lax.rsqrt(var + EPS) * (1.0 + scale_ref[bi]) + shift_ref[bi]
            u_ref[slot, it * cl:(it + 1) * cl, :] = u.astype(BF16)

    if look_ahead:
        tile = pl.program_id(0) * pl.num_programs(1) + pl.program_id(1)
        cur = tile % 2
        pl.when((tile == 0) & (j == 0))(functools.partial(normalise, 0))
    else:
        cur = 0
        pl.when(j == 0)(functools.partial(normalise, 0))

    @pl.when(j == 0)
    def _():
        small_ref[...] = _dot_nt(u_ref[cur], ws_ref[...]).reshape(bb, lt, SMALL_W)

    if transpose_kv:
        @pl.when(j < last)
        def _():
            proj_ref[...] = _dot_nt(u_ref[cur], w_ref[...]).reshape(bb, lt, PROJ_STEP_W)

        def last_step(slot):
            kv = _dot_nt(w_ref[...], u_ref[slot])
            kt_ref[...] = kv[:W_GROUP].reshape(H_B, D_B, lt)
            vt_ref[...] = kv[W_GROUP:].reshape(H_B, D_B, lt)
            if look_ahead:
                normalise(1 - slot)

        if look_ahead:
            for slot in (0, 1):
                pl.when((j == last) & (cur == slot))(functools.partial(last_step, slot))
        else:
            pl.when(j == last)(functools.partial(last_step, 0))
    else:
        proj_ref[...] = _dot_nt(u_ref[cur], w_ref[...]).reshape(bb, lt, PROJ_STEP_W)


def in_projection(x, mod, wt_main, wt_small, layer, kv_t=None):
    bsz, seq, _ = x.shape
    bb, lt = _row_tiling(bsz, seq, 1024)
    cl = min(lt, 128)
    tm = bb * lt
    transpose_kv = kv_t is not None
    n_l = seq // lt
    ntiles = (bsz // bb) * n_l
    look_ahead = transpose_kv and ntiles > 1
    kern = functools.partial(_inproj_kernel, bb=bb, lt=lt, cl=cl, transpose_kv=transpose_kv,
                             look_ahead=look_ahead)

    def x_tile(b, l, j):
        if not look_ahead:
            return b, l
        tile = jnp.minimum(b * n_l + l + (j == PROJ_STEPS - 1).astype(jnp.int32), ntiles - 1)
        return tile // n_l, tile % n_l

    in_specs = [pl.BlockSpec((bb, lt, D_MODEL), lambda b, l, j: (*x_tile(b, l, j), 0)),
                pl.BlockSpec((bb, 1, D_MODEL), lambda b, l, j: (x_tile(b, l, j)[0], 0, 0)),
                pl.BlockSpec((bb, 1, D_MODEL), lambda b, l, j: (x_tile(b, l, j)[0], 0, 1)),
                pl.BlockSpec((None, PROJ_STEP_W, D_MODEL), lambda b, l, j: (layer, j, 0)),
                pl.BlockSpec((None, SMALL_W, D_MODEL), lambda b, l, j: (layer, 0, 0))]
    small_spec = pl.BlockSpec((bb, lt, SMALL_W), lambda b, l, j: (b, l, 0))
    small_shape = jax.ShapeDtypeStruct((bsz, seq, SMALL_W), F32)
    scratch = [pltpu.VMEM((2 if look_ahead else 1, tm, D_MODEL), BF16)]
    sem = ("arbitrary", "arbitrary", "arbitrary")
    grid = (bsz // bb, seq // lt, PROJ_STEPS)
    if not transpose_kv:
        return pl.pallas_call(
            kern, grid=grid, in_specs=in_specs,
            out_specs=[pl.BlockSpec((bb, lt, PROJ_STEP_W), lambda b, l, j: (b, l, j)), small_spec],
            out_shape=[jax.ShapeDtypeStruct((bsz, seq, N_MAIN_GROUPS * W_GROUP), F32), small_shape],
            scratch_shapes=scratch, compiler_params=_cparams(sem), name="in_projection",
        )(x, mod, mod, wt_main, wt_small)
    assert bb == 1
    any_spec = pl.BlockSpec(memory_space=pl.ANY)
    t_spec = pl.BlockSpec((None, None, H_B, D_B, lt), lambda b, l, j: (layer, b, 0, 0, l))
    t_shape = jax.ShapeDtypeStruct(kv_t[0].shape, F32)
    return pl.pallas_call(
        kern, grid=grid, in_specs=in_specs + [any_spec, any_spec],
        out_specs=[pl.BlockSpec((bb, lt, PROJ_STEP_W), lambda b, l, j: (b, l, jnp.minimum(j, PROJ_STEPS - 2))),
                   small_spec, t_spec, t_spec],
        out_shape=[jax.ShapeDtypeStruct((bsz, seq, G_K * W_GROUP), F32), small_shape, t_shape, t_shape],
        input_output_aliases={5: 2, 6: 3},
        scratch_shapes=scratch, compiler_params=_cparams(sem), name="in_projection_kvt",
    )(x, mod, mod, wt_main, wt_small, kv_t[0], kv_t[1])


def _outproj_kernel(ya_ref, yb_ref, yc_ref, yd_ref, x_ref, gate_ref, w_ref, g_ref, b_ref, o_ref, acc_ref,
                    *, bb, lt, cl):
    tm = bb * lt
    n_l = lt // cl
    n_chunks = bb * n_l
    halves = 2 if n_chunks % 2 == 0 else 1
    ys = [r[...].reshape(tm, W_GROUP) for r in (ya_ref, yb_ref, yc_ref, yd_ref)]
    hr = tm // halves
    for half in range(halves):
        rows = slice(half * hr, (half + 1) * hr)
        acc = _dot(ys[0][rows], w_ref[0])
        for g in range(1, 4):
            acc += _dot(ys[g][rows], w_ref[g])
        acc_ref[rows, :] = acc

    for it in range(n_chunks):
        bi, r0 = it // n_l, (it % n_l) * cl
        o = acc_ref[it * cl:(it + 1) * cl, :]
        v = ALPHA * x_ref[bi, r0:r0 + cl, :] + (1.0 + gate_ref[bi]) * o
        mu = jnp.mean(v, axis=-1, keepdims=True)
        vc = v - mu
        var = jnp.mean(vc * vc, axis=-1, keepdims=True)
        o_ref[bi, r0:r0 + cl, :] = vc * lax.rsqrt(var + EPS) * g_ref[...] + b_ref[...]


def out_projection(ys, x, mod, w_out, ln_g, ln_b, layer):
    bsz, seq, _ = x.shape
    bb, lt = _row_tiling(bsz, seq, 512)
    cl = min(lt, 128)
    kern = functools.partial(_outproj_kernel, bb=bb, lt=lt, cl=cl)
    yspec = pl.BlockSpec((bb, lt, W_GROUP), lambda b, l: (b, l, 0))
    return pl.pallas_call(
        kern,
        grid=(bsz // bb, seq // lt),
        in_specs=[yspec, yspec, yspec, yspec,
                  pl.BlockSpec((bb, lt, D_MODEL), lambda b, l: (b, l, 0)),
                  pl.BlockSpec((bb, 1, D_MODEL), lambda b, l: (b, 0, 2)),
                  pl.BlockSpec((None, 4, W_GROUP, D_MODEL), lambda b, l: (layer, 0, 0, 0)),
                  pl.BlockSpec((1, D_MODEL), lambda b, l: (0, 0)),
                  pl.BlockSpec((1, D_MODEL), lambda b, l: (0, 0))],
        out_specs=pl.BlockSpec((bb, lt, D_MODEL), lambda b, l: (b, l, 0)),
        out_shape=jax.ShapeDtypeStruct((bsz, seq, D_MODEL), F32),
        scratch_shapes=[pltpu.VMEM((bb * lt, D_MODEL), F32)],
        compiler_params=_cparams(("arbitrary", "arbitrary")),
        name="out_projection",
    )(*ys, x, mod, w_out, ln_g.reshape(1, D_MODEL), ln_b.reshape(1, D_MODEL))


CONV_PAD = 8


def _ssd_kernel(z_ref, x_ref, bc_ref, small_ref, cbuf_ref, ssm0_ref, cw_ref, cb_ref, dtb_ref, alog_ref,
                dskip_ref, nw_ref, y_ref, cnew_ref, ssmnew_ref, ext_ref, state_ref, wcat_ref, xbd_ref,
                ccat_ref, bcat_ref, *, q, nchunks):
    c = pl.program_id(1)
    tail = K_A - 1

    @pl.when(c == 0)
    def _():
        ext_ref[CONV_PAD - tail:CONV_PAD, :] = cbuf_ref[...]
        state_ref[...] = jnp.zeros_like(state_ref)
        for h in range(H_A):
            state_ref[h * P_A:(h + 1) * P_A, h * N_A:(h + 1) * N_A] = ssm0_ref[h]

    ext_ref[CONV_PAD:CONV_PAD + q, 0:W_GROUP] = x_ref[...]
    ext_ref[CONV_PAD:CONV_PAD + q, W_GROUP:] = bc_ref[...]
    conv = cb_ref[...]
    for k in range(K_A):
        off = CONV_PAD - tail + k
        conv = conv + cw_ref[k:k + 1, :] * ext_ref[off:off + q, :]
    new_tail = ext_ref[CONV_PAD + q - tail:CONV_PAD + q, :]
    ext_ref[CONV_PAD - tail:CONV_PAD, :] = new_tail

    @pl.when(c == nchunks - 1)
    def _():
        cnew_ref[...] = new_tail

    xbc = _silu(conv)
    xs = xbc[:, :W_GROUP]
    bm = [xbc[:, W_GROUP + g * N_A:W_GROUP + (g + 1) * N_A] for g in range(G_A)]
    cm = [xbc[:, W_GROUP + (G_A + g) * N_A:W_GROUP + (G_A + g + 1) * N_A] for g in range(G_A)]

    lane = _iota((1, SMALL_W), 1)
    head_lanes = lane < H_A
    dt = jnp.where(head_lanes, _softplus(small_ref[...] + dtb_ref[...]), 0.0)
    a = -jnp.exp(alog_ref[...])
    da = dt * a
    acum = _select_left(_lower_tri(q), da)
    eye8 = _iota((8, SMALL_W), 0) == _iota((8, SMALL_W), 1)
    acum_row = _select_nt(eye8, acum)
    last = acum[q - 1:q, :]
    e_acum = jnp.exp(acum)
    w_s = jnp.exp(last - acum)
    e_last = jnp.exp(last)

    expand = _iota((SMALL_W, W_GROUP), 1) // P_A == _iota((SMALL_W, W_GROUP), 0)
    dt_wide = _select_right(dt, expand)
    xdt = xs * dt_wide
    xdt_bf = xdt.astype(BF16)
    col_head = _iota((1, W_GROUP), 1) // P_A
    causal = _lower_tri(q)

    gmat = [_dot_nt(cm[g].astype(BF16), bm[g].astype(BF16)) for g in range(G_A)]
    for h in range(H_A):
        g = h // (H_A // G_A)
        seg = acum[:, h:h + 1] - acum_row[h:h + 1, :]
        decay = jnp.exp(jnp.where(causal, seg, -jnp.inf))
        wcat_ref[:, h * q:(h + 1) * q] = (gmat[g] * decay).astype(BF16)
        xbd_ref[h * q:(h + 1) * q, :] = jnp.where(col_head == h, xdt, 0.0).astype(BF16)
        ccat_ref[:, h * N_A:(h + 1) * N_A] = (cm[g] * e_acum[:, h:h + 1]).astype(BF16)
        bcat_ref[:, h * N_A:(h + 1) * N_A] = (bm[g] * w_s[:, h:h + 1]).astype(BF16)

    y = _dot(wcat_ref[...], xbd_ref[...])
    y = y + _dot_nt(ccat_ref[...], state_ref[...].astype(BF16))
    y = y + dskip_ref[...] * xs

    upd = _dot_tn(xdt_bf, bcat_ref[...])
    for h in range(H_A):
        rs, cs = slice(h * P_A, (h + 1) * P_A), slice(h * N_A, (h + 1) * N_A)
        state_ref[rs, cs] = e_last[:, h:h + 1] * state_ref[rs, cs] + upd[rs, cs]

    @pl.when(c == nchunks - 1)
    def _():
        for h in range(H_A):
            ssmnew_ref[h] = state_ref[h * P_A:(h + 1) * P_A, h * N_A:(h + 1) * N_A]

    yz = y * _silu(z_ref[...])
    ms = jnp.mean(yz * yz, axis=-1, keepdims=True)
    y_ref[...] = (yz * lax.rsqrt(ms + EPS) * nw_ref[...]).astype(BF16)


def mixer_ssd(proj, small, cbuf, ssm0_all, layer, conv_w, conv_b, dt_bias, a_log, d_skip, norm_w):
    bsz, seq, _ = proj.shape
    q = min(seq, 256)
    nchunks = seq // q
    pad = lambda v: jnp.zeros((1, SMALL_W), F32).at[0, :v.shape[0]].set(v)
    bb = _batch_rows(bsz, nchunks)
    kern = _over_batch(functools.partial(_ssd_kernel, q=q, nchunks=nchunks), bb,
                       (True,) * 6 + (False,) * 6 + (True,) * 3 + (False,) * 6)
    col = lambda j: pl.BlockSpec((bb, q, W_GROUP), lambda b, c: (b, c, j))
    full = lambda shape: pl.BlockSpec(shape, lambda b, c: (0,) * len(shape))
    return pl.pallas_call(
        kern,
        grid=(bsz // bb, nchunks),
        in_specs=[col(G_ZA), col(G_XA), col(G_BCA),
                  pl.BlockSpec((bb, q, SMALL_W), lambda b, c: (b, c, 0)),
                  pl.BlockSpec((bb, K_A - 1, CONV_DIM_A), lambda b, c: (b, 0, 0)),
                  pl.BlockSpec((None, bb, H_A, P_A, N_A), lambda b, c: (layer, b, 0, 0, 0)),
                  full((K_A, CONV_DIM_A)), full((1, CONV_DIM_A)), full((1, SMALL_W)), full((1, SMALL_W)),
                  full((1, W_GROUP)), full((1, W_GROUP))],
        out_specs=[pl.BlockSpec((bb, q, W_GROUP), lambda b, c: (b, c, 0)),
                   pl.BlockSpec((bb, K_A - 1, CONV_DIM_A), lambda b, c: (b, 0, 0)),
                   pl.BlockSpec((bb, H_A, P_A, N_A), lambda b, c: (b, 0, 0, 0))],
        out_shape=[jax.ShapeDtypeStruct((bsz, seq, W_GROUP), BF16),
                   jax.ShapeDtypeStruct((bsz, K_A - 1, CONV_DIM_A), F32),
                   jax.ShapeDtypeStruct((bsz, H_A, P_A, N_A), F32)],
        scratch_shapes=[pltpu.VMEM((CONV_PAD + q, CONV_DIM_A), F32),
                        pltpu.VMEM((H_A * P_A, H_A * N_A), F32),
                        pltpu.VMEM((q, H_A * q), BF16),
                        pltpu.VMEM((H_A * q, W_GROUP), BF16),
                        pltpu.VMEM((q, H_A * N_A), BF16),
                        pltpu.VMEM((q, H_A * N_A), BF16)],
        compiler_params=_cparams(("arbitrary", "arbitrary")),
        name="mixer_ssd",
    )(proj, proj, proj, small, cbuf, ssm0_all, conv_w, conv_b.reshape(1, CONV_DIM_A), pad(dt_bias), pad(a_log),
      jnp.repeat(d_skip, P_A).reshape(1, W_GROUP), norm_w.reshape(1, W_GROUP))


def _mlstm_kernel(xc_ref, zc_ref, small_ref, cbuf_ref, c0_ref, n0_ref, m0_ref, cw_ref, cb_ref, wq_ref, wk_ref,
                  wv_ref, igb_ref, fgb_ref, nw_ref, skip_ref, y_ref, cnew_ref, cst_ref, nst_ref, mst_ref,
                  ext_ref, cs_ref, ns_ref, ms_ref, *, q, nchunks):
    c = pl.program_id(1)
    tail = K_C - 1

    @pl.when(c == 0)
    def _():
        ext_ref[CONV_PAD - tail:CONV_PAD, :] = cbuf_ref[...]
        cs_ref[...] = c0_ref[...]
        ns_ref[...] = n0_ref[...]
        ms_ref[...] = m0_ref[...]

    x_in = xc_ref[...]
    ext_ref[CONV_PAD:CONV_PAD + q, :] = x_in
    conv = cb_ref[...]
    for k in range(K_C):
        off = CONV_PAD - tail + k
        conv = conv + cw_ref[k:k + 1, :] * ext_ref[off:off + q, :]
    new_tail = ext_ref[CONV_PAD + q - tail:CONV_PAD + q, :]
    ext_ref[CONV_PAD - tail:CONV_PAD, :] = new_tail

    @pl.when(c == nchunks - 1)
    def _():
        cnew_ref[...] = new_tail

    xconv = _silu(conv)
    sm = small_ref[...]
    ipre = sm + igb_ref[...]
    fpre = sm + fgb_ref[...]
    logf = jnp.minimum(fpre, 0.0) - jnp.log(1.0 + jnp.exp(-jnp.abs(fpre)))
    bcum = _select_left(_lower_tri(q), logf)
    sel_i = _iota((8, SMALL_W), 1) == _iota((8, SMALL_W), 0) + LANE_I
    sel_f = _iota((8, SMALL_W), 1) == _iota((8, SMALL_W), 0) + LANE_F
    r_row = _select_nt(sel_i, ipre) - _select_nt(sel_f, bcum)
    causal = _lower_tri(q)
    scale_k = DH_C ** -0.5
    heads = range(H_C)
    hs = [slice(h * DH_C, (h + 1) * DH_C) for h in heads]

    xh = [xconv[:, hs[h]].astype(BF16) for h in heads]
    qh = [_dot(xh[h], wq_ref[h]) for h in heads]
    kh = [_dot(xh[h], wk_ref[h]) * scale_k for h in heads]
    vh = [_dot(x_in[:, hs[h]].astype(BF16), wv_ref[h]) for h in heads]
    qb = [t.astype(BF16) for t in qh]
    kb = [t.astype(BF16) for t in kh]
    vb = [t.astype(BF16) for t in vh]
    qk = [_dot_nt(qb[h], kb[h]) for h in heads]
    c_old = [cs_ref[h] for h in heads]
    n_old = [ns_ref[h:h + 1, :] for h in heads]
    qc = [_dot_nt(qb[h], c_old[h].astype(BF16)) for h in heads]

    b_col = [bcum[:, LANE_F + h:LANE_F + h + 1] for h in heads]
    m_prev = [ms_ref[:, h:h + 1] for h in heads]
    d = [jnp.where(causal, b_col[h] + r_row[h:h + 1, :], -jnp.inf) for h in heads]
    inter = [b_col[h] + m_prev[h] for h in heads]
    m_t = [jnp.maximum(inter[h], jnp.max(d[h], axis=-1, keepdims=True)) for h in heads]
    w = [jnp.exp(d[h] - m_t[h]) * qk[h] for h in heads]
    gq = [jnp.exp(inter[h] - m_t[h]) for h in heads]
    m_new = [m_t[h][q - 1:q, :] for h in heads]
    b_last = [b_col[h][q - 1:q, :] for h in heads]
    ws = [jnp.exp(b_last[h] - b_col[h] + ipre[:, LANE_I + h:LANE_I + h + 1] - m_new[h]) for h in heads]
    g_last = [jnp.exp(b_last[h] + m_prev[h] - m_new[h]) for h in heads]
    wv = [_dot(w[h].astype(BF16), vb[h]) for h in heads]
    c_upd = [_dot_tn((vh[h] * ws[h]).astype(BF16), kb[h]) for h in heads]

    for h in heads:
        num = wv[h] + gq[h] * qc[h]
        nq = jnp.sum(w[h], axis=-1, keepdims=True) + gq[h] * jnp.sum(qh[h] * n_old[h], axis=-1, keepdims=True)
        hid = num / jnp.maximum(jnp.abs(nq), jnp.exp(-m_t[h]))
        cs_ref[h] = g_last[h] * c_old[h] + c_upd[h]
        ns_ref[h:h + 1, :] = g_last[h] * n_old[h] + jnp.sum(ws[h] * kh[h], axis=0, keepdims=True)
        ms_ref[:, h:h + 1] = m_new[h]

        mu = jnp.mean(hid, axis=-1, keepdims=True)
        hc = hid - mu
        var = jnp.mean(hc * hc, axis=-1, keepdims=True)
        hn = hc * lax.rsqrt(var + EPS) * nw_ref[:, hs[h]]
        yh = (hn + skip_ref[:, hs[h]] * xconv[:, hs[h]]) * _silu(zc_ref[:, hs[h]])
        y_ref[:, hs[h]] = yh.astype(BF16)

    @pl.when(c == nchunks - 1)
    def _():
        cst_ref[...] = cs_ref[...]
        nst_ref[...] = ns_ref[...]
        mst_ref[...] = ms_ref[...]


def mixer_mlstm(proj, small, cbuf, c0_all, layer, n0, m0, conv_w, conv_b, wq, wk, wv, ig_bias, fg_bias, norm_w,
                skip):
    bsz, seq, _ = proj.shape
    q = min(seq, 256)
    nchunks = seq // q
    bb = _batch_rows(bsz, nchunks)
    kern = _over_batch(functools.partial(_mlstm_kernel, q=q, nchunks=nchunks), bb,
                       (True,) * 7 + (False,) * 9 + (True,) * 5 + (False,) * 4)
    pad_at = lambda v, lane: jnp.zeros((1, SMALL_W), F32).at[0, lane:lane + v.shape[0]].set(v)
    col = lambda j: pl.BlockSpec((bb, q, W_GROUP), lambda b, c: (b, c, j))
    full = lambda shape: pl.BlockSpec(shape, lambda b, c: (0,) * len(shape))
    y, cnew, cst, nst, mst = pl.pallas_call(
        kern,
        grid=(bsz // bb, nchunks),
        in_specs=[col(G_XC), col(G_ZC),
                  pl.BlockSpec((bb, q, SMALL_W), lambda b, c: (b, c, 0)),
                  pl.BlockSpec((bb, K_C - 1, W_GROUP), lambda b, c: (b, 0, 0)),
                  pl.BlockSpec((None, bb, H_C, DH_C, DH_C), lambda b, c: (layer, b, 0, 0, 0)),
                  pl.BlockSpec((bb, H_C, DH_C), lambda b, c: (b, 0, 0)),
                  pl.BlockSpec((bb, 1, H_C), lambda b, c: (b, 0, 0)),
                  full((K_C, W_GROUP)), full((1, W_GROUP)),
                  full((H_C, DH_C, DH_C)), full((H_C, DH_C, DH_C)), full((H_C, DH_C, DH_C)),
                  full((1, SMALL_W)), full((1, SMALL_W)), full((1, W_GROUP)), full((1, W_GROUP))],
        out_specs=[pl.BlockSpec((bb, q, W_GROUP), lambda b, c: (b, c, 0)),
                   pl.BlockSpec((bb, K_C - 1, W_GROUP), lambda b, c: (b, 0, 0)),
                   pl.BlockSpec((bb, H_C, DH_C, DH_C), lambda b, c: (b, 0, 0, 0)),
                   pl.BlockSpec((bb, H_C, DH_C), lambda b, c: (b, 0, 0)),
                   pl.BlockSpec((bb, 1, H_C), lambda b, c: (b, 0, 0))],
        out_shape=[jax.ShapeDtypeStruct((bsz, seq, W_GROUP), BF16),
                   jax.ShapeDtypeStruct((bsz, K_C - 1, W_GROUP), F32),
                   jax.ShapeDtypeStruct((bsz, H_C, DH_C, DH_C), F32),
                   jax.ShapeDtypeStruct((bsz, H_C, DH_C), F32),
                   jax.ShapeDtypeStruct((bsz, 1, H_C), F32)],
        scratch_shapes=[pltpu.VMEM((CONV_PAD + q, W_GROUP), F32),
                        pltpu.VMEM((H_C, DH_C, DH_C), F32),
                        pltpu.VMEM((H_C, DH_C), F32),
                        pltpu.VMEM((1, H_C), F32)],
        compiler_params=_cparams(("arbitrary", "arbitrary")),
        name="mixer_mlstm",
    )(proj, proj, small, cbuf, c0_all, n0, m0.reshape(bsz, 1, H_C), conv_w, conv_b.reshape(1, W_GROUP),
      wq.astype(BF16), wk.astype(BF16), wv.astype(BF16), pad_at(ig_bias, LANE_I), pad_at(fg_bias, LANE_F),
      norm_w.reshape(1, W_GROUP), skip.reshape(1, W_GROUP))
    return y, cnew, cst, nst, mst.reshape(bsz, H_C)


CONV_D_PAD = 32


def _conf_kernel(a_ref, b_ref, g_ref, cbuf_ref, cw_ref, cb_ref, lg_ref, lb_ref, y_ref, cnew_ref, ext_ref,
                 win_ref, *, q, nchunks):
    c = pl.program_id(1)
    tail = K_D - 1

    @pl.when(c == 0)
    def _():
        ext_ref[CONV_D_PAD - tail:CONV_D_PAD, :] = cbuf_ref[...]

    ext_ref[CONV_D_PAD:CONV_D_PAD + q, :] = a_ref[...] * _sigmoid(b_ref[...])
    conv = cb_ref[...]
    offsets = [CONV_D_PAD - tail + k for k in range(K_D)]
    for r in range(8):
        taps = [k for k in range(K_D) if offsets[k] % 8 == r]
        if not taps:
            continue
        span = max(offsets[k] for k in taps) - r
        if r == 0:
            window_ref = ext_ref
        else:
            window_ref = win_ref.at[r - 1]
            window_ref[0:span + q, :] = ext_ref[r:r + span + q, :]
        for k in taps:
            a = offsets[k] - r
            conv = conv + cw_ref[k:k + 1, :] * window_ref[a:a + q, :]
    new_tail = ext_ref[CONV_D_PAD + q - tail:CONV_D_PAD + q, :]
    ext_ref[CONV_D_PAD - tail:CONV_D_PAD, :] = new_tail

    @pl.when(c == nchunks - 1)
    def _():
        cnew_ref[...] = new_tail

    mu = jnp.mean(conv, axis=-1, keepdims=True)
    cc = conv - mu
    var = jnp.mean(cc * cc, axis=-1, keepdims=True)
    v = cc * lax.rsqrt(var + EPS) * lg_ref[...] + lb_ref[...]
    y_ref[...] = (_silu(v) * _silu(g_ref[...])).astype(BF16)


def mixer_conformer(proj, cbuf, conv_w, conv_b, ln_g, ln_b):
    bsz, seq, _ = proj.shape
    q = min(seq, 256)
    nchunks = seq // q
    bb = _batch_rows(bsz, nchunks)
    kern = _over_batch(functools.partial(_conf_kernel, q=q, nchunks=nchunks), bb,
                       (True,) * 4 + (False,) * 4 + (True,) * 2 + (False,) * 2)
    col = lambda j: pl.BlockSpec((bb, q, W_GROUP), lambda b, c: (b, c, j))
    full = lambda shape: pl.BlockSpec(shape, lambda b, c: (0,) * len(shape))
    return pl.pallas_call(
        kern,
        grid=(bsz // bb, nchunks),
        in_specs=[col(G_AD), col(G_BD), col(G_GD),
                  pl.BlockSpec((bb, K_D - 1, W_GROUP), lambda b, c: (b, 0, 0)),
                  full((K_D, W_GROUP)), full((1, W_GROUP)), full((1, W_GROUP)), full((1, W_GROUP))],
        out_specs=[pl.BlockSpec((bb, q, W_GROUP), lambda b, c: (b, c, 0)),
                   pl.BlockSpec((bb, K_D - 1, W_GROUP), lambda b, c: (b, 0, 0))],
        out_shape=[jax.ShapeDtypeStruct((bsz, seq, W_GROUP), BF16),
                   jax.ShapeDtypeStruct((bsz, K_D - 1, W_GROUP), F32)],
        scratch_shapes=[pltpu.VMEM((CONV_D_PAD + q, W_GROUP), F32),
                        pltpu.VMEM((7, CONV_D_PAD + q, W_GROUP), F32)],
        compiler_params=_cparams(("arbitrary", "arbitrary")),
        name="mixer_conformer",
    )(proj, proj, proj, cbuf, conv_w, conv_b.reshape(1, W_GROUP), ln_g.reshape(1, W_GROUP),
      ln_b.reshape(1, W_GROUP))


ATT_BLOCK = 256


LOG2E = math.log2(math.e)
Q_SCALE = D_B ** -0.5 * LOG2E


def _stick_blocks(z2s, mask, carries, suffix_mat):
    masks = mask if isinstance(mask, (list, tuple)) else [mask] * len(z2s)
    keep = lambda m, v: v if m is None else jnp.where(m, v, 0.0)
    sps = [jnp.maximum(z2, 0.0) + jnp.log2(1.0 + jnp.exp2(-jnp.abs(z2))) for z2 in z2s]
    sps = [keep(m, sp) for m, sp in zip(masks, sps)]
    suffixes = [_dot(sp.astype(BF16), suffix_mat) for sp in sps]
    ws = [jnp.exp2(z2 - sp - suffix - carry) for z2, sp, suffix, carry in zip(z2s, sps, suffixes, carries)]
    ws = [keep(m, w) for m, w in zip(masks, ws)]
    totals = [suffix[:, 0:1] + sp[:, 0:1] for suffix, sp in zip(suffixes, sps)]
    return ws, [carry + total for carry, total in zip(carries, totals)]


def _suffix_matrix(n):
    return (jnp.arange(n)[:, None] > jnp.arange(n)[None, :]).astype(BF16)


def _attn_prompt_kernel(q_ref, kt_ref, vt_ref, g_ref, sm_ref, y_ref, kb_ref, vb_ref, z_ref, w_ref, acc_ref,
                        carry_ref, *, t, nblk):
    qi = pl.program_id(2)
    heads = range(kb_ref.shape[0])

    @pl.when(qi == 0)
    def _():
        for hh in heads:
            for blk in range(nblk):
                kb_ref[hh, blk] = kt_ref[hh, :, blk * t:(blk + 1) * t].astype(BF16)
                vb_ref[hh, blk] = vt_ref[hh, :, blk * t:(blk + 1) * t].astype(BF16)

    suffix_mat = sm_ref[...]
    qs = [(q_ref[0, :, hh * D_B:(hh + 1) * D_B] * Q_SCALE).astype(BF16) for hh in heads]

    def form_scores(blk, slot):
        for hh in heads:
            z_ref[slot, hh] = _dot(qs[hh], kb_ref[hh, blk])

    def add_values(blk):
        for hh in heads:
            acc_ref[hh] += _dot_nt(w_ref[hh], vb_ref[hh, blk])

    halves = (slice(0, t // 2), slice(t // 2, t))

    def form_weights(z2s, mask):
        tiles = [(hh, rs) for hh in heads for rs in halves]
        masks = None if mask is None else [mask[rs] for _, rs in tiles]
        ws, carries = _stick_blocks([z2s[hh][rs] for hh, rs in tiles], masks,
                                    [carry_ref[hh, rs] for hh, rs in tiles], suffix_mat)
        for (hh, rs), w, carry in zip(tiles, ws, carries):
            w_ref[hh, rs] = w.astype(BF16)
            carry_ref[hh, rs] = carry

    acc_ref[...] = jnp.zeros_like(acc_ref)
    carry_ref[...] = jnp.zeros_like(carry_ref)
    form_scores(qi, 0)
    form_scores(jnp.maximum(qi - 1, 0), 1)
    form_weights([z_ref[0, hh] for hh in heads], _lower_tri(t, strict=True))

    def body(i, carry):
        blk = qi - 1 - i
        z2s = [z_ref[(i + 1) % 2, hh] for hh in heads]
        add_values(blk + 1)
        form_scores(jnp.maximum(blk - 1, 0), i % 2)
        form_weights(z2s, None)
        return carry
    lax.fori_loop(0, qi, body, 0)
    add_values(0)
    out = jnp.concatenate([acc_ref[hh] for hh in heads], axis=-1)
    y_ref[0] = (out * _silu(g_ref[0])).astype(BF16)


def attention_prompt(proj, kt_all, vt_all, layer):
    bsz, seq, _ = proj.shape
    t = min(seq, ATT_BLOCK)
    nq = seq // t
    lanes = 128
    per = W_GROUP // lanes
    hp = lanes // D_B
    kern = functools.partial(_attn_prompt_kernel, t=t, nblk=nq)
    t_spec = pl.BlockSpec((None, None, hp, D_B, seq), lambda b, p, i: (layer, b, p, 0, 0))
    return pl.pallas_call(
        kern,
        grid=(bsz, per, nq),
        in_specs=[pl.BlockSpec((1, t, lanes), lambda b, p, i: (b, i, G_Q * per + p)),
                  t_spec, t_spec,
                  pl.BlockSpec((1, t, lanes), lambda b, p, i: (b, i, G_GB * per + p)),
                  pl.BlockSpec((t, t), lambda b, p, i: (0, 0))],
        out_specs=pl.BlockSpec((1, t, lanes), lambda b, p, i: (b, i, p)),
        out_shape=jax.ShapeDtypeStruct((bsz, seq, W_GROUP), BF16),
        scratch_shapes=[pltpu.VMEM((hp, nq, D_B, t), BF16),
                        pltpu.VMEM((hp, nq, D_B, t), BF16),
                        pltpu.VMEM((2, hp, t, t), F32),
                        pltpu.VMEM((hp, t, t), BF16),
                        pltpu.VMEM((hp, t, D_B), F32),
                        pltpu.VMEM((hp, t, 1), F32)],
        compiler_params=_cparams(("arbitrary", "arbitrary", "arbitrary")),
        name="attention_prompt",
    )(proj, kt_all, vt_all, proj, _suffix_matrix(t))


CACHE_BLOCK = 2048


def _attn_sample_kernel(q_ref, kn_ref, vn_ref, g_ref, kc_ref, vc_ref, sm_ref, smn_ref, _, __,
                        y_ref, knew_ref, vnew_ref, qb_ref, acc_ref, carry_ref, *, lq, nkb):
    j = pl.program_id(1)
    rows = H_B * lq
    heads = range(H_B)
    hl = lambda h: slice(h * D_B, (h + 1) * D_B)

    @pl.when(j == 0)
    def _():
        kn, vn = kn_ref[0], vn_ref[0]
        q2 = (q_ref[0] * Q_SCALE).astype(BF16)
        knb, vnb = kn.astype(BF16), vn.astype(BF16)
        for h in heads:
            qb_ref[h] = q2[:, hl(h)]
            knew_ref[:, h, :] = kn[:, hl(h)]
            vnew_ref[:, h, :] = vn[:, hl(h)]
        z2 = jnp.concatenate([_dot_nt(q2[:, hl(h)], knb[:, hl(h)]) for h in heads], axis=0)
        mask = _iota((rows, lq), 1) < _iota((rows, lq), 0) % lq
        (w,), (carry,) = _stick_blocks([z2], mask, [jnp.zeros((rows, 1), F32)], smn_ref[...])
        wb = w.astype(BF16)
        for h in heads:
            acc_ref[h] = _dot(wb[h * lq:(h + 1) * lq], vnb[:, hl(h)])
        carry_ref[...] = carry

    suffix_mat = sm_ref[...]
    subs = [slice(s * ATT_BLOCK, (s + 1) * ATT_BLOCK) for s in reversed(range(CACHE_BLOCK // ATT_BLOCK))]
    z2s = [jnp.concatenate([_dot(qb_ref[h], kc_ref[h, :, ks].astype(BF16)) for h in heads], axis=0)
           for ks in subs]
    zero = jnp.zeros((rows, 1), F32)
    ws, totals = _stick_blocks(z2s, None, [zero] * len(subs), suffix_mat)
    carry = carry_ref[...]
    accs = [acc_ref[h] for h in heads]
    for ks, w, total in zip(subs, ws, totals):
        wb = (w * jnp.exp2(-carry)).astype(BF16)
        accs = [accs[h] + _dot_nt(wb[h * lq:(h + 1) * lq], vc_ref[h, :, ks].astype(BF16)) for h in heads]
        carry = carry + total
    for h in heads:
        acc_ref[h] = accs[h]
    carry_ref[...] = carry

    @pl.when(j == nkb - 1)
    def _():
        out = jnp.concatenate(accs, axis=-1)
        y_ref[0] = (out * _silu(g_ref[0])).astype(BF16)


def attention_sample(proj, ktc, vtc, knew_all, vnew_all, layer):
    bsz, lq, _ = proj.shape
    past = ktc.shape[-1]
    nkb = past // CACHE_BLOCK
    kern = functools.partial(_attn_sample_kernel, lq=lq, nkb=nkb)
    col = lambda c: pl.BlockSpec((1, lq, W_GROUP), lambda b, j: (b, 0, c))
    cache = pl.BlockSpec((None, None, H_B, D_B, CACHE_BLOCK), lambda b, j: (layer, b, 0, 0, nkb - 1 - j))
    any_spec = pl.BlockSpec(memory_space=pl.ANY)
    new_spec = pl.BlockSpec((None, None, lq, H_B, D_B), lambda b, j: (layer, b, 0, 0, 0))
    new_shape = jax.ShapeDtypeStruct(knew_all.shape, F32)
    return pl.pallas_call(
        kern,
        grid=(bsz, nkb),
        in_specs=[col(G_Q), col(G_K), col(G_V), col(G_GB), cache, cache,
                  pl.BlockSpec((ATT_BLOCK, ATT_BLOCK), lambda b, j: (0, 0)),
                  pl.BlockSpec((lq, lq), lambda b, j: (0, 0)),
                  any_spec, any_spec],
        out_specs=[pl.BlockSpec((1, lq, W_GROUP), lambda b, j: (b, 0, 0)), new_spec, new_spec],
        out_shape=[jax.ShapeDtypeStruct((bsz, lq, W_GROUP), BF16), new_shape, new_shape],
        input_output_aliases={8: 1, 9: 2},
        scratch_shapes=[pltpu.VMEM((H_B, lq, D_B), BF16),
                        pltpu.VMEM((H_B, lq, D_B), F32),
                        pltpu.VMEM((H_B * lq, 1), F32)],
        compiler_params=_cparams(("arbitrary", "arbitrary")),
        name="attention_sample",
    )(proj, proj, proj, proj, ktc, vtc, _suffix_matrix(ATT_BLOCK), _suffix_matrix(lq), knew_all, vnew_all)


SUBLANES = 8


def _repack_kernel(starts_ref, w_ref, dt_ref, if_ref, o_ref, small_ref):
    o_ref[...] = w_ref[0].astype(BF16)

    @pl.when(pl.program_id(1) == 0)
    def _():
        rows = jnp.concatenate([dt_ref[0], if_ref[0]], axis=0)
        pad = jnp.zeros((SMALL_W - rows.shape[0], D_MODEL), F32)
        small_ref[...] = jnp.concatenate([rows, pad], axis=0).astype(BF16)


def _repack_w_in(w_in):
    layers = w_in.shape[0]
    wt = jnp.swapaxes(w_in, -1, -2)
    o_dt = W_GROUP + CONV_DIM_A
    o_q = o_dt + H_A
    o_k, o_v, o_gb, o_xc = o_q + W_GROUP, o_q + 2 * W_GROUP, o_q + 3 * W_GROUP, o_q + 4 * W_GROUP
    o_i = o_xc + 2 * W_GROUP
    o_ad = o_i + 2 * H_C
    starts = [0, W_GROUP, 2 * W_GROUP, o_q, o_gb, o_xc, o_xc + W_GROUP, o_ad, o_ad + W_GROUP, o_ad + 2 * W_GROUP,
              o_k, o_v]
    assert all(s % SUBLANES == 0 for s in starts + [o_dt, o_i]) and o_q - o_dt == o_ad - o_i == SUBLANES
    tiles = jnp.asarray([s // SUBLANES for s in starts], jnp.int32)
    rows8 = lambda start: pl.BlockSpec((pl.Element(1), pl.Element(SUBLANES), pl.Element(D_MODEL)),
                                       lambda l, g, tiles: (l, start, 0))
    return pl.pallas_call(
        _repack_kernel,
        grid_spec=pltpu.PrefetchScalarGridSpec(
            num_scalar_prefetch=1, grid=(layers, N_MAIN_GROUPS),
            in_specs=[pl.BlockSpec((pl.Element(1), pl.Element(W_GROUP), pl.Element(D_MODEL)),
                                   lambda l, g, tiles: (l, tiles[g] * SUBLANES, 0)),
                      rows8(o_dt), rows8(o_i)],
            out_specs=[pl.BlockSpec((None, W_GROUP, D_MODEL), lambda l, g, tiles: (l, g, 0)),
                       pl.BlockSpec((None, SMALL_W, D_MODEL), lambda l, g, tiles: (l, 0, 0))]),
        out_shape=[jax.ShapeDtypeStruct((layers, N_MAIN_GROUPS * W_GROUP, D_MODEL), BF16),
                   jax.ShapeDtypeStruct((layers, SMALL_W, D_MODEL), BF16)],
        compiler_params=_cparams(("arbitrary", "arbitrary")),
        name="repack_w_in",
    )(tiles, wt, wt, wt)


STACKED_WEIGHTS = ("w_main", "w_small", "w_out")


def _mixer_layer(x, mod, layer, kv_t, kv_cache_t, conv_a_buf, ssm0_all, conv_c_buf, mc0_all, mn0, mm0, conv_d_buf,
                 lw):
    if kv_cache_t is None:
        proj, small, kt_all, vt_all = in_projection(x, mod, lw["w_main"], lw["w_small"], layer, kv_t)
        y_b = attention_prompt(proj, kt_all, vt_all, layer)
    else:
        proj, small = in_projection(x, mod, lw["w_main"], lw["w_small"], layer)
        y_b, kt_all, vt_all = attention_sample(proj, kv_cache_t[0], kv_cache_t[1], kv_t[0], kv_t[1], layer)
    y_a, conv_a_new, ssm_new = mixer_ssd(proj, small, conv_a_buf, ssm0_all, layer, lw["conv_a_w"], lw["conv_a_b"],
                                         lw["dt_bias"], lw["a_log"], lw["d_skip"], lw["norm_a_w"])
    y_c, conv_c_new, mc_new, mn_new, mm_new = mixer_mlstm(
        proj, small, conv_c_buf, mc0_all, layer, mn0, mm0, lw["conv_c_w"], lw["conv_c_b"], lw["wq_c"], lw["wk_c"],
        lw["wv_c"], lw["ig_bias"], lw["fg_bias"], lw["norm_c_w"], lw["skip_c"])
    y_d, conv_d_new = mixer_conformer(proj, conv_d_buf, lw["conv_d_w"], lw["conv_d_b"], lw["ln_d_g"],
                                      lw["ln_d_b"])
    x_new = out_projection((y_a, y_b, y_c, y_d), x, mod, lw["w_out"], lw["ln_g"], lw["ln_b"], layer)
    return x_new, (kt_all, vt_all), (conv_a_new, ssm_new, conv_c_new, mc_new, mn_new, mm_new, conv_d_new)


def _run_trunk(x, mods, cache_k, cache_v, st_conv_a, st_ssm, st_conv_c, st_mc, st_mn, st_mm, st_conv_d, weights):
    bsz, seq, _ = x.shape
    outs = [[] for _ in range(7)]
    if cache_k is None:
        kv_cache_t = None
        kv_t = (jnp.zeros((DEPTH, bsz, H_B, D_B, seq), F32),) * 2
    else:
        kv_cache_t = (jnp.transpose(cache_k, (0, 1, 3, 4, 2)), jnp.transpose(cache_v, (0, 1, 3, 4, 2)))
        kv_t = (jnp.zeros((DEPTH, bsz, seq, H_B, D_B), F32),) * 2
    for l in range(DEPTH):
        lw = {name: (w if name in STACKED_WEIGHTS else w[l]) for name, w in weights.items()}
        x, kv_t, new = _mixer_layer(x, mods[l], l, kv_t, kv_cache_t, st_conv_a[l], st_ssm, st_conv_c[l],
                                    st_mc, st_mn[l], st_mm[l], st_conv_d[l], lw)
        for o, t in zip(outs, new):
            o.append(t)
    if cache_k is None:
        kv_new = [jnp.transpose(t, (0, 1, 4, 2, 3)) for t in kv_t]
    else:
        kv_new = list(kv_t)
    return x, kv_new + [jnp.stack(o) for o in outs]


def kernel(x_prompt, x_sample, cache_k, cache_v, state_conv_a, state_ssm, state_conv_c, state_mlstm_c,
           state_mlstm_n, state_mlstm_m, state_conv_d, c_prompt, c_sample, w_mod, b_mod, w_in, conv_a_w,
           conv_a_b, dt_bias, a_log, d_skip, norm_a_w, conv_c_w, conv_c_b, wq_c, wk_c, wv_c, ig_bias, fg_bias,
           norm_c_w, skip_c, conv_d_w, conv_d_b, ln_d_g, ln_d_b, w_out, ln_g, ln_b):
    batch, dec_batch = x_prompt.shape[0], x_sample.shape[0]
    w_main, w_small = _repack_w_in(w_in)
    weights = dict(w_main=w_main, w_small=w_small, conv_a_w=conv_a_w, conv_a_b=conv_a_b, dt_bias=dt_bias,
                   a_log=a_log, d_skip=d_skip, norm_a_w=norm_a_w, conv_c_w=conv_c_w, conv_c_b=conv_c_b,
                   wq_c=wq_c, wk_c=wk_c, wv_c=wv_c, ig_bias=ig_bias, fg_bias=fg_bias, norm_c_w=norm_c_w,
                   skip_c=skip_c, conv_d_w=conv_d_w, conv_d_b=conv_d_b, ln_d_g=ln_d_g, ln_d_b=ln_d_b,
                   w_out=w_out.reshape(DEPTH, 4, W_GROUP, D_MODEL).astype(BF16), ln_g=ln_g, ln_b=ln_b)

    rows = batch + dec_batch
    rows_pad = -(-rows // 8) * 8
    c_all = jnp.concatenate([c_prompt, c_sample, jnp.zeros((rows_pad - rows, D_MODEL), F32)], axis=0)
    mod_all = modulation(c_all, w_mod, b_mod)
    mods_p = mod_all[:, :batch].reshape(DEPTH, batch, 1, 3 * D_MODEL)
    mods_s = mod_all[:, batch:rows].reshape(DEPTH, dec_batch, 1, 3 * D_MODEL)

    def zeros(*shape):
        return jnp.zeros((DEPTH, batch) + shape, F32)

    y_prompt, sp = _run_trunk(x_prompt, mods_p, None, None,
                              zeros(K_A - 1, CONV_DIM_A), zeros(H_A, P_A, N_A), zeros(K_C - 1, W_GROUP),
                              zeros(H_C, DH_C, DH_C), zeros(H_C, DH_C), zeros(H_C), zeros(K_D - 1, W_GROUP),
                              weights)
    y_sample, ss = _run_trunk(x_sample, mods_s, cache_k, cache_v, state_conv_a, state_ssm, state_conv_c,
                              state_mlstm_c, state_mlstm_n, state_mlstm_m, state_conv_d, weights)
    return (y_prompt, y_sample, *sp, *ss)
```

```python
import functools
import math

import jax
import jax.numpy as jnp
from jax import lax
from jax.experimental import pallas as pl
from jax.experimental.pallas import tpu as pltpu

D_MODEL = 2048
DEPTH = 4
W_GROUP = 512
H_A, P_A, N_A, G_A, K_A = 8, 64, 128, 2, 4
CONV_DIM_A = W_GROUP + 2 * G_A * N_A
H_B, D_B = 8, 64
H_C, DH_C, K_C = 4, 128, 4
K_D = 31
ALPHA = (2 * DEPTH) ** 0.25
EPS = 1e-5
N_MAIN_GROUPS = 12
G_ZA, G_XA, G_BCA, G_Q, G_GB, G_XC, G_ZC, G_AD, G_BD, G_GD, G_K, G_V = range(N_MAIN_GROUPS)
LANES = 128
SMALL_W = LANES
LANE_DT, LANE_I, LANE_F = 0, 8, 12

F32 = jnp.float32
BF16 = jnp.bfloat16
V7X_VMEM_BYTES = 64 * 1024 * 1024
VMEM_LIMIT = V7X_VMEM_BYTES * 7 // 8


def _cparams(sem):
    return pltpu.CompilerParams(dimension_semantics=sem, vmem_limit_bytes=VMEM_LIMIT)


def _dot(a, b):
    return jnp.dot(a, b, preferred_element_type=F32)


def _dot_nt(a, b):
    return lax.dot_general(a, b, (((1,), (1,)), ((), ())), preferred_element_type=F32)


def _dot_tn(a, b):
    return lax.dot_general(a, b, (((0,), (0,)), ((), ())), preferred_element_type=F32)


def _split3(a):
    hi = a.astype(BF16)
    r = a - hi.astype(F32)
    mid = r.astype(BF16)
    lo = (r - mid.astype(F32)).astype(BF16)
    return jnp.concatenate([hi, mid, lo], axis=1)


def _sum3(p):
    n = p.shape[1] // 3
    return (p[:, 2 * n:] + p[:, n:2 * n]) + p[:, :n]


def _select_left(sel01, a):
    return _sum3(_dot(sel01.astype(BF16), _split3(a)))


def _select_right(a, sel01):
    s = sel01.astype(BF16)
    return _dot(_split3(a), jnp.concatenate([s, s, s], axis=0))


def _select_nt(sel01, a):
    s = sel01.astype(BF16)
    return _dot_nt(jnp.concatenate([s, s, s], axis=1), _split3(a))


def _sigmoid(x):
    return 1.0 / (1.0 + jnp.exp(-x))


def _silu(x):
    return x * _sigmoid(x)


def _softplus(x):
    return jnp.maximum(x, 0.0) + jnp.log(1.0 + jnp.exp(-jnp.abs(x)))


def _iota(shape, dim):
    return lax.broadcasted_iota(jnp.int32, shape, dim)


def _lower_tri(n, strict=False):
    r, c = _iota((n, n), 0), _iota((n, n), 1)
    return (c < r) if strict else (c <= r)


MOD_TN = 1024


def _mod_kernel(c_ref, w_ref, b_ref, o_ref):
    o_ref[...] = _dot(c_ref[...].astype(BF16), w_ref[...].astype(BF16)) + b_ref[...]


def modulation(c_all, w_mod, b_mod):
    rows = c_all.shape[0]
    n = w_mod.shape[-1]
    return pl.pallas_call(
        _mod_kernel,
        grid=(DEPTH, n // MOD_TN),
        in_specs=[pl.BlockSpec((rows, D_MODEL), lambda l, j: (0, 0)),
                  pl.BlockSpec((None, D_MODEL, MOD_TN), lambda l, j: (l, 0, j)),
                  pl.BlockSpec((None, 1, MOD_TN), lambda l, j: (l, 0, j))],
        out_specs=pl.BlockSpec((None, rows, MOD_TN), lambda l, j: (l, 0, j)),
        out_shape=jax.ShapeDtypeStruct((DEPTH, rows, n), F32),
        compiler_params=_cparams(("arbitrary", "arbitrary")),
        name="modulation",
    )(c_all, w_mod, b_mod.reshape(DEPTH, 1, n))


MIXER_BATCH_ROWS = 8


def _batch_rows(bsz, nchunks):
    if nchunks > 1 or bsz % MIXER_BATCH_ROWS:
        return 1
    return MIXER_BATCH_ROWS


def _over_batch(inner, bb, batched):
    def kern(*refs):
        def one(bi):
            inner(*[r.at[bi] if flag else r for r, flag in zip(refs, batched)])
        if bb == 1:
            one(0)
        else:
            def body(bi, carry):
                one(bi)
                return carry
            lax.fori_loop(0, bb, body, 0)
    return kern


def _row_tiling(bsz, seq, target):
    if seq >= target:
        return 1, target
    return min(bsz, target // seq), seq


PROJ_STEP_W = 2 * W_GROUP
PROJ_STEPS = N_MAIN_GROUPS * W_GROUP // PROJ_STEP_W
assert (G_K, G_V) == (N_MAIN_GROUPS - 2, N_MAIN_GROUPS - 1) and PROJ_STEP_W == 2 * W_GROUP


def _inproj_kernel(*refs, bb, lt, cl, transpose_kv, look_ahead):
    if transpose_kv:
        x_ref, shift_ref, scale_ref, w_ref, ws_ref, _, _, proj_ref, small_ref, kt_ref, vt_ref, u_ref = refs
    else:
        x_ref, shift_ref, scale_ref, w_ref, ws_ref, proj_ref, small_ref, u_ref = refs
    j = pl.program_id(2)
    n_l = lt // cl
    last = PROJ_STEPS - 1

    def normalise(slot):
        for it in range(bb * n_l):
            bi, r0 = it // n_l, (it % n_l) * cl
            x = x_ref[bi, r0:r0 + cl, :]
            mu = jnp.mean(x, axis=-1, keepdims=True)
            xc = x - mu
            var = jnp.mean(xc * xc, axis=-1, keepdims=True)
            u = xc * lax.rsqrt(var + EPS) * (1.0 + scale_ref[bi]) + shift_ref[bi]
            u_ref[slot, it * cl:(it + 1) * cl, :] = u.astype(BF16)

    if look_ahead:
        tile = pl.program_id(0) * pl.num_programs(1) + pl.program_id(1)
        cur = tile % 2
        pl.when((tile == 0) & (j == 0))(functools.partial(normalise, 0))
    else:
        cur = 0
        pl.when(j == 0)(functools.partial(normalise, 0))

    @pl.when(j == 0)
    def _():
        small_ref[...] = _dot_nt(u_ref[cur], ws_ref[...]).reshape(bb, lt, SMALL_W)

    if transpose_kv:
        @pl.when(j < last)
        def _():
            proj_ref[...] = _dot_nt(u_ref[cur], w_ref[...]).reshape(bb, lt, PROJ_STEP_W)

        def last_step(slot):
            kv = _dot_nt(w_ref[...], u_ref[slot])
            kt_ref[...] = kv[:W_GROUP].reshape(H_B, D_B, lt)
            vt_ref[...] = kv[W_GROUP:].reshape(H_B, D_B, lt)
            if look_ahead:
                normalise(1 - slot)

        if look_ahead:
            for slot in (0, 1):
                pl.when((j == last) & (cur == slot))(functools.partial(last_step, slot))
        else:
            pl.when(j == last)(functools.partial(last_step, 0))
    else:
        proj_ref[...] = _dot_nt(u_ref[cur], w_ref[...]).reshape(bb, lt, PROJ_STEP_W)


def in_projection(x, mod, wt_main, wt_small, layer, kv_t=None):
    bsz, seq, _ = x.shape
    bb, lt = _row_tiling(bsz, seq, 1024)
    cl = min(lt, 128)
    tm = bb * lt
    transpose_kv = kv_t is not None
    n_l = seq // lt
    ntiles = (bsz // bb) * n_l
    look_ahead = transpose_kv and ntiles > 1
    kern = functools.partial(_inproj_kernel, bb=bb, lt=lt, cl=cl, transpose_kv=transpose_kv,
                             look_ahead=look_ahead)

    def x_tile(b, l, j):
        if not look_ahead:
            return b, l
        tile = jnp.minimum(b * n_l + l + (j == PROJ_STEPS - 1).astype(jnp.int32), ntiles - 1)
        return tile // n_l, tile % n_l

    in_specs = [pl.BlockSpec((bb, lt, D_MODEL), lambda b, l, j: (*x_tile(b, l, j), 0)),
                pl.BlockSpec((bb, 1, D_MODEL), lambda b, l, j: (x_tile(b, l, j)[0], 0, 0)),
                pl.BlockSpec((bb, 1, D_MODEL), lambda b, l, j: (x_tile(b, l, j)[0], 0, 1)),
                pl.BlockSpec((None, PROJ_STEP_W, D_MODEL), lambda b, l, j: (layer, j, 0)),
                pl.BlockSpec((None, SMALL_W, D_MODEL), lambda b, l, j: (layer, 0, 0))]
    small_spec = pl.BlockSpec((bb, lt, SMALL_W), lambda b, l, j: (b, l, 0))
    small_shape = jax.ShapeDtypeStruct((bsz, seq, SMALL_W), F32)
    scratch = [pltpu.VMEM((2 if look_ahead else 1, tm, D_MODEL), BF16)]
    sem = ("arbitrary", "arbitrary", "arbitrary")
    grid = (bsz // bb, seq // lt, PROJ_STEPS)
    if not transpose_kv:
        return pl.pallas_call(
            kern, grid=grid, in_specs=in_specs,
            out_specs=[pl.BlockSpec((bb, lt, PROJ_STEP_W), lambda b, l, j: (b, l, j)), small_spec],
            out_shape=[jax.ShapeDtypeStruct((bsz, seq, N_MAIN_GROUPS * W_GROUP), F32), small_shape],
            scratch_shapes=scratch, compiler_params=_cparams(sem), name="in_projection",
        )(x, mod, mod, wt_main, wt_small)
    assert bb == 1
    any_spec = pl.BlockSpec(memory_space=pl.ANY)
    t_spec = pl.BlockSpec((None, None, H_B, D_B, lt), lambda b, l, j: (layer, b, 0, 0, l))
    t_shape = jax.ShapeDtypeStruct(kv_t[0].shape, F32)
    return pl.pallas_call(
        kern, grid=grid, in_specs=in_specs + [any_spec, any_spec],
        out_specs=[pl.BlockSpec((bb, lt, PROJ_STEP_W), lambda b, l, j: (b, l, jnp.minimum(j, PROJ_STEPS - 2))),
                   small_spec, t_spec, t_spec],
        out_shape=[jax.ShapeDtypeStruct((bsz, seq, G_K * W_GROUP), F32), small_shape, t_shape, t_shape],
        input_output_aliases={5: 2, 6: 3},
        scratch_shapes=scratch, compiler_params=_cparams(sem), name="in_projection_kvt",
    )(x, mod, mod, wt_main, wt_small, kv_t[0], kv_t[1])


def _outproj_kernel(ya_ref, yb_ref, yc_ref, yd_ref, x_ref, gate_ref, w_ref, g_ref, b_ref, o_ref, acc_ref,
                    *, bb, lt, cl):
    tm = bb * lt
    n_l = lt // cl
    n_chunks = bb * n_l
    halves = 2 if n_chunks % 2 == 0 else 1
    ys = [r[...].reshape(tm, W_GROUP) for r in (ya_ref, yb_ref, yc_ref, yd_ref)]
    hr = tm // halves
    for half in range(halves):
        rows = slice(half * hr, (half + 1) * hr)
        acc = _dot(ys[0][rows], w_ref[0])
        for g in range(1, 4):
            acc += _dot(ys[g][rows], w_ref[g])
        acc_ref[rows, :] = acc

    for it in range(n_chunks):
        bi, r0 = it // n_l, (it % n_l) * cl
        o = acc_ref[it * cl:(it + 1) * cl, :]
        v = ALPHA * x_ref[bi, r0:r0 + cl, :] + (1.0 + gate_ref[bi]) * o
        mu = jnp.mean(v, axis=-1, keepdims=True)
        vc = v - mu
        var = jnp.mean(vc * vc, axis=-1, keepdims=True)
        o_ref[bi, r0:r0 + cl, :] = vc * lax.rsqrt(var + EPS) * g_ref[...] + b_ref[...]


def out_projection(ys, x, mod, w_out, ln_g, ln_b, layer):
    bsz, seq, _ = x.shape
    bb, lt = _row_tiling(bsz, seq, 512)
    cl = min(lt, 128)
    kern = functools.partial(_outproj_kernel, bb=bb, lt=lt, cl=cl)
    yspec = pl.BlockSpec((bb, lt, W_GROUP), lambda b, l: (b, l, 0))
    return pl.pallas_call(
        kern,
        grid=(bsz // bb, seq // lt),
        in_specs=[yspec, yspec, yspec, yspec,
                  pl.BlockSpec((bb, lt, D_MODEL), lambda b, l: (b, l, 0)),
                  pl.BlockSpec((bb, 1, D_MODEL), lambda b, l: (b, 0, 2)),
                  pl.BlockSpec((None, 4, W_GROUP, D_MODEL), lambda b, l: (layer, 0, 0, 0)),
                  pl.BlockSpec((1, D_MODEL), lambda b, l: (0, 0)),
                  pl.BlockSpec((1, D_MODEL), lambda b, l: (0, 0))],
        out_specs=pl.BlockSpec((bb, lt, D_MODEL), lambda b, l: (b, l, 0)),
        out_shape=jax.ShapeDtypeStruct((bsz, seq, D_MODEL), F32),
        scratch_shapes=[pltpu.VMEM((bb * lt, D_MODEL), F32)],
        compiler_params=_cparams(("arbitrary", "arbitrary")),
        name="out_projection",
    )(*ys, x, mod, w_out, ln_g.reshape(1, D_MODEL), ln_b.reshape(1, D_MODEL))


CONV_PAD = 8


def _ssd_kernel(z_ref, x_ref, bc_ref, small_ref, cbuf_ref, ssm0_ref, cw_ref, cb_ref, dtb_ref, alog_ref,
                dskip_ref, nw_ref, y_ref, cnew_ref, ssmnew_ref, ext_ref, state_ref, wcat_ref, xbd_ref,
                ccat_ref, bcat_ref, *, q, nchunks):
    c = pl.program_id(1)
    tail = K_A - 1

    @pl.when(c == 0)
    def _():
        ext_ref[CONV_PAD - tail:CONV_PAD, :] = cbuf_ref[...]
        state_ref[...] = jnp.zeros_like(state_ref)
        for h in range(H_A):
            state_ref[h * P_A:(h + 1) * P_A, h * N_A:(h + 1) * N_A] = ssm0_ref[h]

    ext_ref[CONV_PAD:CONV_PAD + q, 0:W_GROUP] = x_ref[...]
    ext_ref[CONV_PAD:CONV_PAD + q, W_GROUP:] = bc_ref[...]
    conv = cb_ref[...]
    for k in range(K_A):
        off = CONV_PAD - tail + k
        conv = conv + cw_ref[k:k + 1, :] * ext_ref[off:off + q, :]
    new_tail = ext_ref[CONV_PAD + q - tail:CONV_PAD + q, :]
    ext_ref[CONV_PAD - tail:CONV_PAD, :] = new_tail

    @pl.when(c == nchunks - 1)
    def _():
        cnew_ref[...] = new_tail

    xbc = _silu(conv)
    xs = xbc[:, :W_GROUP]
    bm = [xbc[:, W_GROUP + g * N_A:W_GROUP + (g + 1) * N_A] for g in range(G_A)]
    cm = [xbc[:, W_GROUP + (G_A + g) * N_A:W_GROUP + (G_A + g + 1) * N_A] for g in range(G_A)]

    lane = _iota((1, SMALL_W), 1)
    head_lanes = lane < H_A
    dt = jnp.where(head_lanes, _softplus(small_ref[...] + dtb_ref[...]), 0.0)
    a = -jnp.exp(alog_ref[...])
    da = dt * a
    acum = _select_left(_lower_tri(q), da)
    eye8 = _iota((8, SMALL_W), 0) == _iota((8, SMALL_W), 1)
    acum_row = _select_nt(eye8, acum)
    last = acum[q - 1:q, :]
    e_acum = jnp.exp(acum)
    w_s = jnp.exp(last - acum)
    e_last = jnp.exp(last)

    expand = _iota((SMALL_W, W_GROUP), 1) // P_A == _iota((SMALL_W, W_GROUP), 0)
    dt_wide = _select_right(dt, expand)
    xdt = xs * dt_wide
    xdt_bf = xdt.astype(BF16)
    col_head = _iota((1, W_GROUP), 1) // P_A
    causal = _lower_tri(q)

    gmat = [_dot_nt(cm[g].astype(BF16), bm[g].astype(BF16)) for g in range(G_A)]
    for h in range(H_A):
        g = h // (H_A // G_A)
        seg = acum[:, h:h + 1] - acum_row[h:h + 1, :]
        decay = jnp.exp(jnp.where(causal, seg, -jnp.inf))
        wcat_ref[:, h * q:(h + 1) * q] = (gmat[g] * decay).astype(BF16)
        xbd_ref[h * q:(h + 1) * q, :] = jnp.where(col_head == h, xdt, 0.0).astype(BF16)
        ccat_ref[:, h * N_A:(h + 1) * N_A] = (cm[g] * e_acum[:, h:h + 1]).astype(BF16)
        bcat_ref[:, h * N_A:(h + 1) * N_A] = (bm[g] * w_s[:, h:h + 1]).astype(BF16)

    y = _dot(wcat_ref[...], xbd_ref[...])
    y = y + _dot_nt(ccat_ref[...], state_ref[...].astype(BF16))
    y = y + dskip_ref[...] * xs

    upd = _dot_tn(xdt_bf, bcat_ref[...])
    for h in range(H_A):
        rs, cs = slice(h * P_A, (h + 1) * P_A), slice(h * N_A, (h + 1) * N_A)
        state_ref[rs, cs] = e_last[:, h:h + 1] * state_ref[rs, cs] + upd[rs, cs]

    @pl.when(c == nchunks - 1)
    def _():
        for h in range(H_A):
            ssmnew_ref[h] = state_ref[h * P_A:(h + 1) * P_A, h * N_A:(h + 1) * N_A]

    yz = y * _silu(z_ref[...])
    ms = jnp.mean(yz * yz, axis=-1, keepdims=True)
    y_ref[...] = (yz * lax.rsqrt(ms + EPS) * nw_ref[...]).astype(BF16)


def mixer_ssd(proj, small, cbuf, ssm0_all, layer, conv_w, conv_b, dt_bias, a_log, d_skip, norm_w):
    bsz, seq, _ = proj.shape
    q = min(seq, 256)
    nchunks = seq // q
    pad = lambda v: jnp.zeros((1, SMALL_W), F32).at[0, :v.shape[0]].set(v)
    bb = _batch_rows(bsz, nchunks)
    kern = _over_batch(functools.partial(_ssd_kernel, q=q, nchunks=nchunks), bb,
                       (True,) * 6 + (False,) * 6 + (True,) * 3 + (False,) * 6)
    col = lambda j: pl.BlockSpec((bb, q, W_GROUP), lambda b, c: (b, c, j))
    full = lambda shape: pl.BlockSpec(shape, lambda b, c: (0,) * len(shape))
    return pl.pallas_call(
        kern,
        grid=(bsz // bb, nchunks),
        in_specs=[col(G_ZA), col(G_XA), col(G_BCA),
                  pl.BlockSpec((bb, q, SMALL_W), lambda b, c: (b, c, 0)),
                  pl.BlockSpec((bb, K_A - 1, CONV_DIM_A), lambda b, c: (b, 0, 0)),
                  pl.BlockSpec((None, bb, H_A, P_A, N_A), lambda b, c: (layer, b, 0, 0, 0)),
                  full((K_A, CONV_DIM_A)), full((1, CONV_DIM_A)), full((1, SMALL_W)), full((1, SMALL_W)),
                  full((1, W_GROUP)), full((1, W_GROUP))],
        out_specs=[pl.BlockSpec((bb, q, W_GROUP), lambda b, c: (b, c, 0)),
                   pl.BlockSpec((bb, K_A - 1, CONV_DIM_A), lambda b, c: (b, 0, 0)),
                   pl.BlockSpec((bb, H_A, P_A, N_A), lambda b, c: (b, 0, 0, 0))],
        out_shape=[jax.ShapeDtypeStruct((bsz, seq, W_GROUP), BF16),
                   jax.ShapeDtypeStruct((bsz, K_A - 1, CONV_DIM_A), F32),
                   jax.ShapeDtypeStruct((bsz, H_A, P_A, N_A), F32)],
        scratch_shapes=[pltpu.VMEM((CONV_PAD + q, CONV_DIM_A), F32),
                        pltpu.VMEM((H_A * P_A, H_A * N_A), F32),
                        pltpu.VMEM((q, H_A * q), BF16),
                        pltpu.VMEM((H_A * q, W_GROUP), BF16),
                        pltpu.VMEM((q, H_A * N_A), BF16),
                        pltpu.VMEM((q, H_A * N_A), BF16)],
        compiler_params=_cparams(("arbitrary", "arbitrary")),
        name="mixer_ssd",
    )(proj, proj, proj, small, cbuf, ssm0_all, conv_w, conv_b.reshape(1, CONV_DIM_A), pad(dt_bias), pad(a_log),
      jnp.repeat(d_skip, P_A).reshape(1, W_GROUP), norm_w.reshape(1, W_GROUP))


def _mlstm_kernel(xc_ref, zc_ref, small_ref, cbuf_ref, c0_ref, n0_ref, m0_ref, cw_ref, cb_ref, wq_ref, wk_ref,
                  wv_ref, igb_ref, fgb_ref, nw_ref, skip_ref, y_ref, cnew_ref, cst_ref, nst_ref, mst_ref,
                  ext_ref, cs_ref, ns_ref, ms_ref, *, q, nchunks):
    c = pl.program_id(1)
    tail = K_C - 1

    @pl.when(c == 0)
    def _():
        ext_ref[CONV_PAD - tail:CONV_PAD, :] = cbuf_ref[...]
        cs_ref[...] = c0_ref[...]
        ns_ref[...] = n0_ref[...]
        ms_ref[...] = m0_ref[...]

    x_in = xc_ref[...]
    ext_ref[CONV_PAD:CONV_PAD + q, :] = x_in
    conv = cb_ref[...]
    for k in range(K_C):
        off = CONV_PAD - tail + k
        conv = conv + cw_ref[k:k + 1, :] * ext_ref[off:off + q, :]
    new_tail = ext_ref[CONV_PAD + q - tail:CONV_PAD + q, :]
    ext_ref[CONV_PAD - tail:CONV_PAD, :] = new_tail

    @pl.when(c == nchunks - 1)
    def _():
        cnew_ref[...] = new_tail

    xconv = _silu(conv)
    sm = small_ref[...]
    ipre = sm + igb_ref[...]
    fpre = sm + fgb_ref[...]
    logf = jnp.minimum(fpre, 0.0) - jnp.log(1.0 + jnp.exp(-jnp.abs(fpre)))
    bcum = _select_left(_lower_tri(q), logf)
    sel_i = _iota((8, SMALL_W), 1) == _iota((8, SMALL_W), 0) + LANE_I
    sel_f = _iota((8, SMALL_W), 1) == _iota((8, SMALL_W), 0) + LANE_F
    r_row = _select_nt(sel_i, ipre) - _select_nt(sel_f, bcum)
    causal = _lower_tri(q)
    scale_k = DH_C ** -0.5
    heads = range(H_C)
    hs = [slice(h * DH_C, (h + 1) * DH_C) for h in heads]

    xh = [xconv[:, hs[h]].astype(BF16) for h in heads]
    qh = [_dot(xh[h], wq_ref[h]) for h in heads]
    kh = [_dot(xh[h], wk_ref[h]) * scale_k for h in heads]
    vh = [_dot(x_in[:, hs[h]].astype(BF16), wv_ref[h]) for h in heads]
    qb = [t.astype(BF16) for t in qh]
    kb = [t.astype(BF16) for t in kh]
    vb = [t.astype(BF16) for t in vh]
    qk = [_dot_nt(qb[h], kb[h]) for h in heads]
    c_old = [cs_ref[h] for h in heads]
    n_old = [ns_ref[h:h + 1, :] for h in heads]
    qc = [_dot_nt(qb[h], c_old[h].astype(BF16)) for h in heads]

    b_col = [bcum[:, LANE_F + h:LANE_F + h + 1] for h in heads]
    m_prev = [ms_ref[:, h:h + 1] for h in heads]
    d = [jnp.where(causal, b_col[h] + r_row[h:h + 1, :], -jnp.inf) for h in heads]
    inter = [b_col[h] + m_prev[h] for h in heads]
    m_t = [jnp.maximum(inter[h], jnp.max(d[h], axis=-1, keepdims=True)) for h in heads]
    w = [jnp.exp(d[h] - m_t[h]) * qk[h] for h in heads]
    gq = [jnp.exp(inter[h] - m_t[h]) for h in heads]
    m_new = [m_t[h][q - 1:q, :] for h in heads]
    b_last = [b_col[h][q - 1:q, :] for h in heads]
    ws = [jnp.exp(b_last[h] - b_col[h] + ipre[:, LANE_I + h:LANE_I + h + 1] - m_new[h]) for h in heads]
    g_last = [jnp.exp(b_last[h] + m_prev[h] - m_new[h]) for h in heads]
    wv = [_dot(w[h].astype(BF16), vb[h]) for h in heads]
    c_upd = [_dot_tn((vh[h] * ws[h]).astype(BF16), kb[h]) for h in heads]

    for h in heads:
        num = wv[h] + gq[h] * qc[h]
        nq = jnp.sum(w[h], axis=-1, keepdims=True) + gq[h] * jnp.sum(qh[h] * n_old[h], axis=-1, keepdims=True)
        hid = num / jnp.maximum(jnp.abs(nq), jnp.exp(-m_t[h]))
        cs_ref[h] = g_last[h] * c_old[h] + c_upd[h]
        ns_ref[h:h + 1, :] = g_last[h] * n_old[h] + jnp.sum(ws[h] * kh[h], axis=0, keepdims=True)
        ms_ref[:, h:h + 1] = m_new[h]

        mu = jnp.mean(hid, axis=-1, keepdims=True)
        hc = hid - mu
        var = jnp.mean(hc * hc, axis=-1, keepdims=True)
        hn = hc * lax.rsqrt(var + EPS) * nw_ref[:, hs[h]]
        yh = (hn + skip_ref[:, hs[h]] * xconv[:, hs[h]]) * _silu(zc_ref[:, hs[h]])
        y_ref[:, hs[h]] = yh.astype(BF16)

    @pl.when(c == nchunks - 1)
    def _():
        cst_ref[...] = cs_ref[...]
        nst_ref[...] = ns_ref[...]
        mst_ref[...] = ms_ref[...]


def mixer_mlstm(proj, small, cbuf, c0_all, layer, n0, m0, conv_w, conv_b, wq, wk, wv, ig_bias, fg_bias, norm_w,
                skip):
    bsz, seq, _ = proj.shape
    q = min(seq, 256)
    nchunks = seq // q
    bb = _batch_rows(bsz, nchunks)
    kern = _over_batch(functools.partial(_mlstm_kernel, q=q, nchunks=nchunks), bb,
                       (True,) * 7 + (False,) * 9 + (True,) * 5 + (False,) * 4)
    pad_at = lambda v, lane: jnp.zeros((1, SMALL_W), F32).at[0, lane:lane + v.shape[0]].set(v)
    col = lambda j: pl.BlockSpec((bb, q, W_GROUP), lambda b, c: (b, c, j))
    full = lambda shape: pl.BlockSpec(shape, lambda b, c: (0,) * len(shape))
    y, cnew, cst, nst, mst = pl.pallas_call(
        kern,
        grid=(bsz // bb, nchunks),
        in_specs=[col(G_XC), col(G_ZC),
                  pl.BlockSpec((bb, q, SMALL_W), lambda b, c: (b, c, 0)),
                  pl.BlockSpec((bb, K_C - 1, W_GROUP), lambda b, c: (b, 0, 0)),
                  pl.BlockSpec((None, bb, H_C, DH_C, DH_C), lambda b, c: (layer, b, 0, 0, 0)),
                  pl.BlockSpec((bb, H_C, DH_C), lambda b, c: (b, 0, 0)),
                  pl.BlockSpec((bb, 1, H_C), lambda b, c: (b, 0, 0)),
                  full((K_C, W_GROUP)), full((1, W_GROUP)),
                  full((H_C, DH_C, DH_C)), full((H_C, DH_C, DH_C)), full((H_C, DH_C, DH_C)),
                  full((1, SMALL_W)), full((1, SMALL_W)), full((1, W_GROUP)), full((1, W_GROUP))],
        out_specs=[pl.BlockSpec((bb, q, W_GROUP), lambda b, c: (b, c, 0)),
                   pl.BlockSpec((bb, K_C - 1, W_GROUP), lambda b, c: (b, 0, 0)),
                   pl.BlockSpec((bb, H_C, DH_C, DH_C), lambda b, c: (b, 0, 0, 0)),
                   pl.BlockSpec((bb, H_C, DH_C), lambda b, c: (b, 0, 0)),
                   pl.BlockSpec((bb, 1, H_C), lambda b, c: (b, 0, 0))],
        out_shape=[jax.ShapeDtypeStruct((bsz, seq, W_GROUP), BF16),
                   jax.ShapeDtypeStruct((bsz, K_C - 1, W_GROUP), F32),
                   jax.ShapeDtypeStruct((bsz, H_C, DH_C, DH_C), F32),
                   jax.ShapeDtypeStruct((bsz, H_C, DH_C), F32),
                   jax.ShapeDtypeStruct((bsz, 1, H_C), F32)],
        scratch_shapes=[pltpu.VMEM((CONV_PAD + q, W_GROUP), F32),
                        pltpu.VMEM((H_C, DH_C, DH_C), F32),
                        pltpu.VMEM((H_C, DH_C), F32),
                        pltpu.VMEM((1, H_C), F32)],
        compiler_params=_cparams(("arbitrary", "arbitrary")),
        name="mixer_mlstm",
    )(proj, proj, small, cbuf, c0_all, n0, m0.reshape(bsz, 1, H_C), conv_w, conv_b.reshape(1, W_GROUP),
      wq.astype(BF16), wk.astype(BF16), wv.astype(BF16), pad_at(ig_bias, LANE_I), pad_at(fg_bias, LANE_F),
      norm_w.reshape(1, W_GROUP), skip.reshape(1, W_GROUP))
    return y, cnew, cst, nst, mst.reshape(bsz, H_C)


CONV_D_PAD = 32


def _conf_kernel(a_ref, b_ref, g_ref, cbuf_ref, cw_ref, cb_ref, lg_ref, lb_ref, y_ref, cnew_ref, ext_ref,
                 win_ref, *, q, nchunks):
    c = pl.program_id(1)
    tail = K_D - 1

    @pl.when(c == 0)
    def _():
        ext_ref[CONV_D_PAD - tail:CONV_D_PAD, :] = cbuf_ref[...]

    ext_ref[CONV_D_PAD:CONV_D_PAD + q, :] = a_ref[...] * _sigmoid(b_ref[...])
    conv = cb_ref[...]
    offsets = [CONV_D_PAD - tail + k for k in range(K_D)]
    for r in range(8):
        taps = [k for k in range(K_D) if offsets[k] % 8 == r]
        if not taps:
            continue
        span = max(offsets[k] for k in taps) - r
        if r == 0:
            window_ref = ext_ref
        else:
            window_ref = win_ref.at[r - 1]
            window_ref[0:span + q, :] = ext_ref[r:r + span + q, :]
        for k in taps:
            a = offsets[k] - r
            conv = conv + cw_ref[k:k + 1, :] * window_ref[a:a + q, :]
    new_tail = ext_ref[CONV_D_PAD + q - tail:CONV_D_PAD + q, :]
    ext_ref[CONV_D_PAD - tail:CONV_D_PAD, :] = new_tail

    @pl.when(c == nchunks - 1)
    def _():
        cnew_ref[...] = new_tail

    mu = jnp.mean(conv, axis=-1, keepdims=True)
    cc = conv - mu
    var = jnp.mean(cc * cc, axis=-1, keepdims=True)
    v = cc * lax.rsqrt(var + EPS) * lg_ref[...] + lb_ref[...]
    y_ref[...] = (_silu(v) * _silu(g_ref[...])).astype(BF16)


def mixer_conformer(proj, cbuf, conv_w, conv_b, ln_g, ln_b):
    bsz, seq, _ = proj.shape
    q = min(seq, 256)
    nchunks = seq // q
    bb = _batch_rows(bsz, nchunks)
    kern = _over_batch(functools.partial(_conf_kernel, q=q, nchunks=nchunks), bb,
                       (True,) * 4 + (False,) * 4 + (True,) * 2 + (False,) * 2)
    col = lambda j: pl.BlockSpec((bb, q, W_GROUP), lambda b, c: (b, c, j))
    full = lambda shape: pl.BlockSpec(shape, lambda b, c: (0,) * len(shape))
    return pl.pallas_call(
        kern,
        grid=(bsz // bb, nchunks),
        in_specs=[col(G_AD), col(G_BD), col(G_GD),
                  pl.BlockSpec((bb, K_D - 1, W_GROUP), lambda b, c: (b, 0, 0)),
                  full((K_D, W_GROUP)), full((1, W_GROUP)), full((1, W_GROUP)), full((1, W_GROUP))],
        out_specs=[pl.BlockSpec((bb, q, W_GROUP), lambda b, c: (b, c, 0)),
                   pl.BlockSpec((bb, K_D - 1, W_GROUP), lambda b, c: (b, 0, 0))],
        out_shape=[jax.ShapeDtypeStruct((bsz, seq, W_GROUP), BF16),
                   jax.ShapeDtypeStruct((bsz, K_D - 1, W_GROUP), F32)],
        scratch_shapes=[pltpu.VMEM((CONV_D_PAD + q, W_GROUP), F32),
                        pltpu.VMEM((7, CONV_D_PAD + q, W_GROUP), F32)],
        compiler_params=_cparams(("arbitrary", "arbitrary")),
        name="mixer_conformer",
    )(proj, proj, proj, cbuf, conv_w, conv_b.reshape(1, W_GROUP), ln_g.reshape(1, W_GROUP),
      ln_b.reshape(1, W_GROUP))


ATT_BLOCK = 256


LOG2E = math.log2(math.e)
Q_SCALE = D_B ** -0.5 * LOG2E


def _stick_blocks(z2s, mask, carries, suffix_mat):
    masks = mask if isinstance(mask, (list, tuple)) else [mask] * len(z2s)
    keep = lambda m, v: v if m is None else jnp.where(m, v, 0.0)
    sps = [jnp.maximum(z2, 0.0) + jnp.log2(1.0 + jnp.exp2(-jnp.abs(z2))) for z2 in z2s]
    sps = [keep(m, sp) for m, sp in zip(masks, sps)]
    suffixes = [_dot(sp.astype(BF16), suffix_mat) for sp in sps]
    ws = [jnp.exp2(z2 - sp - suffix - carry) for z2, sp, suffix, carry in zip(z2s, sps, suffixes, carries)]
    ws = [keep(m, w) for m, w in zip(masks, ws)]
    totals = [suffix[:, 0:1] + sp[:, 0:1] for suffix, sp in zip(suffixes, sps)]
    return ws, [carry + total for carry, total in zip(carries, totals)]


def _suffix_matrix(n):
    return (jnp.arange(n)[:, None] > jnp.arange(n)[None, :]).astype(BF16)


def _attn_prompt_kernel(q_ref, kt_ref, vt_ref, g_ref, sm_ref, y_ref, kb_ref, vb_ref, z_ref, w_ref, acc_ref,
                        carry_ref, *, t, nblk):
    qi = pl.program_id(2)
    heads = range(kb_ref.shape[0])

    @pl.when(qi == 0)
    def _():
        for hh in heads:
            for blk in range(nblk):
                kb_ref[hh, blk] = kt_ref[hh, :, blk * t:(blk + 1) * t].astype(BF16)
                vb_ref[hh, blk] = vt_ref[hh, :, blk * t:(blk + 1) * t].astype(BF16)

    suffix_mat = sm_ref[...]
    qs = [(q_ref[0, :, hh * D_B:(hh + 1) * D_B] * Q_SCALE).astype(BF16) for hh in heads]

    def form_scores(blk, slot):
        for hh in heads:
            z_ref[slot, hh] = _dot(qs[hh], kb_ref[hh, blk])

    def add_values(blk):
        for hh in heads:
            acc_ref[hh] += _dot_nt(w_ref[hh], vb_ref[hh, blk])

    halves = (slice(0, t // 2), slice(t // 2, t))

    def form_weights(z2s, mask):
        tiles = [(hh, rs) for hh in heads for rs in halves]
        masks = None if mask is None else [mask[rs] for _, rs in tiles]
        ws, carries = _stick_blocks([z2s[hh][rs] for hh, rs in tiles], masks,
                                    [carry_ref[hh, rs] for hh, rs in tiles], suffix_mat)
        for (hh, rs), w, carry in zip(tiles, ws, carries):
            w_ref[hh, rs] = w.astype(BF16)
            carry_ref[hh, rs] = carry

    acc_ref[...] = jnp.zeros_like(acc_ref)
    carry_ref[...] = jnp.zeros_like(carry_ref)
    form_scores(qi, 0)
    form_scores(jnp.maximum(qi - 1, 0), 1)
    form_weights([z_ref[0, hh] for hh in heads], _lower_tri(t, strict=True))

    def body(i, carry):
        blk = qi - 1 - i
        z2s = [z_ref[(i + 1) % 2, hh] for hh in heads]
        add_values(blk + 1)
        form_scores(jnp.maximum(blk - 1, 0), i % 2)
        form_weights(z2s, None)
        return carry
    lax.fori_loop(0, qi, body, 0)
    add_values(0)
    out = jnp.concatenate([acc_ref[hh] for hh in heads], axis=-1)
    y_ref[0] = (out * _silu(g_ref[0])).astype(BF16)


def attention_prompt(proj, kt_all, vt_all, layer):
    bsz, seq, _ = proj.shape
    t = min(seq, ATT_BLOCK)
    nq = seq // t
    lanes = LANES
    per = W_GROUP // lanes
    hp = lanes // D_B
    kern = functools.partial(_attn_prompt_kernel, t=t, nblk=nq)
    t_spec = pl.BlockSpec((None, None, hp, D_B, seq), lambda b, p, i: (layer, b, p, 0, 0))
    return pl.pallas_call(
        kern,
        grid=(bsz, per, nq),
        in_specs=[pl.BlockSpec((1, t, lanes), lambda b, p, i: (b, i, G_Q * per + p)),
                  t_spec, t_spec,
                  pl.BlockSpec((1, t, lanes), lambda b, p, i: (b, i, G_GB * per + p)),
                  pl.BlockSpec((t, t), lambda b, p, i: (0, 0))],
        out_specs=pl.BlockSpec((1, t, lanes), lambda b, p, i: (b, i, p)),
        out_shape=jax.ShapeDtypeStruct((bsz, seq, W_GROUP), BF16),
        scratch_shapes=[pltpu.VMEM((hp, nq, D_B, t), BF16),
                        pltpu.VMEM((hp, nq, D_B, t), BF16),
                        pltpu.VMEM((2, hp, t, t), F32),
                        pltpu.VMEM((hp, t, t), BF16),
                        pltpu.VMEM((hp, t, D_B), F32),
                        pltpu.VMEM((hp, t, 1), F32)],
        compiler_params=_cparams(("arbitrary", "arbitrary", "arbitrary")),
        name="attention_prompt",
    )(proj, kt_all, vt_all, proj, _suffix_matrix(t))


CACHE_BLOCK = 2048


def _attn_sample_kernel(q_ref, kn_ref, vn_ref, g_ref, kc_ref, vc_ref, sm_ref, smn_ref, _, __,
                        y_ref, knew_ref, vnew_ref, qb_ref, acc_ref, carry_ref, *, lq, nkb):
    j = pl.program_id(1)
    rows = H_B * lq
    heads = range(H_B)
    hl = lambda h: slice(h * D_B, (h + 1) * D_B)

    @pl.when(j == 0)
    def _():
        kn, vn = kn_ref[0], vn_ref[0]
        q2 = (q_ref[0] * Q_SCALE).astype(BF16)
        knb, vnb = kn.astype(BF16), vn.astype(BF16)
        for h in heads:
            qb_ref[h] = q2[:, hl(h)]
            knew_ref[:, h, :] = kn[:, hl(h)]
            vnew_ref[:, h, :] = vn[:, hl(h)]
        z2 = jnp.concatenate([_dot_nt(q2[:, hl(h)], knb[:, hl(h)]) for h in heads], axis=0)
        mask = _iota((rows, lq), 1) < _iota((rows, lq), 0) % lq
        (w,), (carry,) = _stick_blocks([z2], mask, [jnp.zeros((rows, 1), F32)], smn_ref[...])
        wb = w.astype(BF16)
        for h in heads:
            acc_ref[h] = _dot(wb[h * lq:(h + 1) * lq], vnb[:, hl(h)])
        carry_ref[...] = carry

    suffix_mat = sm_ref[...]
    subs = [slice(s * ATT_BLOCK, (s + 1) * ATT_BLOCK) for s in reversed(range(CACHE_BLOCK // ATT_BLOCK))]
    z2s = [jnp.concatenate([_dot(qb_ref[h], kc_ref[h, :, ks].astype(BF16)) for h in heads], axis=0)
           for ks in subs]
    zero = jnp.zeros((rows, 1), F32)
    ws, totals = _stick_blocks(z2s, None, [zero] * len(subs), suffix_mat)
    carry = carry_ref[...]
    accs = [acc_ref[h] for h in heads]
    for ks, w, total in zip(subs, ws, totals):
        wb = (w * jnp.exp2(-carry)).astype(BF16)
        accs = [accs[h] + _dot_nt(wb[h * lq:(h + 1) * lq], vc_ref[h, :, ks].astype(BF16)) for h in heads]
        carry = carry + total
    for h in heads:
        acc_ref[h] = accs[h]
    carry_ref[...] = carry

    @pl.when(j == nkb - 1)
    def _():
        out = jnp.concatenate(accs, axis=-1)
        y_ref[0] = (out * _silu(g_ref[0])).astype(BF16)


def attention_sample(proj, ktc, vtc, knew_all, vnew_all, layer):
    bsz, lq, _ = proj.shape
    past = ktc.shape[-1]
    nkb = past // CACHE_BLOCK
    kern = functools.partial(_attn_sample_kernel, lq=lq, nkb=nkb)
    col = lambda c: pl.BlockSpec((1, lq, W_GROUP), lambda b, j: (b, 0, c))
    cache = pl.BlockSpec((None, None, H_B, D_B, CACHE_BLOCK), lambda b, j: (layer, b, 0, 0, nkb - 1 - j))
    any_spec = pl.BlockSpec(memory_space=pl.ANY)
    new_spec = pl.BlockSpec((None, None, lq, H_B, D_B), lambda b, j: (layer, b, 0, 0, 0))
    new_shape = jax.ShapeDtypeStruct(knew_all.shape, F32)
    return pl.pallas_call(
        kern,
        grid=(bsz, nkb),
        in_specs=[col(G_Q), col(G_K), col(G_V), col(G_GB), cache, cache,
                  pl.BlockSpec((ATT_BLOCK, ATT_BLOCK), lambda b, j: (0, 0)),
                  pl.BlockSpec((lq, lq), lambda b, j: (0, 0)),
                  any_spec, any_spec],
        out_specs=[pl.BlockSpec((1, lq, W_GROUP), lambda b, j: (b, 0, 0)), new_spec, new_spec],
        out_shape=[jax.ShapeDtypeStruct((bsz, lq, W_GROUP), BF16), new_shape, new_shape],
        input_output_aliases={8: 1, 9: 2},
        scratch_shapes=[pltpu.VMEM((H_B, lq, D_B), BF16),
                        pltpu.VMEM((H_B, lq, D_B), F32),
                        pltpu.VMEM((H_B * lq, 1), F32)],
        compiler_params=_cparams(("arbitrary", "arbitrary")),
        name="attention_sample",
    )(proj, proj, proj, proj, ktc, vtc, _suffix_matrix(ATT_BLOCK), _suffix_matrix(lq), knew_all, vnew_all)


SUBLANES = 8


def _repack_kernel(starts_ref, w_ref, dt_ref, if_ref, o_ref, small_ref):
    o_ref[...] = w_ref[0].astype(BF16)

    @pl.when(pl.program_id(1) == 0)
    def _():
        rows = jnp.concatenate([dt_ref[0], if_ref[0]], axis=0)
        pad = jnp.zeros((SMALL_W - rows.shape[0], D_MODEL), F32)
        small_ref[...] = jnp.concatenate([rows, pad], axis=0).astype(BF16)


def _repack_w_in(w_in):
    layers = w_in.shape[0]
    wt = jnp.swapaxes(w_in, -1, -2)
    o_dt = W_GROUP + CONV_DIM_A
    o_q = o_dt + H_A
    o_k, o_v, o_gb, o_xc = o_q + W_GROUP, o_q + 2 * W_GROUP, o_q + 3 * W_GROUP, o_q + 4 * W_GROUP
    o_i = o_xc + 2 * W_GROUP
    o_ad = o_i + 2 * H_C
    starts = [0, W_GROUP, 2 * W_GROUP, o_q, o_gb, o_xc, o_xc + W_GROUP, o_ad, o_ad + W_GROUP, o_ad + 2 * W_GROUP,
              o_k, o_v]
    assert all(s % SUBLANES == 0 for s in starts + [o_dt, o_i]) and o_q - o_dt == o_ad - o_i == SUBLANES
    tiles = jnp.asarray([s // SUBLANES for s in starts], jnp.int32)
    rows8 = lambda start: pl.BlockSpec((pl.Element(1), pl.Element(SUBLANES), pl.Element(D_MODEL)),
                                       lambda l, g, tiles: (l, start, 0))
    return pl.pallas_call(
        _repack_kernel,
        grid_spec=pltpu.PrefetchScalarGridSpec(
            num_scalar_prefetch=1, grid=(layers, N_MAIN_GROUPS),
            in_specs=[pl.BlockSpec((pl.Element(1), pl.Element(W_GROUP), pl.Element(D_MODEL)),
                                   lambda l, g, tiles: (l, tiles[g] * SUBLANES, 0)),
                      rows8(o_dt), rows8(o_i)],
            out_specs=[pl.BlockSpec((None, W_GROUP, D_MODEL), lambda l, g, tiles: (l, g, 0)),
                       pl.BlockSpec((None, SMALL_W, D_MODEL), lambda l, g, tiles: (l, 0, 0))]),
        out_shape=[jax.ShapeDtypeStruct((layers, N_MAIN_GROUPS * W_GROUP, D_MODEL), BF16),
                   jax.ShapeDtypeStruct((layers, SMALL_W, D_MODEL), BF16)],
        compiler_params=_cparams(("arbitrary", "arbitrary")),
        name="repack_w_in",
    )(tiles, wt, wt, wt)


STACKED_WEIGHTS = ("w_main", "w_small", "w_out")


def _mixer_layer(x, mod, layer, kv_t, kv_cache_t, conv_a_buf, ssm0_all, conv_c_buf, mc0_all, mn0, mm0, conv_d_buf,
                 lw):
    if kv_cache_t is None:
        proj, small, kt_all, vt_all = in_projection(x, mod, lw["w_main"], lw["w_small"], layer, kv_t)
        y_b = attention_prompt(proj, kt_all, vt_all, layer)
    else:
        proj, small = in_projection(x, mod, lw["w_main"], lw["w_small"], layer)
        y_b, kt_all, vt_all = attention_sample(proj, kv_cache_t[0], kv_cache_t[1], kv_t[0], kv_t[1], layer)
    y_a, conv_a_new, ssm_new = mixer_ssd(proj, small, conv_a_buf, ssm0_all, layer, lw["conv_a_w"], lw["conv_a_b"],
                                         lw["dt_bias"], lw["a_log"], lw["d_skip"], lw["norm_a_w"])
    y_c, conv_c_new, mc_new, mn_new, mm_new = mixer_mlstm(
        proj, small, conv_c_buf, mc0_all, layer, mn0, mm0, lw["conv_c_w"], lw["conv_c_b"], lw["wq_c"], lw["wk_c"],
        lw["wv_c"], lw["ig_bias"], lw["fg_bias"], lw["norm_c_w"], lw["skip_c"])
    y_d, conv_d_new = mixer_conformer(proj, conv_d_buf, lw["conv_d_w"], lw["conv_d_b"], lw["ln_d_g"],
                                      lw["ln_d_b"])
    x_new = out_projection((y_a, y_b, y_c, y_d), x, mod, lw["w_out"], lw["ln_g"], lw["ln_b"], layer)
    return x_new, (kt_all, vt_all), (conv_a_new, ssm_new, conv_c_new, mc_new, mn_new, mm_new, conv_d_new)


def _run_trunk(x, mods, cache_k, cache_v, st_conv_a, st_ssm, st_conv_c, st_mc, st_mn, st_mm, st_conv_d, weights):
    bsz, seq, _ = x.shape
    outs = [[] for _ in range(7)]
    if cache_k is None:
        kv_cache_t = None
        kv_t = (jnp.zeros((DEPTH, bsz, H_B, D_B, seq), F32),) * 2
    else:
        kv_cache_t = (jnp.transpose(cache_k, (0, 1, 3, 4, 2)), jnp.transpose(cache_v, (0, 1, 3, 4, 2)))
        kv_t = (jnp.zeros((DEPTH, bsz, seq, H_B, D_B), F32),) * 2
    for l in range(DEPTH):
        lw = {name: (w if name in STACKED_WEIGHTS else w[l]) for name, w in weights.items()}
        x, kv_t, new = _mixer_layer(x, mods[l], l, kv_t, kv_cache_t, st_conv_a[l], st_ssm, st_conv_c[l],
                                    st_mc, st_mn[l], st_mm[l], st_conv_d[l], lw)
        for o, t in zip(outs, new):
            o.append(t)
    if cache_k is None:
        kv_new = [jnp.transpose(t, (0, 1, 4, 2, 3)) for t in kv_t]
    else:
        kv_new = list(kv_t)
    return x, kv_new + [jnp.stack(o) for o in outs]


def kernel(x_prompt, x_sample, cache_k, cache_v, state_conv_a, state_ssm, state_conv_c, state_mlstm_c,
           state_mlstm_n, state_mlstm_m, state_conv_d, c_prompt, c_sample, w_mod, b_mod, w_in, conv_a_w,
           conv_a_b, dt_bias, a_log, d_skip, norm_a_w, conv_c_w, conv_c_b, wq_c, wk_c, wv_c, ig_bias, fg_bias,
           norm_c_w, skip_c, conv_d_w, conv_d_b, ln_d_g, ln_d_b, w_out, ln_g, ln_b):
    batch, dec_batch = x_prompt.shape[0], x_sample.shape[0]
    w_main, w_small = _repack_w_in(w_in)
    weights = dict(w_main=w_main, w_small=w_small, conv_a_w=conv_a_w, conv_a_b=conv_a_b, dt_bias=dt_bias,
                   a_log=a_log, d_skip=d_skip, norm_a_w=norm_a_w, conv_c_w=conv_c_w, conv_c_b=conv_c_b,
                   wq_c=wq_c, wk_c=wk_c, wv_c=wv_c, ig_bias=ig_bias, fg_bias=fg_bias, norm_c_w=norm_c_w,
                   skip_c=skip_c, conv_d_w=conv_d_w, conv_d_b=conv_d_b, ln_d_g=ln_d_g, ln_d_b=ln_d_b,
                   w_out=w_out.reshape(DEPTH, 4, W_GROUP, D_MODEL).astype(BF16), ln_g=ln_g, ln_b=ln_b)

    rows = batch + dec_batch
    rows_pad = -(-rows // 8) * 8
    c_all = jnp.concatenate([c_prompt, c_sample, jnp.zeros((rows_pad - rows, D_MODEL), F32)], axis=0)
    mod_all = modulation(c_all, w_mod, b_mod)
    mods_p = mod_all[:, :batch].reshape(DEPTH, batch, 1, 3 * D_MODEL)
    mods_s = mod_all[:, batch:rows].reshape(DEPTH, dec_batch, 1, 3 * D_MODEL)

    def zeros(*shape):
        return jnp.zeros((DEPTH, batch) + shape, F32)

    y_prompt, sp = _run_trunk(x_prompt, mods_p, None, None,
                              zeros(K_A - 1, CONV_DIM_A), zeros(H_A, P_A, N_A), zeros(K_C - 1, W_GROUP),
                              zeros(H_C, DH_C, DH_C), zeros(H_C, DH_C), zeros(H_C), zeros(K_D - 1, W_GROUP),
                              weights)
    y_sample, ss = _run_trunk(x_sample, mods_s, cache_k, cache_v, state_conv_a, state_ssm, state_conv_c,
                              state_mlstm_c, state_mlstm_n, state_mlstm_m, state_conv_d, weights)
    return (y_prompt, y_sample, *sp, *ss)
```

```python
import functools
import math

import jax
import jax.numpy as jnp
from jax import lax
from jax.experimental import pallas as pl
from jax.experimental.pallas import tpu as pltpu

D_MODEL = 2048
DEPTH = 4
W_GROUP = 512
H_A, P_A, N_A, G_A, K_A = 8, 64, 128, 2, 4
CONV_DIM_A = W_GROUP + 2 * G_A * N_A
H_B, D_B = 8, 64
H_C, DH_C, K_C = 4, 128, 4
K_D = 31
ALPHA = (2 * DEPTH) ** 0.25
EPS = 1e-5
N_MAIN_GROUPS = 12
G_ZA, G_XA, G_BCA, G_Q, G_GB, G_XC, G_ZC, G_AD, G_BD, G_GD, G_K, G_V = range(N_MAIN_GROUPS)
LANES = 128
SMALL_W = LANES
LANE_DT, LANE_I, LANE_F = 0, 8, 12

F32 = jnp.float32
BF16 = jnp.bfloat16
V7X_VMEM_BYTES = 64 * 1024 * 1024
VMEM_LIMIT = V7X_VMEM_BYTES * 7 // 8


def _cparams(sem):
    return pltpu.CompilerParams(dimension_semantics=sem, vmem_limit_bytes=VMEM_LIMIT)


def _dot(a, b):
    return jnp.dot(a, b, preferred_element_type=F32)


def _dot_nt(a, b):
    return lax.dot_general(a, b, (((1,), (1,)), ((), ())), preferred_element_type=F32)


def _dot_tn(a, b):
    return lax.dot_general(a, b, (((0,), (0,)), ((), ())), preferred_element_type=F32)


def _split3(a):
    hi = a.astype(BF16)
    r = a - hi.astype(F32)
    mid = r.astype(BF16)
    lo = (r - mid.astype(F32)).astype(BF16)
    return jnp.concatenate([hi, mid, lo], axis=1)


def _sum3(p):
    n = p.shape[1] // 3
    return (p[:, 2 * n:] + p[:, n:2 * n]) + p[:, :n]


def _select_left(sel01, a):
    return _sum3(_dot(sel01.astype(BF16), _split3(a)))


def _select_right(a, sel01):
    s = sel01.astype(BF16)
    return _dot(_split3(a), jnp.concatenate([s, s, s], axis=0))


def _select_nt(sel01, a):
    s = sel01.astype(BF16)
    return _dot_nt(jnp.concatenate([s, s, s], axis=1), _split3(a))


def _sigmoid(x):
    return 1.0 / (1.0 + jnp.exp(-x))


def _silu(x):
    return x * _sigmoid(x)


def _softplus(x):
    return jnp.maximum(x, 0.0) + jnp.log(1.0 + jnp.exp(-jnp.abs(x)))


def _iota(shape, dim):
    return lax.broadcasted_iota(jnp.int32, shape, dim)


def _lower_tri(n, strict=False):
    r, c = _iota((n, n), 0), _iota((n, n), 1)
    return (c < r) if strict else (c <= r)


MOD_TN = 1024


def _mod_kernel(c_ref, w_ref, b_ref, o_ref):
    o_ref[...] = _dot(c_ref[...].astype(BF16), w_ref[...].astype(BF16)) + b_ref[...]


def modulation(c_all, w_mod, b_mod):
    rows = c_all.shape[0]
    n = w_mod.shape[-1]
    return pl.pallas_call(
        _mod_kernel,
        grid=(DEPTH, n // MOD_TN),
        in_specs=[pl.BlockSpec((rows, D_MODEL), lambda l, j: (0, 0)),
                  pl.BlockSpec((None, D_MODEL, MOD_TN), lambda l, j: (l, 0, j)),
                  pl.BlockSpec((None, 1, MOD_TN), lambda l, j: (l, 0, j))],
        out_specs=pl.BlockSpec((None, rows, MOD_TN), lambda l, j: (l, 0, j)),
        out_shape=jax.ShapeDtypeStruct((DEPTH, rows, n), F32),
        compiler_params=_cparams(("arbitrary", "arbitrary")),
        name="modulation",
    )(c_all, w_mod, b_mod.reshape(DEPTH, 1, n))


MIXER_BATCH_ROWS = 8


def _batch_rows(bsz, nchunks):
    if nchunks > 1 or bsz % MIXER_BATCH_ROWS:
        return 1
    return MIXER_BATCH_ROWS


def _over_batch(inner, bb, batched):
    def kern(*refs):
        def one(bi):
            inner(*[r.at[bi] if flag else r for r, flag in zip(refs, batched)])
        if bb == 1:
            one(0)
        else:
            def body(bi, carry):
                one(bi)
                return carry
            lax.fori_loop(0, bb, body, 0)
    return kern


def _row_tiling(bsz, seq, target):
    if seq >= target:
        return 1, target
    return min(bsz, target // seq), seq


PROJ_STEP_W = 2 * W_GROUP
PROJ_STEPS = N_MAIN_GROUPS * W_GROUP // PROJ_STEP_W
assert (G_K, G_V) == (N_MAIN_GROUPS - 2, N_MAIN_GROUPS - 1) and PROJ_STEP_W == 2 * W_GROUP


def _inproj_kernel(*refs, bb, lt, cl, transpose_kv, look_ahead):
    if transpose_kv:
        x_ref, shift_ref, scale_ref, w_ref, ws_ref, _, _, proj_ref, small_ref, kt_ref, vt_ref, u_ref = refs
    else:
        x_ref, shift_ref, scale_ref, w_ref, ws_ref, proj_ref, small_ref, u_ref = refs
    j = pl.program_id(2)
    n_l = lt // cl
    last = PROJ_STEPS - 1

    def normalise(slot):
        for it in range(bb * n_l):
            bi, r0 = it // n_l, (it % n_l) * cl
            x = x_ref[bi, r0:r0 + cl, :]
            mu = jnp.mean(x, axis=-1, keepdims=True)
            xc = x - mu
            var = jnp.mean(xc * xc, axis=-1, keepdims=True)
            u = xc * lax.rsqrt(var + EPS) * (1.0 + scale_ref[bi]) + shift_ref[bi]
            u_ref[slot, it * cl:(it + 1) * cl, :] = u.astype(BF16)

    if look_ahead:
        tile = pl.program_id(0) * pl.num_programs(1) + pl.program_id(1)
        cur = tile % 2
        pl.when((tile == 0) & (j == 0))(functools.partial(normalise, 0))
    else:
        cur = 0
        pl.when(j == 0)(functools.partial(normalise, 0))

    @pl.when(j == 0)
    def _():
        small_ref[...] = _dot_nt(u_ref[cur], ws_ref[...]).reshape(bb, lt, SMALL_W)

    if transpose_kv:
        @pl.when(j < last)
        def _():
            proj_ref[...] = _dot_nt(u_ref[cur], w_ref[...]).reshape(bb, lt, PROJ_STEP_W)

        def last_step(slot):
            kv = _dot_nt(w_ref[...], u_ref[slot])
            kt_ref[...] = kv[:W_GROUP].reshape(H_B, D_B, lt)
            vt_ref[...] = kv[W_GROUP:].reshape(H_B, D_B, lt)
            if look_ahead:
                normalise(1 - slot)

        if look_ahead:
            for slot in (0, 1):
                pl.when((j == last) & (cur == slot))(functools.partial(last_step, slot))
        else:
            pl.when(j == last)(functools.partial(last_step, 0))
    else:
        proj_ref[...] = _dot_nt(u_ref[cur], w_ref[...]).reshape(bb, lt, PROJ_STEP_W)


def in_projection(x, mod, wt_main, wt_small, layer, kv_t=None):
    bsz, seq, _ = x.shape
    bb, lt = _row_tiling(bsz, seq, 1024)
    cl = min(lt, 128)
    tm = bb * lt
    transpose_kv = kv_t is not None
    n_l = seq // lt
    ntiles = (bsz // bb) * n_l
    look_ahead = transpose_kv and ntiles > 1
    kern = functools.partial(_inproj_kernel, bb=bb, lt=lt, cl=cl, transpose_kv=transpose_kv,
                             look_ahead=look_ahead)

    def x_tile(b, l, j):
        if not look_ahead:
            return b, l
        tile = jnp.minimum(b * n_l + l + (j == PROJ_STEPS - 1).astype(jnp.int32), ntiles - 1)
        return tile // n_l, tile % n_l

    in_specs = [pl.BlockSpec((bb, lt, D_MODEL), lambda b, l, j: (*x_tile(b, l, j), 0)),
                pl.BlockSpec((bb, 1, D_MODEL), lambda b, l, j: (x_tile(b, l, j)[0], 0, 0)),
                pl.BlockSpec((bb, 1, D_MODEL), lambda b, l, j: (x_tile(b, l, j)[0], 0, 1)),
                pl.BlockSpec((None, PROJ_STEP_W, D_MODEL), lambda b, l, j: (layer, j, 0)),
                pl.BlockSpec((None, SMALL_W, D_MODEL), lambda b, l, j: (layer, 0, 0))]
    small_spec = pl.BlockSpec((bb, lt, SMALL_W), lambda b, l, j: (b, l, 0))
    small_shape = jax.ShapeDtypeStruct((bsz, seq, SMALL_W), F32)
    scratch = [pltpu.VMEM((2 if look_ahead else 1, tm, D_MODEL), BF16)]
    sem = ("arbitrary", "arbitrary", "arbitrary")
    grid = (bsz // bb, seq // lt, PROJ_STEPS)
    if not transpose_kv:
        return pl.pallas_call(
            kern, grid=grid, in_specs=in_specs,
            out_specs=[pl.BlockSpec((bb, lt, PROJ_STEP_W), lambda b, l, j: (b, l, j)), small_spec],
            out_shape=[jax.ShapeDtypeStruct((bsz, seq, N_MAIN_GROUPS * W_GROUP), F32), small_shape],
            scratch_shapes=scratch, compiler_params=_cparams(sem), name="in_projection",
        )(x, mod, mod, wt_main, wt_small)
    assert bb == 1
    any_spec = pl.BlockSpec(memory_space=pl.ANY)
    t_spec = pl.BlockSpec((None, None, H_B, D_B, lt), lambda b, l, j: (layer, b, 0, 0, l))
    t_shape = jax.ShapeDtypeStruct(kv_t[0].shape, F32)
    return pl.pallas_call(
        kern, grid=grid, in_specs=in_specs + [any_spec, any_spec],
        out_specs=[pl.BlockSpec((bb, lt, PROJ_STEP_W), lambda b, l, j: (b, l, jnp.minimum(j, PROJ_STEPS - 2))),
                   small_spec, t_spec, t_spec],
        out_shape=[jax.ShapeDtypeStruct((bsz, seq, G_K * W_GROUP), F32), small_shape, t_shape, t_shape],
        input_output_aliases={5: 2, 6: 3},
        scratch_shapes=scratch, compiler_params=_cparams(sem), name="in_projection_kvt",
    )(x, mod, mod, wt_main, wt_small, kv_t[0], kv_t[1])


def _outproj_kernel(ya_ref, yb_ref, yc_ref, yd_ref, x_ref, gate_ref, w_ref, g_ref, b_ref, o_ref, acc_ref,
                    *, bb, lt, cl):
    tm = bb * lt
    n_l = lt // cl
    n_chunks = bb * n_l
    halves = 2 if n_chunks % 2 == 0 else 1
    ys = [r[...].reshape(tm, W_GROUP) for r in (ya_ref, yb_ref, yc_ref, yd_ref)]
    hr = tm // halves
    for half in range(halves):
        rows = slice(half * hr, (half + 1) * hr)
        acc = _dot(ys[0][rows], w_ref[0])
        for g in range(1, 4):
            acc += _dot(ys[g][rows], w_ref[g])
        acc_ref[rows, :] = acc

    for it in range(n_chunks):
        bi, r0 = it // n_l, (it % n_l) * cl
        o = acc_ref[it * cl:(it + 1) * cl, :]
        v = ALPHA * x_ref[bi, r0:r0 + cl, :] + (1.0 + gate_ref[bi]) * o
        mu = jnp.mean(v, axis=-1, keepdims=True)
        vc = v - mu
        var = jnp.mean(vc * vc, axis=-1, keepdims=True)
        o_ref[bi, r0:r0 + cl, :] = vc * lax.rsqrt(var + EPS) * g_ref[...] + b_ref[...]


def out_projection(ys, x, mod, w_out, ln_g, ln_b, layer):
    bsz, seq, _ = x.shape
    bb, lt = _row_tiling(bsz, seq, 512)
    cl = min(lt, 128)
    kern = functools.partial(_outproj_kernel, bb=bb, lt=lt, cl=cl)
    yspec = pl.BlockSpec((bb, lt, W_GROUP), lambda b, l: (b, l, 0))
    return pl.pallas_call(
        kern,
        grid=(bsz // bb, seq // lt),
        in_specs=[yspec, yspec, yspec, yspec,
                  pl.BlockSpec((bb, lt, D_MODEL), lambda b, l: (b, l, 0)),
                  pl.BlockSpec((bb, 1, D_MODEL), lambda b, l: (b, 0, 2)),
                  pl.BlockSpec((None, 4, W_GROUP, D_MODEL), lambda b, l: (layer, 0, 0, 0)),
                  pl.BlockSpec((1, D_MODEL), lambda b, l: (0, 0)),
                  pl.BlockSpec((1, D_MODEL), lambda b, l: (0, 0))],
        out_specs=pl.BlockSpec((bb, lt, D_MODEL), lambda b, l: (b, l, 0)),
        out_shape=jax.ShapeDtypeStruct((bsz, seq, D_MODEL), F32),
        scratch_shapes=[pltpu.VMEM((bb * lt, D_MODEL), F32)],
        compiler_params=_cparams(("arbitrary", "arbitrary")),
        name="out_projection",
    )(*ys, x, mod, w_out, ln_g.reshape(1, D_MODEL), ln_b.reshape(1, D_MODEL))


CONV_PAD = 8


def _ssd_kernel(z_ref, x_ref, bc_ref, small_ref, cbuf_ref, ssm0_ref, cw_ref, cb_ref, dtb_ref, alog_ref,
                dskip_ref, nw_ref, y_ref, cnew_ref, ssmnew_ref, ext_ref, state_ref, wcat_ref, xbd_ref,
                ccat_ref, bcat_ref, *, q, nchunks):
    c = pl.program_id(1)
    tail = K_A - 1

    @pl.when(c == 0)
    def _():
        ext_ref[CONV_PAD - tail:CONV_PAD, :] = cbuf_ref[...]
        state_ref[...] = jnp.zeros_like(state_ref)
        for h in range(H_A):
            state_ref[h * P_A:(h + 1) * P_A, h * N_A:(h + 1) * N_A] = ssm0_ref[h]

    ext_ref[CONV_PAD:CONV_PAD + q, 0:W_GROUP] = x_ref[...]
    ext_ref[CONV_PAD:CONV_PAD + q, W_GROUP:] = bc_ref[...]
    conv = cb_ref[...]
    for k in range(K_A):
        off = CONV_PAD - tail + k
        conv = conv + cw_ref[k:k + 1, :] * ext_ref[off:off + q, :]
    new_tail = ext_ref[CONV_PAD + q - tail:CONV_PAD + q, :]
    ext_ref[CONV_PAD - tail:CONV_PAD, :] = new_tail

    @pl.when(c == nchunks - 1)
    def _():
        cnew_ref[...] = new_tail

    xbc = _silu(conv)
    xs = xbc[:, :W_GROUP]
    bm = [xbc[:, W_GROUP + g * N_A:W_GROUP + (g + 1) * N_A] for g in range(G_A)]
    cm = [xbc[:, W_GROUP + (G_A + g) * N_A:W_GROUP + (G_A + g + 1) * N_A] for g in range(G_A)]

    lane = _iota((1, SMALL_W), 1)
    head_lanes = lane < H_A
    dt = jnp.where(head_lanes, _softplus(small_ref[...] + dtb_ref[...]), 0.0)
    a = -jnp.exp(alog_ref[...])
    da = dt * a
    acum = _select_left(_lower_tri(q), da)
    eye8 = _iota((8, SMALL_W), 0) == _iota((8, SMALL_W), 1)
    acum_row = _select_nt(eye8, acum)
    last = acum[q - 1:q, :]
    e_acum = jnp.exp(acum)
    w_s = jnp.exp(last - acum)
    e_last = jnp.exp(last)

    expand = _iota((SMALL_W, W_GROUP), 1) // P_A == _iota((SMALL_W, W_GROUP), 0)
    dt_wide = _select_right(dt, expand)
    xdt = xs * dt_wide
    xdt_bf = xdt.astype(BF16)
    col_head = _iota((1, W_GROUP), 1) // P_A
    causal = _lower_tri(q)

    gmat = [_dot_nt(cm[g].astype(BF16), bm[g].astype(BF16)) for g in range(G_A)]
    for h in range(H_A):
        g = h // (H_A // G_A)
        seg = acum[:, h:h + 1] - acum_row[h:h + 1, :]
        decay = jnp.exp(jnp.where(causal, seg, -jnp.inf))
        wcat_ref[:, h * q:(h + 1) * q] = (gmat[g] * decay).astype(BF16)
        xbd_ref[h * q:(h + 1) * q, :] = jnp.where(col_head == h, xdt, 0.0).astype(BF16)
        ccat_ref[:, h * N_A:(h + 1) * N_A] = (cm[g] * e_acum[:, h:h + 1]).astype(BF16)
        bcat_ref[:, h * N_A:(h + 1) * N_A] = (bm[g] * w_s[:, h:h + 1]).astype(BF16)

    y = _dot(wcat_ref[...], xbd_ref[...])
    y = y + _dot_nt(ccat_ref[...], state_ref[...].astype(BF16))
    y = y + dskip_ref[...] * xs

    upd = _dot_tn(xdt_bf, bcat_ref[...])
    for h in range(H_A):
        rs, cs = slice(h * P_A, (h + 1) * P_A), slice(h * N_A, (h + 1) * N_A)
        state_ref[rs, cs] = e_last[:, h:h + 1] * state_ref[rs, cs] + upd[rs, cs]

    @pl.when(c == nchunks - 1)
    def _():
        for h in range(H_A):
            ssmnew_ref[h] = state_ref[h * P_A:(h + 1) * P_A, h * N_A:(h + 1) * N_A]

    yz = y * _silu(z_ref[...])
    ms = jnp.mean(yz * yz, axis=-1, keepdims=True)
    y_ref[...] = (yz * lax.rsqrt(ms + EPS) * nw_ref[...]).astype(BF16)


def mixer_ssd(proj, small, cbuf, ssm0_all, layer, conv_w, conv_b, dt_bias, a_log, d_skip, norm_w):
    bsz, seq, _ = proj.shape
    q = min(seq, 256)
    nchunks = seq // q
    pad = lambda v: jnp.zeros((1, SMALL_W), F32).at[0, :v.shape[0]].set(v)
    bb = _batch_rows(bsz, nchunks)
    kern = _over_batch(functools.partial(_ssd_kernel, q=q, nchunks=nchunks), bb,
                       (True,) * 6 + (False,) * 6 + (True,) * 3 + (False,) * 6)
    col = lambda j: pl.BlockSpec((bb, q, W_GROUP), lambda b, c: (b, c, j))
    full = lambda shape: pl.BlockSpec(shape, lambda b, c: (0,) * len(shape))
    return pl.pallas_call(
        kern,
        grid=(bsz // bb, nchunks),
        in_specs=[col(G_ZA), col(G_XA), col(G_BCA),
                  pl.BlockSpec((bb, q, SMALL_W), lambda b, c: (b, c, 0)),
                  pl.BlockSpec((bb, K_A - 1, CONV_DIM_A), lambda b, c: (b, 0, 0)),
                  pl.BlockSpec((None, bb, H_A, P_A, N_A), lambda b, c: (layer, b, 0, 0, 0)),
                  full((K_A, CONV_DIM_A)), full((1, CONV_DIM_A)), full((1, SMALL_W)), full((1, SMALL_W)),
                  full((1, W_GROUP)), full((1, W_GROUP))],
        out_specs=[pl.BlockSpec((bb, q, W_GROUP), lambda b, c: (b, c, 0)),
                   pl.BlockSpec((bb, K_A - 1, CONV_DIM_A), lambda b, c: (b, 0, 0)),
                   pl.BlockSpec((bb, H_A, P_A, N_A), lambda b, c: (b, 0, 0, 0))],
        out_shape=[jax.ShapeDtypeStruct((bsz, seq, W_GROUP), BF16),
                   jax.ShapeDtypeStruct((bsz, K_A - 1, CONV_DIM_A), F32),
                   jax.ShapeDtypeStruct((bsz, H_A, P_A, N_A), F32)],
        scratch_shapes=[pltpu.VMEM((CONV_PAD + q, CONV_DIM_A), F32),
                        pltpu.VMEM((H_A * P_A, H_A * N_A), F32),
                        pltpu.VMEM((q, H_A * q), BF16),
                        pltpu.VMEM((H_A * q, W_GROUP), BF16),
                        pltpu.VMEM((q, H_A * N_A), BF16),
                        pltpu.VMEM((q, H_A * N_A), BF16)],
        compiler_params=_cparams(("arbitrary", "arbitrary")),
        name="mixer_ssd",
    )(proj, proj, proj, small, cbuf, ssm0_all, conv_w, conv_b.reshape(1, CONV_DIM_A), pad(dt_bias), pad(a_log),
      jnp.repeat(d_skip, P_A).reshape(1, W_GROUP), norm_w.reshape(1, W_GROUP))


def _mlstm_kernel(xc_ref, zc_ref, small_ref, cbuf_ref, c0_ref, n0_ref, m0_ref, cw_ref, cb_ref, wq_ref, wk_ref,
                  wv_ref, igb_ref, fgb_ref, nw_ref, skip_ref, y_ref, cnew_ref, cst_ref, nst_ref, mst_ref,
                  ext_ref, cs_ref, ns_ref, ms_ref, *, q, nchunks):
    c = pl.program_id(1)
    tail = K_C - 1

    @pl.when(c == 0)
    def _():
        ext_ref[CONV_PAD - tail:CONV_PAD, :] = cbuf_ref[...]
        cs_ref[...] = c0_ref[...]
        ns_ref[...] = n0_ref[...]
        ms_ref[...] = m0_ref[...]

    x_in = xc_ref[...]
    ext_ref[CONV_PAD:CONV_PAD + q, :] = x_in
    conv = cb_ref[...]
    for k in range(K_C):
        off = CONV_PAD - tail + k
        conv = conv + cw_ref[k:k + 1, :] * ext_ref[off:off + q, :]
    new_tail = ext_ref[CONV_PAD + q - tail:CONV_PAD + q, :]
    ext_ref[CONV_PAD - tail:CONV_PAD, :] = new_tail

    @pl.when(c == nchunks - 1)
    def _():
        cnew_ref[...] = new_tail

    xconv = _silu(conv)
    sm = small_ref[...]
    ipre = sm + igb_ref[...]
    fpre = sm + fgb_ref[...]
    logf = jnp.minimum(fpre, 0.0) - jnp.log(1.0 + jnp.exp(-jnp.abs(fpre)))
    bcum = _select_left(_lower_tri(q), logf)
    sel_i = _iota((8, SMALL_W), 1) == _iota((8, SMALL_W), 0) + LANE_I
    sel_f = _iota((8, SMALL_W), 1) == _iota((8, SMALL_W), 0) + LANE_F
    r_row = _select_nt(sel_i, ipre) - _select_nt(sel_f, bcum)
    causal = _lower_tri(q)
    scale_k = DH_C ** -0.5
    heads = range(H_C)
    hs = [slice(h * DH_C, (h + 1) * DH_C) for h in heads]

    xh = [xconv[:, hs[h]].astype(BF16) for h in heads]
    qh = [_dot(xh[h], wq_ref[h]) for h in heads]
    kh = [_dot(xh[h], wk_ref[h]) * scale_k for h in heads]
    vh = [_dot(x_in[:, hs[h]].astype(BF16), wv_ref[h]) for h in heads]
    qb = [t.astype(BF16) for t in qh]
    kb = [t.astype(BF16) for t in kh]
    vb = [t.astype(BF16) for t in vh]
    qk = [_dot_nt(qb[h], kb[h]) for h in heads]
    c_old = [cs_ref[h] for h in heads]
    n_old = [ns_ref[h:h + 1, :] for h in heads]
    qc = [_dot_nt(qb[h], c_old[h].astype(BF16)) for h in heads]

    b_col = [bcum[:, LANE_F + h:LANE_F + h + 1] for h in heads]
    m_prev = [ms_ref[:, h:h + 1] for h in heads]
    d = [jnp.where(causal, b_col[h] + r_row[h:h + 1, :], -jnp.inf) for h in heads]
    inter = [b_col[h] + m_prev[h] for h in heads]
    m_t = [jnp.maximum(inter[h], jnp.max(d[h], axis=-1, keepdims=True)) for h in heads]
    w = [jnp.exp(d[h] - m_t[h]) * qk[h] for h in heads]
    gq = [jnp.exp(inter[h] - m_t[h]) for h in heads]
    m_new = [m_t[h][q - 1:q, :] for h in heads]
    b_last = [b_col[h][q - 1:q, :] for h in heads]
    ws = [jnp.exp(b_last[h] - b_col[h] + ipre[:, LANE_I + h:LANE_I + h + 1] - m_new[h]) for h in heads]
    g_last = [jnp.exp(b_last[h] + m_prev[h] - m_new[h]) for h in heads]
    wv = [_dot(w[h].astype(BF16), vb[h]) for h in heads]
    c_upd = [_dot_tn((vh[h] * ws[h]).astype(BF16), kb[h]) for h in heads]

    for h in heads:
        num = wv[h] + gq[h] * qc[h]
        nq = jnp.sum(w[h], axis=-1, keepdims=True) + gq[h] * jnp.sum(qh[h] * n_old[h], axis=-1, keepdims=True)
        hid = num / jnp.maximum(jnp.abs(nq), jnp.exp(-m_t[h]))
        cs_ref[h] = g_last[h] * c_old[h] + c_upd[h]
        ns_ref[h:h + 1, :] = g_last[h] * n_old[h] + jnp.sum(ws[h] * kh[h], axis=0, keepdims=True)
        ms_ref[:, h:h + 1] = m_new[h]

        mu = jnp.mean(hid, axis=-1, keepdims=True)
        hc = hid - mu
        var = jnp.mean(hc * hc, axis=-1, keepdims=True)
        hn = hc * lax.rsqrt(var + EPS) * nw_ref[:, hs[h]]
        yh = (hn + skip_ref[:, hs[h]] * xconv[:, hs[h]]) * _silu(zc_ref[:, hs[h]])
        y_ref[:, hs[h]] = yh.astype(BF16)

    @pl.when(c == nchunks - 1)
    def _():
        cst_ref[...] = cs_ref[...]
        nst_ref[...] = ns_ref[...]
        mst_ref[...] = ms_ref[...]


def mixer_mlstm(proj, small, cbuf, c0_all, layer, n0, m0, conv_w, conv_b, wq, wk, wv, ig_bias, fg_bias, norm_w,
                skip):
    bsz, seq, _ = proj.shape
    q = min(seq, 256)
    nchunks = seq // q
    bb = _batch_rows(bsz, nchunks)
    kern = _over_batch(functools.partial(_mlstm_kernel, q=q, nchunks=nchunks), bb,
                       (True,) * 7 + (False,) * 9 + (True,) * 5 + (False,) * 4)
    pad_at = lambda v, lane: jnp.zeros((1, SMALL_W), F32).at[0, lane:lane + v.shape[0]].set(v)
    col = lambda j: pl.BlockSpec((bb, q, W_GROUP), lambda b, c: (b, c, j))
    full = lambda shape: pl.BlockSpec(shape, lambda b, c: (0,) * len(shape))
    y, cnew, cst, nst, mst = pl.pallas_call(
        kern,
        grid=(bsz // bb, nchunks),
        in_specs=[col(G_XC), col(G_ZC),
                  pl.BlockSpec((bb, q, SMALL_W), lambda b, c: (b, c, 0)),
                  pl.BlockSpec((bb, K_C - 1, W_GROUP), lambda b, c: (b, 0, 0)),
                  pl.BlockSpec((None, bb, H_C, DH_C, DH_C), lambda b, c: (layer, b, 0, 0, 0)),
                  pl.BlockSpec((bb, H_C, DH_C), lambda b, c: (b, 0, 0)),
                  pl.BlockSpec((bb, 1, H_C), lambda b, c: (b, 0, 0)),
                  full((K_C, W_GROUP)), full((1, W_GROUP)),
                  full((H_C, DH_C, DH_C)), full((H_C, DH_C, DH_C)), full((H_C, DH_C, DH_C)),
                  full((1, SMALL_W)), full((1, SMALL_W)), full((1, W_GROUP)), full((1, W_GROUP))],
        out_specs=[pl.BlockSpec((bb, q, W_GROUP), lambda b, c: (b, c, 0)),
                   pl.BlockSpec((bb, K_C - 1, W_GROUP), lambda b, c: (b, 0, 0)),
                   pl.BlockSpec((bb, H_C, DH_C, DH_C), lambda b, c: (b, 0, 0, 0)),
                   pl.BlockSpec((bb, H_C, DH_C), lambda b, c: (b, 0, 0)),
                   pl.BlockSpec((bb, 1, H_C), lambda b, c: (b, 0, 0))],
        out_shape=[jax.ShapeDtypeStruct((bsz, seq, W_GROUP), BF16),
                   jax.ShapeDtypeStruct((bsz, K_C - 1, W_GROUP), F32),
                   jax.ShapeDtypeStruct((bsz, H_C, DH_C, DH_C), F32),
                   jax.ShapeDtypeStruct((bsz, H_C, DH_C), F32),
                   jax.ShapeDtypeStruct((bsz, 1, H_C), F32)],
        scratch_shapes=[pltpu.VMEM((CONV_PAD + q, W_GROUP), F32),
                        pltpu.VMEM((H_C, DH_C, DH_C), F32),
                        pltpu.VMEM((H_C, DH_C), F32),
                        pltpu.VMEM((1, H_C), F32)],
        compiler_params=_cparams(("arbitrary", "arbitrary")),
        name="mixer_mlstm",
    )(proj, proj, small, cbuf, c0_all, n0, m0.reshape(bsz, 1, H_C), conv_w, conv_b.reshape(1, W_GROUP),
      wq.astype(BF16), wk.astype(BF16), wv.astype(BF16), pad_at(ig_bias, LANE_I), pad_at(fg_bias, LANE_F),
      norm_w.reshape(1, W_GROUP), skip.reshape(1, W_GROUP))
    return y, cnew, cst, nst, mst.reshape(bsz, H_C)


CONV_D_PAD = 32


def _conf_kernel(a_ref, b_ref, g_ref, cbuf_ref, cw_ref, cb_ref, lg_ref, lb_ref, y_ref, cnew_ref, ext_ref,
                 win_ref, *, q, nchunks):
    c = pl.program_id(1)
    tail = K_D - 1

    @pl.when(c == 0)
    def _():
        ext_ref[CONV_D_PAD - tail:CONV_D_PAD, :] = cbuf_ref[...]

    ext_ref[CONV_D_PAD:CONV_D_PAD + q, :] = a_ref[...] * _sigmoid(b_ref[...])
    conv = cb_ref[...]
    offsets = [CONV_D_PAD - tail + k for k in range(K_D)]
    for r in range(8):
        taps = [k for k in range(K_D) if offsets[k] % 8 == r]
        if not taps:
            continue
        span = max(offsets[k] for k in taps) - r
        if r == 0:
            window_ref = ext_ref
        else:
            window_ref = win_ref.at[r - 1]
            window_ref[0:span + q, :] = ext_ref[r:r + span + q, :]
        for k in taps:
            a = offsets[k] - r
            conv = conv + cw_ref[k:k + 1, :] * window_ref[a:a + q, :]
    new_tail = ext_ref[CONV_D_PAD + q - tail:CONV_D_PAD + q, :]
    ext_ref[CONV_D_PAD - tail:CONV_D_PAD, :] = new_tail

    @pl.when(c == nchunks - 1)
    def _():
        cnew_ref[...] = new_tail

    mu = jnp.mean(conv, axis=-1, keepdims=True)
    cc = conv - mu
    var = jnp.mean(cc * cc, axis=-1, keepdims=True)
    v = cc * lax.rsqrt(var + EPS) * lg_ref[...] + lb_ref[...]
    y_ref[...] = (_silu(v) * _silu(g_ref[...])).astype(BF16)


def mixer_conformer(proj, cbuf, conv_w, conv_b, ln_g, ln_b):
    bsz, seq, _ = proj.shape
    q = min(seq, 256)
    nchunks = seq // q
    bb = _batch_rows(bsz, nchunks)
    kern = _over_batch(functools.partial(_conf_kernel, q=q, nchunks=nchunks), bb,
                       (True,) * 4 + (False,) * 4 + (True,) * 2 + (False,) * 2)
    col = lambda j: pl.BlockSpec((bb, q, W_GROUP), lambda b, c: (b, c, j))
    full = lambda shape: pl.BlockSpec(shape, lambda b, c: (0,) * len(shape))
    return pl.pallas_call(
        kern,
        grid=(bsz // bb, nchunks),
        in_specs=[col(G_AD), col(G_BD), col(G_GD),
                  pl.BlockSpec((bb, K_D - 1, W_GROUP), lambda b, c: (b, 0, 0)),
                  full((K_D, W_GROUP)), full((1, W_GROUP)), full((1, W_GROUP)), full((1, W_GROUP))],
        out_specs=[pl.BlockSpec((bb, q, W_GROUP), lambda b, c: (b, c, 0)),
                   pl.BlockSpec((bb, K_D - 1, W_GROUP), lambda b, c: (b, 0, 0))],
        out_shape=[jax.ShapeDtypeStruct((bsz, seq, W_GROUP), BF16),
                   jax.ShapeDtypeStruct((bsz, K_D - 1, W_GROUP), F32)],
        scratch_shapes=[pltpu.VMEM((CONV_D_PAD + q, W_GROUP), F32),
                        pltpu.VMEM((7, CONV_D_PAD + q, W_GROUP), F32)],
        compiler_params=_cparams(("arbitrary", "arbitrary")),
        name="mixer_conformer",
    )(proj, proj, proj, cbuf, conv_w, conv_b.reshape(1, W_GROUP), ln_g.reshape(1, W_GROUP),
      ln_b.reshape(1, W_GROUP))


ATT_BLOCK = 256


LOG2E = math.log2(math.e)
Q_SCALE = D_B ** -0.5 * LOG2E


def _stick_blocks(z2s, mask, carries, suffix_mat):
    masks = mask if isinstance(mask, (list, tuple)) else [mask] * len(z2s)
    keep = lambda m, v: v if m is None else jnp.where(m, v, 0.0)
    sps = [jnp.maximum(z2, 0.0) + jnp.log2(1.0 + jnp.exp2(-jnp.abs(z2))) for z2 in z2s]
    sps = [keep(m, sp) for m, sp in zip(masks, sps)]
    suffixes = [_dot(sp.astype(BF16), suffix_mat) for sp in sps]
    ws = [jnp.exp2(z2 - sp - suffix - carry) for z2, sp, suffix, carry in zip(z2s, sps, suffixes, carries)]
    ws = [keep(m, w) for m, w in zip(masks, ws)]
    totals = [suffix[:, 0:1] + sp[:, 0:1] for suffix, sp in zip(suffixes, sps)]
    return ws, [carry + total for carry, total in zip(carries, totals)]


def _suffix_matrix(n):
    return (jnp.arange(n)[:, None] > jnp.arange(n)[None, :]).astype(BF16)


def _attn_prompt_kernel(q_ref, kt_ref, vt_ref, g_ref, sm_ref, y_ref, kb_ref, vb_ref, z_ref, w_ref, acc_ref,
                        carry_ref, *, t, nblk):
    qi = pl.program_id(2)
    heads = range(kb_ref.shape[0])

    @pl.when(qi == 0)
    def _():
        for hh in heads:
            for blk in range(nblk):
                kb_ref[hh, blk] = kt_ref[hh, :, blk * t:(blk + 1) * t].astype(BF16)
                vb_ref[hh, blk] = vt_ref[hh, :, blk * t:(blk + 1) * t].astype(BF16)

    suffix_mat = sm_ref[...]
    qs = [(q_ref[0, :, hh * D_B:(hh + 1) * D_B] * Q_SCALE).astype(BF16) for hh in heads]

    def form_scores(blk, slot):
        for hh in heads:
            z_ref[slot, hh] = _dot(qs[hh], kb_ref[hh, blk])

    def add_values(blk):
        for hh in heads:
            acc_ref[hh] += _dot_nt(w_ref[hh], vb_ref[hh, blk])

    halves = (slice(0, t // 2), slice(t // 2, t))

    def form_weights(z2s, mask):
        tiles = [(hh, rs) for hh in heads for rs in halves]
        masks = None if mask is None else [mask[rs] for _, rs in tiles]
        ws, carries = _stick_blocks([z2s[hh][rs] for hh, rs in tiles], masks,
                                    [carry_ref[hh, rs] for hh, rs in tiles], suffix_mat)
        for (hh, rs), w, carry in zip(tiles, ws, carries):
            w_ref[hh, rs] = w.astype(BF16)
            carry_ref[hh, rs] = carry

    acc_ref[...] = jnp.zeros_like(acc_ref)
    carry_ref[...] = jnp.zeros_like(carry_ref)
    form_scores(qi, 0)
    form_scores(jnp.maximum(qi - 1, 0), 1)
    form_weights([z_ref[0, hh] for hh in heads], _lower_tri(t, strict=True))

    def body(i, carry):
        blk = qi - 1 - i
        z2s = [z_ref[(i + 1) % 2, hh] for hh in heads]
        add_values(blk + 1)
        form_scores(jnp.maximum(blk - 1, 0), i % 2)
        form_weights(z2s, None)
        return carry
    lax.fori_loop(0, qi, body, 0)
    add_values(0)
    out = jnp.concatenate([acc_ref[hh] for hh in heads], axis=-1)
    y_ref[0] = (out * _silu(g_ref[0])).astype(BF16)


def attention_prompt(proj, kt_all, vt_all, layer):
    bsz, seq, _ = proj.shape
    t = min(seq, ATT_BLOCK)
    nq = seq // t
    lanes = LANES
    per = W_GROUP // lanes
    hp = lanes // D_B
    kern = functools.partial(_attn_prompt_kernel, t=t, nblk=nq)
    t_spec = pl.BlockSpec((None, None, hp, D_B, seq), lambda b, p, i: (layer, b, p, 0, 0))
    return pl.pallas_call(
        kern,
        grid=(bsz, per, nq),
        in_specs=[pl.BlockSpec((1, t, lanes), lambda b, p, i: (b, i, G_Q * per + p)),
                  t_spec, t_spec,
                  pl.BlockSpec((1, t, lanes), lambda b, p, i: (b, i, G_GB * per + p)),
                  pl.BlockSpec((t, t), lambda b, p, i: (0, 0))],
        out_specs=pl.BlockSpec((1, t, lanes), lambda b, p, i: (b, i, p)),
        out_shape=jax.ShapeDtypeStruct((bsz, seq, W_GROUP), BF16),
        scratch_shapes=[pltpu.VMEM((hp, nq, D_B, t), BF16),
                        pltpu.VMEM((hp, nq, D_B, t), BF16),
                        pltpu.VMEM((2, hp, t, t), F32),
                        pltpu.VMEM((hp, t, t), BF16),
                        pltpu.VMEM((hp, t, D_B), F32),
                        pltpu.VMEM((hp, t, 1), F32)],
        compiler_params=_cparams(("arbitrary", "arbitrary", "arbitrary")),
        name="attention_prompt",
    )(proj, kt_all, vt_all, proj, _suffix_matrix(t))


CACHE_BLOCK = 4096


def _attn_sample_kernel(q_ref, kn_ref, vn_ref, g_ref, kc_ref, vc_ref, sm_ref, smn_ref, _, __,
                        y_ref, knew_ref, vnew_ref, qb_ref, acc_ref, carry_ref, *, lq, nkb):
    j = pl.program_id(1)
    rows = H_B * lq
    heads = range(H_B)
    hl = lambda h: slice(h * D_B, (h + 1) * D_B)

    @pl.when(j == 0)
    def _():
        kn, vn = kn_ref[0], vn_ref[0]
        q2 = (q_ref[0] * Q_SCALE).astype(BF16)
        knb, vnb = kn.astype(BF16), vn.astype(BF16)
        for h in heads:
            qb_ref[h] = q2[:, hl(h)]
            knew_ref[:, h, :] = kn[:, hl(h)]
            vnew_ref[:, h, :] = vn[:, hl(h)]
        z2 = jnp.concatenate([_dot_nt(q2[:, hl(h)], knb[:, hl(h)]) for h in heads], axis=0)
        mask = _iota((rows, lq), 1) < _iota((rows, lq), 0) % lq
        (w,), (carry,) = _stick_blocks([z2], mask, [jnp.zeros((rows, 1), F32)], smn_ref[...])
        wb = w.astype(BF16)
        for h in heads:
            acc_ref[h] = _dot(wb[h * lq:(h + 1) * lq], vnb[:, hl(h)])
        carry_ref[...] = carry

    suffix_mat = sm_ref[...]
    subs = [slice(s * ATT_BLOCK, (s + 1) * ATT_BLOCK) for s in reversed(range(CACHE_BLOCK // ATT_BLOCK))]
    z2s = [jnp.concatenate([_dot(qb_ref[h], kc_ref[h, :, ks].astype(BF16)) for h in heads], axis=0)
           for ks in subs]
    zero = jnp.zeros((rows, 1), F32)
    ws, totals = _stick_blocks(z2s, None, [zero] * len(subs), suffix_mat)
    carry = carry_ref[...]
    accs = [acc_ref[h] for h in heads]
    for ks, w, total in zip(subs, ws, totals):
        wb = (w * jnp.exp2(-carry)).astype(BF16)
        accs = [accs[h] + _dot_nt(wb[h * lq:(h + 1) * lq], vc_ref[h, :, ks].astype(BF16)) for h in heads]
        carry = carry + total
    for h in heads:
        acc_ref[h] = accs[h]
    carry_ref[...] = carry

    @pl.when(j == nkb - 1)
    def _():
        out = jnp.concatenate(accs, axis=-1)
        y_ref[0] = (out * _silu(g_ref[0])).astype(BF16)


def attention_sample(proj, ktc, vtc, knew_all, vnew_all, layer):
    bsz, lq, _ = proj.shape
    past = ktc.shape[-1]
    nkb = past // CACHE_BLOCK
    kern = functools.partial(_attn_sample_kernel, lq=lq, nkb=nkb)
    col = lambda c: pl.BlockSpec((1, lq, W_GROUP), lambda b, j: (b, 0, c))
    cache = pl.BlockSpec((None, None, H_B, D_B, CACHE_BLOCK), lambda b, j: (layer, b, 0, 0, nkb - 1 - j))
    any_spec = pl.BlockSpec(memory_space=pl.ANY)
    new_spec = pl.BlockSpec((None, None, lq, H_B, D_B), lambda b, j: (layer, b, 0, 0, 0))
    new_shape = jax.ShapeDtypeStruct(knew_all.shape, F32)
    return pl.pallas_call(
        kern,
        grid=(bsz, nkb),
        in_specs=[col(G_Q), col(G_K), col(G_V), col(G_GB), cache, cache,
                  pl.BlockSpec((ATT_BLOCK, ATT_BLOCK), lambda b, j: (0, 0)),
                  pl.BlockSpec((lq, lq), lambda b, j: (0, 0)),
                  any_spec, any_spec],
        out_specs=[pl.BlockSpec((1, lq, W_GROUP), lambda b, j: (b, 0, 0)), new_spec, new_spec],
        out_shape=[jax.ShapeDtypeStruct((bsz, lq, W_GROUP), BF16), new_shape, new_shape],
        input_output_aliases={8: 1, 9: 2},
        scratch_shapes=[pltpu.VMEM((H_B, lq, D_B), BF16),
                        pltpu.VMEM((H_B, lq, D_B), F32),
                        pltpu.VMEM((H_B * lq, 1), F32)],
        compiler_params=_cparams(("arbitrary", "arbitrary")),
        name="attention_sample",
    )(proj, proj, proj, proj, ktc, vtc, _suffix_matrix(ATT_BLOCK), _suffix_matrix(lq), knew_all, vnew_all)


SUBLANES = 8


def _repack_kernel(starts_ref, w_ref, dt_ref, if_ref, o_ref, small_ref):
    o_ref[...] = w_ref[0].astype(BF16)

    @pl.when(pl.program_id(1) == 0)
    def _():
        rows = jnp.concatenate([dt_ref[0], if_ref[0]], axis=0)
        pad = jnp.zeros((SMALL_W - rows.shape[0], D_MODEL), F32)
        small_ref[...] = jnp.concatenate([rows, pad], axis=0).astype(BF16)


def _repack_w_in(w_in):
    layers = w_in.shape[0]
    wt = jnp.swapaxes(w_in, -1, -2)
    o_dt = W_GROUP + CONV_DIM_A
    o_q = o_dt + H_A
    o_k, o_v, o_gb, o_xc = o_q + W_GROUP, o_q + 2 * W_GROUP, o_q + 3 * W_GROUP, o_q + 4 * W_GROUP
    o_i = o_xc + 2 * W_GROUP
    o_ad = o_i + 2 * H_C
    starts = [0, W_GROUP, 2 * W_GROUP, o_q, o_gb, o_xc, o_xc + W_GROUP, o_ad, o_ad + W_GROUP, o_ad + 2 * W_GROUP,
              o_k, o_v]
    assert all(s % SUBLANES == 0 for s in starts + [o_dt, o_i]) and o_q - o_dt == o_ad - o_i == SUBLANES
    tiles = jnp.asarray([s // SUBLANES for s in starts], jnp.int32)
    rows8 = lambda start: pl.BlockSpec((pl.Element(1), pl.Element(SUBLANES), pl.Element(D_MODEL)),
                                       lambda l, g, tiles: (l, start, 0))
    return pl.pallas_call(
        _repack_kernel,
        grid_spec=pltpu.PrefetchScalarGridSpec(
            num_scalar_prefetch=1, grid=(layers, N_MAIN_GROUPS),
            in_specs=[pl.BlockSpec((pl.Element(1), pl.Element(W_GROUP), pl.Element(D_MODEL)),
                                   lambda l, g, tiles: (l, tiles[g] * SUBLANES, 0)),
                      rows8(o_dt), rows8(o_i)],
            out_specs=[pl.BlockSpec((None, W_GROUP, D_MODEL), lambda l, g, tiles: (l, g, 0)),
                       pl.BlockSpec((None, SMALL_W, D_MODEL), lambda l, g, tiles: (l, 0, 0))]),
        out_shape=[jax.ShapeDtypeStruct((layers, N_MAIN_GROUPS * W_GROUP, D_MODEL), BF16),
                   jax.ShapeDtypeStruct((layers, SMALL_W, D_MODEL), BF16)],
        compiler_params=_cparams(("arbitrary", "arbitrary")),
        name="repack_w_in",
    )(tiles, wt, wt, wt)


STACKED_WEIGHTS = ("w_main", "w_small", "w_out")


def _mixer_layer(x, mod, layer, kv_t, kv_cache_t, conv_a_buf, ssm0_all, conv_c_buf, mc0_all, mn0, mm0, conv_d_buf,
                 lw):
    if kv_cache_t is None:
        proj, small, kt_all, vt_all = in_projection(x, mod, lw["w_main"], lw["w_small"], layer, kv_t)
        y_b = attention_prompt(proj, kt_all, vt_all, layer)
    else:
        proj, small = in_projection(x, mod, lw["w_main"], lw["w_small"], layer)
        y_b, kt_all, vt_all = attention_sample(proj, kv_cache_t[0], kv_cache_t[1], kv_t[0], kv_t[1], layer)
    y_a, conv_a_new, ssm_new = mixer_ssd(proj, small, conv_a_buf, ssm0_all, layer, lw["conv_a_w"], lw["conv_a_b"],
                                         lw["dt_bias"], lw["a_log"], lw["d_skip"], lw["norm_a_w"])
    y_c, conv_c_new, mc_new, mn_new, mm_new = mixer_mlstm(
        proj, small, conv_c_buf, mc0_all, layer, mn0, mm0, lw["conv_c_w"], lw["conv_c_b"], lw["wq_c"], lw["wk_c"],
        lw["wv_c"], lw["ig_bias"], lw["fg_bias"], lw["norm_c_w"], lw["skip_c"])
    y_d, conv_d_new = mixer_conformer(proj, conv_d_buf, lw["conv_d_w"], lw["conv_d_b"], lw["ln_d_g"],
                                      lw["ln_d_b"])
    x_new = out_projection((y_a, y_b, y_c, y_d), x, mod, lw["w_out"], lw["ln_g"], lw["ln_b"], layer)
    return x_new, (kt_all, vt_all), (conv_a_new, ssm_new, conv_c_new, mc_new, mn_new, mm_new, conv_d_new)


def _run_trunk(x, mods, cache_k, cache_v, st_conv_a, st_ssm, st_conv_c, st_mc, st_mn, st_mm, st_conv_d, weights):
    bsz, seq, _ = x.shape
    outs = [[] for _ in range(7)]
    if cache_k is None:
        kv_cache_t = None
        kv_t = (jnp.zeros((DEPTH, bsz, H_B, D_B, seq), F32),) * 2
    else:
        kv_cache_t = (jnp.transpose(cache_k, (0, 1, 3, 4, 2)), jnp.transpose(cache_v, (0, 1, 3, 4, 2)))
        kv_t = (jnp.zeros((DEPTH, bsz, seq, H_B, D_B), F32),) * 2
    for l in range(DEPTH):
        lw = {name: (w if name in STACKED_WEIGHTS else w[l]) for name, w in weights.items()}
        x, kv_t, new = _mixer_layer(x, mods[l], l, kv_t, kv_cache_t, st_conv_a[l], st_ssm, st_conv_c[l],
                                    st_mc, st_mn[l], st_mm[l], st_conv_d[l], lw)
        for o, t in zip(outs, new):
            o.append(t)
    if cache_k is None:
        kv_new = [jnp.transpose(t, (0, 1, 4, 2, 3)) for t in kv_t]
    else:
        kv_new = list(kv_t)
    return x, kv_new + [jnp.stack(o) for o in outs]


def kernel(x_prompt, x_sample, cache_k, cache_v, state_conv_a, state_ssm, state_conv_c, state_mlstm_c,
           state_mlstm_n, state_mlstm_m, state_conv_d, c_prompt, c_sample, w_mod, b_mod, w_in, conv_a_w,
           conv_a_b, dt_bias, a_log, d_skip, norm_a_w, conv_c_w, conv_c_b, wq_c, wk_c, wv_c, ig_bias, fg_bias,
           norm_c_w, skip_c, conv_d_w, conv_d_b, ln_d_g, ln_d_b, w_out, ln_g, ln_b):
    batch, dec_batch = x_prompt.shape[0], x_sample.shape[0]
    w_main, w_small = _repack_w_in(w_in)
    weights = dict(w_main=w_main, w_small=w_small, conv_a_w=conv_a_w, conv_a_b=conv_a_b, dt_bias=dt_bias,
                   a_log=a_log, d_skip=d_skip, norm_a_w=norm_a_w, conv_c_w=conv_c_w, conv_c_b=conv_c_b,
                   wq_c=wq_c, wk_c=wk_c, wv_c=wv_c, ig_bias=ig_bias, fg_bias=fg_bias, norm_c_w=norm_c_w,
                   skip_c=skip_c, conv_d_w=conv_d_w, conv_d_b=conv_d_b, ln_d_g=ln_d_g, ln_d_b=ln_d_b,
                   w_out=w_out.reshape(DEPTH, 4, W_GROUP, D_MODEL).astype(BF16), ln_g=ln_g, ln_b=ln_b)

    rows = batch + dec_batch
    rows_pad = -(-rows // 8) * 8
    c_all = jnp.concatenate([c_prompt, c_sample, jnp.zeros((rows_pad - rows, D_MODEL), F32)], axis=0)
    mod_all = modulation(c_all, w_mod, b_mod)
    mods_p = mod_all[:, :batch].reshape(DEPTH, batch, 1, 3 * D_MODEL)
    mods_s = mod_all[:, batch:rows].reshape(DEPTH, dec_batch, 1, 3 * D_MODEL)

    def zeros(*shape):
        return jnp.zeros((DEPTH, batch) + shape, F32)

    y_prompt, sp = _run_trunk(x_prompt, mods_p, None, None,
                              zeros(K_A - 1, CONV_DIM_A), zeros(H_A, P_A, N_A), zeros(K_C - 1, W_GROUP),
                              zeros(H_C, DH_C, DH_C), zeros(H_C, DH_C), zeros(H_C), zeros(K_D - 1, W_GROUP),
                              weights)
    y_sample, ss = _run_trunk(x_sample, mods_s, cache_k, cache_v, state_conv_a, state_ssm, state_conv_c,
                              state_mlstm_c, state_mlstm_n, state_mlstm_m, state_conv_d, weights)
    return (y_prompt, y_sample, *sp, *ss)
```

```python
import functools
import math

import jax
import jax.numpy as jnp
from jax import lax
from jax.experimental import pallas as pl
from jax.experimental.pallas import tpu as pltpu

D_MODEL = 2048
DEPTH = 4
W_GROUP = 512
H_A, P_A, N_A, G_A, K_A = 8, 64, 128, 2, 4
CONV_DIM_A = W_GROUP + 2 * G_A * N_A
H_B, D_B = 8, 64
H_C, DH_C, K_C = 4, 128, 4
K_D = 31
ALPHA = (2 * DEPTH) ** 0.25
EPS = 1e-5
N_MAIN_GROUPS = 12
G_ZA, G_XA, G_BCA, G_Q, G_GB, G_XC, G_ZC, G_AD, G_BD, G_GD, G_K, G_V = range(N_MAIN_GROUPS)
LANES = 128
SMALL_W = LANES
LANE_DT, LANE_I, LANE_F = 0, 8, 12

F32 = jnp.float32
BF16 = jnp.bfloat16
V7X_VMEM_BYTES = 64 * 1024 * 1024
VMEM_LIMIT = V7X_VMEM_BYTES * 7 // 8


def _cparams(sem):
    return pltpu.CompilerParams(dimension_semantics=sem, vmem_limit_bytes=VMEM_LIMIT)


def _dot(a, b):
    return jnp.dot(a, b, preferred_element_type=F32)


def _dot_nt(a, b):
    return lax.dot_general(a, b, (((1,), (1,)), ((), ())), preferred_element_type=F32)


def _dot_tn(a, b):
    return lax.dot_general(a, b, (((0,), (0,)), ((), ())), preferred_element_type=F32)


def _split3(a):
    hi = a.astype(BF16)
    r = a - hi.astype(F32)
    mid = r.astype(BF16)
    lo = (r - mid.astype(F32)).astype(BF16)
    return jnp.concatenate([hi, mid, lo], axis=1)


def _sum3(p):
    n = p.shape[1] // 3
    return (p[:, 2 * n:] + p[:, n:2 * n]) + p[:, :n]


def _select_left(sel01, a):
    return _sum3(_dot(sel01.astype(BF16), _split3(a)))


def _select_right(a, sel01):
    s = sel01.astype(BF16)
    return _dot(_split3(a), jnp.concatenate([s, s, s], axis=0))


def _select_nt(sel01, a):
    s = sel01.astype(BF16)
    return _dot_nt(jnp.concatenate([s, s, s], axis=1), _split3(a))


def _sigmoid(x):
    return 1.0 / (1.0 + jnp.exp(-x))


def _silu(x):
    return x * _sigmoid(x)


def _softplus(x):
    return jnp.maximum(x, 0.0) + jnp.log(1.0 + jnp.exp(-jnp.abs(x)))


def _iota(shape, dim):
    return lax.broadcasted_iota(jnp.int32, shape, dim)


def _lower_tri(n, strict=False):
    r, c = _iota((n, n), 0), _iota((n, n), 1)
    return (c < r) if strict else (c <= r)


MOD_TN = 1024


def _mod_kernel(c_ref, w_ref, b_ref, o_ref):
    o_ref[...] = _dot(c_ref[...].astype(BF16), w_ref[...].astype(BF16)) + b_ref[...]


def modulation(c_all, w_mod, b_mod):
    rows = c_all.shape[0]
    n = w_mod.shape[-1]
    return pl.pallas_call(
        _mod_kernel,
        grid=(DEPTH, n // MOD_TN),
        in_specs=[pl.BlockSpec((rows, D_MODEL), lambda l, j: (0, 0)),
                  pl.BlockSpec((None, D_MODEL, MOD_TN), lambda l, j: (l, 0, j)),
                  pl.BlockSpec((None, 1, MOD_TN), lambda l, j: (l, 0, j))],
        out_specs=pl.BlockSpec((None, rows, MOD_TN), lambda l, j: (l, 0, j)),
        out_shape=jax.ShapeDtypeStruct((DEPTH, rows, n), F32),
        compiler_params=_cparams(("arbitrary", "arbitrary")),
        name="modulation",
    )(c_all, w_mod, b_mod.reshape(DEPTH, 1, n))


MIXER_BATCH_ROWS = 8


def _batch_rows(bsz, nchunks):
    if nchunks > 1 or bsz % MIXER_BATCH_ROWS:
        return 1
    return MIXER_BATCH_ROWS


def _over_batch(inner, bb, batched):
    def kern(*refs):
        def one(bi):
            inner(*[r.at[bi] if flag else r for r, flag in zip(refs, batched)])
        if bb == 1:
            one(0)
        else:
            def body(bi, carry):
                one(bi)
                return carry
            lax.fori_loop(0, bb, body, 0)
    return kern


def _row_tiling(bsz, seq, target):
    if seq >= target:
        return 1, target
    return min(bsz, target // seq), seq


PROJ_STEP_W = 2 * W_GROUP
PROJ_STEPS = N_MAIN_GROUPS * W_GROUP // PROJ_STEP_W
assert (G_K, G_V) == (N_MAIN_GROUPS - 2, N_MAIN_GROUPS - 1) and PROJ_STEP_W == 2 * W_GROUP


def _inproj_kernel(*refs, bb, lt, cl, transpose_kv, look_ahead):
    if transpose_kv:
        x_ref, shift_ref, scale_ref, w_ref, ws_ref, _, _, proj_ref, small_ref, kt_ref, vt_ref, u_ref = refs
    else:
        x_ref, shift_ref, scale_ref, w_ref, ws_ref, proj_ref, small_ref, u_ref = refs
    j = pl.program_id(2)
    n_l = lt // cl
    last = PROJ_STEPS - 1

    def normalise(slot):
        for it in range(bb * n_l):
            bi, r0 = it // n_l, (it % n_l) * cl
            x = x_ref[bi, r0:r0 + cl, :]
            mu = jnp.mean(x, axis=-1, keepdims=True)
            xc = x - mu
            var = jnp.mean(xc * xc, axis=-1, keepdims=True)
            u = xc * lax.rsqrt(var + EPS) * (1.0 + scale_ref[bi]) + shift_ref[bi]
            u_ref[slot, it * cl:(it + 1) * cl, :] = u.astype(BF16)

    if look_ahead:
        tile = pl.program_id(0) * pl.num_programs(1) + pl.program_id(1)
        cur = tile % 2
        pl.when((tile == 0) & (j == 0))(functools.partial(normalise, 0))
    else:
        cur = 0
        pl.when(j == 0)(functools.partial(normalise, 0))

    @pl.when(j == 0)
    def _():
        small_ref[...] = _dot_nt(u_ref[cur], ws_ref[...]).reshape(bb, lt, SMALL_W)

    if transpose_kv:
        @pl.when(j < last)
        def _():
            proj_ref[...] = _dot_nt(u_ref[cur], w_ref[...]).reshape(bb, lt, PROJ_STEP_W)

        def last_step(slot):
            kv = _dot_nt(w_ref[...], u_ref[slot])
            kt_ref[...] = kv[:W_GROUP].reshape(H_B, D_B, lt)
            vt_ref[...] = kv[W_GROUP:].reshape(H_B, D_B, lt)
            if look_ahead:
                normalise(1 - slot)

        if look_ahead:
            for slot in (0, 1):
                pl.when((j == last) & (cur == slot))(functools.partial(last_step, slot))
        else:
            pl.when(j == last)(functools.partial(last_step, 0))
    else:
        proj_ref[...] = _dot_nt(u_ref[cur], w_ref[...]).reshape(bb, lt, PROJ_STEP_W)


def in_projection(x, mod, wt_main, wt_small, layer, kv_t=None):
    bsz, seq, _ = x.shape
    bb, lt = _row_tiling(bsz, seq, 1024)
    cl = min(lt, 128)
    tm = bb * lt
    transpose_kv = kv_t is not None
    n_l = seq // lt
    ntiles = (bsz // bb) * n_l
    look_ahead = transpose_kv and ntiles > 1
    kern = functools.partial(_inproj_kernel, bb=bb, lt=lt, cl=cl, transpose_kv=transpose_kv,
                             look_ahead=look_ahead)

    def x_tile(b, l, j):
        if not look_ahead:
            return b, l
        tile = jnp.minimum(b * n_l + l + (j == PROJ_STEPS - 1).astype(jnp.int32), ntiles - 1)
        return tile // n_l, tile % n_l

    in_specs = [pl.BlockSpec((bb, lt, D_MODEL), lambda b, l, j: (*x_tile(b, l, j), 0)),
                pl.BlockSpec((bb, 1, D_MODEL), lambda b, l, j: (x_tile(b, l, j)[0], 0, 0)),
                pl.BlockSpec((bb, 1, D_MODEL), lambda b, l, j: (x_tile(b, l, j)[0], 0, 1)),
                pl.BlockSpec((None, PROJ_STEP_W, D_MODEL), lambda b, l, j: (layer, j, 0)),
                pl.BlockSpec((None, SMALL_W, D_MODEL), lambda b, l, j: (layer, 0, 0))]
    small_spec = pl.BlockSpec((bb, lt, SMALL_W), lambda b, l, j: (b, l, 0))
    small_shape = jax.ShapeDtypeStruct((bsz, seq, SMALL_W), F32)
    scratch = [pltpu.VMEM((2 if look_ahead else 1, tm, D_MODEL), BF16)]
    sem = ("arbitrary", "arbitrary", "arbitrary")
    grid = (bsz // bb, seq // lt, PROJ_STEPS)
    if not transpose_kv:
        return pl.pallas_call(
            kern, grid=grid, in_specs=in_specs,
            out_specs=[pl.BlockSpec((bb, lt, PROJ_STEP_W), lambda b, l, j: (b, l, j)), small_spec],
            out_shape=[jax.ShapeDtypeStruct((bsz, seq, N_MAIN_GROUPS * W_GROUP), F32), small_shape],
            scratch_shapes=scratch, compiler_params=_cparams(sem), name="in_projection",
        )(x, mod, mod, wt_main, wt_small)
    assert bb == 1
    any_spec = pl.BlockSpec(memory_space=pl.ANY)
    t_spec = pl.BlockSpec((None, None, H_B, D_B, lt), lambda b, l, j: (layer, b, 0, 0, l))
    t_shape = jax.ShapeDtypeStruct(kv_t[0].shape, F32)
    return pl.pallas_call(
        kern, grid=grid, in_specs=in_specs + [any_spec, any_spec],
        out_specs=[pl.BlockSpec((bb, lt, PROJ_STEP_W), lambda b, l, j: (b, l, jnp.minimum(j, PROJ_STEPS - 2))),
                   small_spec, t_spec, t_spec],
        out_shape=[jax.ShapeDtypeStruct((bsz, seq, G_K * W_GROUP), F32), small_shape, t_shape, t_shape],
        input_output_aliases={5: 2, 6: 3},
        scratch_shapes=scratch, compiler_params=_cparams(sem), name="in_projection_kvt",
    )(x, mod, mod, wt_main, wt_small, kv_t[0], kv_t[1])


def _outproj_kernel(ya_ref, yb_ref, yc_ref, yd_ref, x_ref, gate_ref, w_ref, g_ref, b_ref, o_ref, acc_ref,
                    *, bb, lt, cl):
    tm = bb * lt
    n_l = lt // cl
    n_chunks = bb * n_l
    halves = 2 if n_chunks % 2 == 0 else 1
    ys = [r[...].reshape(tm, W_GROUP) for r in (ya_ref, yb_ref, yc_ref, yd_ref)]
    hr = tm // halves
    for half in range(halves):
        rows = slice(half * hr, (half + 1) * hr)
        acc = _dot(ys[0][rows], w_ref[0])
        for g in range(1, 4):
            acc += _dot(ys[g][rows], w_ref[g])
        acc_ref[rows, :] = acc

    for it in range(n_chunks):
        bi, r0 = it // n_l, (it % n_l) * cl
        o = acc_ref[it * cl:(it + 1) * cl, :]
        v = ALPHA * x_ref[bi, r0:r0 + cl, :] + (1.0 + gate_ref[bi]) * o
        mu = jnp.mean(v, axis=-1, keepdims=True)
        vc = v - mu
        var = jnp.mean(vc * vc, axis=-1, keepdims=True)
        o_ref[bi, r0:r0 + cl, :] = vc * lax.rsqrt(var + EPS) * g_ref[...] + b_ref[...]


def out_projection(ys, x, mod, w_out, ln_g, ln_b, layer):
    bsz, seq, _ = x.shape
    bb, lt = _row_tiling(bsz, seq, 512)
    cl = min(lt, 128)
    kern = functools.partial(_outproj_kernel, bb=bb, lt=lt, cl=cl)
    yspec = pl.BlockSpec((bb, lt, W_GROUP), lambda b, l: (b, l, 0))
    return pl.pallas_call(
        kern,
        grid=(bsz // bb, seq // lt),
        in_specs=[yspec, yspec, yspec, yspec,
                  pl.BlockSpec((bb, lt, D_MODEL), lambda b, l: (b, l, 0)),
                  pl.BlockSpec((bb, 1, D_MODEL), lambda b, l: (b, 0, 2)),
                  pl.BlockSpec((None, 4, W_GROUP, D_MODEL), lambda b, l: (layer, 0, 0, 0)),
                  pl.BlockSpec((1, D_MODEL), lambda b, l: (0, 0)),
                  pl.BlockSpec((1, D_MODEL), lambda b, l: (0, 0))],
        out_specs=pl.BlockSpec((bb, lt, D_MODEL), lambda b, l: (b, l, 0)),
        out_shape=jax.ShapeDtypeStruct((bsz, seq, D_MODEL), F32),
        scratch_shapes=[pltpu.VMEM((bb * lt, D_MODEL), F32)],
        compiler_params=_cparams(("arbitrary", "arbitrary")),
        name="out_projection",
    )(*ys, x, mod, w_out, ln_g.reshape(1, D_MODEL), ln_b.reshape(1, D_MODEL))


CONV_PAD = 8


def _ssd_kernel(z_ref, x_ref, bc_ref, small_ref, cbuf_ref, ssm0_ref, cw_ref, cb_ref, dtb_ref, alog_ref,
                dskip_ref, nw_ref, y_ref, cnew_ref, ssmnew_ref, ext_ref, state_ref, wcat_ref, xbd_ref,
                ccat_ref, bcat_ref, *, q, nchunks):
    c = pl.program_id(1)
    tail = K_A - 1

    @pl.when(c == 0)
    def _():
        ext_ref[CONV_PAD - tail:CONV_PAD, :] = cbuf_ref[...]
        state_ref[...] = jnp.zeros_like(state_ref)
        for h in range(H_A):
            state_ref[h * P_A:(h + 1) * P_A, h * N_A:(h + 1) * N_A] = ssm0_ref[h]

    ext_ref[CONV_PAD:CONV_PAD + q, 0:W_GROUP] = x_ref[...]
    ext_ref[CONV_PAD:CONV_PAD + q, W_GROUP:] = bc_ref[...]
    conv = cb_ref[...]
    for k in range(K_A):
        off = CONV_PAD - tail + k
        conv = conv + cw_ref[k:k + 1, :] * ext_ref[off:off + q, :]
    new_tail = ext_ref[CONV_PAD + q - tail:CONV_PAD + q, :]
    ext_ref[CONV_PAD - tail:CONV_PAD, :] = new_tail

    @pl.when(c == nchunks - 1)
    def _():
        cnew_ref[...] = new_tail

    xbc = _silu(conv)
    xs = xbc[:, :W_GROUP]
    bm = [xbc[:, W_GROUP + g * N_A:W_GROUP + (g + 1) * N_A] for g in range(G_A)]
    cm = [xbc[:, W_GROUP + (G_A + g) * N_A:W_GROUP + (G_A + g + 1) * N_A] for g in range(G_A)]

    lane = _iota((1, SMALL_W), 1)
    head_lanes = lane < H_A
    dt = jnp.where(head_lanes, _softplus(small_ref[...] + dtb_ref[...]), 0.0)
    a = -jnp.exp(alog_ref[...])
    da = dt * a
    acum = _select_left(_lower_tri(q), da)
    eye8 = _iota((8, SMALL_W), 0) == _iota((8, SMALL_W), 1)
    acum_row = _select_nt(eye8, acum)
    last = acum[q - 1:q, :]
    e_acum = jnp.exp(acum)
    w_s = jnp.exp(last - acum)
    e_last = jnp.exp(last)

    expand = _iota((SMALL_W, W_GROUP), 1) // P_A == _iota((SMALL_W, W_GROUP), 0)
    dt_wide = _select_right(dt, expand)
    xdt = xs * dt_wide
    xdt_bf = xdt.astype(BF16)
    col_head = _iota((1, W_GROUP), 1) // P_A
    causal = _lower_tri(q)

    gmat = [_dot_nt(cm[g].astype(BF16), bm[g].astype(BF16)) for g in range(G_A)]
    for h in range(H_A):
        g = h // (H_A // G_A)
        seg = acum[:, h:h + 1] - acum_row[h:h + 1, :]
        decay = jnp.exp(jnp.where(causal, seg, -jnp.inf))
        wcat_ref[:, h * q:(h + 1) * q] = (gmat[g] * decay).astype(BF16)
        xbd_ref[h * q:(h + 1) * q, :] = jnp.where(col_head == h, xdt, 0.0).astype(BF16)
        ccat_ref[:, h * N_A:(h + 1) * N_A] = (cm[g] * e_acum[:, h:h + 1]).astype(BF16)
        bcat_ref[:, h * N_A:(h + 1) * N_A] = (bm[g] * w_s[:, h:h + 1]).astype(BF16)

    y = _dot(wcat_ref[...], xbd_ref[...])
    y = y + _dot_nt(ccat_ref[...], state_ref[...].astype(BF16))
    y = y + dskip_ref[...] * xs

    upd = _dot_tn(xdt_bf, bcat_ref[...])
    for h in range(H_A):
        rs, cs = slice(h * P_A, (h + 1) * P_A), slice(h * N_A, (h + 1) * N_A)
        state_ref[rs, cs] = e_last[:, h:h + 1] * state_ref[rs, cs] + upd[rs, cs]

    @pl.when(c == nchunks - 1)
    def _():
        for h in range(H_A):
            ssmnew_ref[h] = state_ref[h * P_A:(h + 1) * P_A, h * N_A:(h + 1) * N_A]

    yz = y * _silu(z_ref[...])
    ms = jnp.mean(yz * yz, axis=-1, keepdims=True)
    y_ref[...] = (yz * lax.rsqrt(ms + EPS) * nw_ref[...]).astype(BF16)


def mixer_ssd(proj, small, cbuf, ssm0_all, layer, conv_w, conv_b, dt_bias, a_log, d_skip, norm_w):
    bsz, seq, _ = proj.shape
    q = min(seq, 256)
    nchunks = seq // q
    pad = lambda v: jnp.zeros((1, SMALL_W), F32).at[0, :v.shape[0]].set(v)
    bb = _batch_rows(bsz, nchunks)
    kern = _over_batch(functools.partial(_ssd_kernel, q=q, nchunks=nchunks), bb,
                       (True,) * 6 + (False,) * 6 + (True,) * 3 + (False,) * 6)
    col = lambda j: pl.BlockSpec((bb, q, W_GROUP), lambda b, c: (b, c, j))
    full = lambda shape: pl.BlockSpec(shape, lambda b, c: (0,) * len(shape))
    return pl.pallas_call(
        kern,
        grid=(bsz // bb, nchunks),
        in_specs=[col(G_ZA), col(G_XA), col(G_BCA),
                  pl.BlockSpec((bb, q, SMALL_W), lambda b, c: (b, c, 0)),
                  pl.BlockSpec((bb, K_A - 1, CONV_DIM_A), lambda b, c: (b, 0, 0)),
                  pl.BlockSpec((None, bb, H_A, P_A, N_A), lambda b, c: (layer, b, 0, 0, 0)),
                  full((K_A, CONV_DIM_A)), full((1, CONV_DIM_A)), full((1, SMALL_W)), full((1, SMALL_W)),
                  full((1, W_GROUP)), full((1, W_GROUP))],
        out_specs=[pl.BlockSpec((bb, q, W_GROUP), lambda b, c: (b, c, 0)),
                   pl.BlockSpec((bb, K_A - 1, CONV_DIM_A), lambda b, c: (b, 0, 0)),
                   pl.BlockSpec((bb, H_A, P_A, N_A), lambda b, c: (b, 0, 0, 0))],
        out_shape=[jax.ShapeDtypeStruct((bsz, seq, W_GROUP), BF16),
                   jax.ShapeDtypeStruct((bsz, K_A - 1, CONV_DIM_A), F32),
                   jax.ShapeDtypeStruct((bsz, H_A, P_A, N_A), F32)],
        scratch_shapes=[pltpu.VMEM((CONV_PAD + q, CONV_DIM_A), F32),
                        pltpu.VMEM((H_A * P_A, H_A * N_A), F32),
                        pltpu.VMEM((q, H_A * q), BF16),
                        pltpu.VMEM((H_A * q, W_GROUP), BF16),
                        pltpu.VMEM((q, H_A * N_A), BF16),
                        pltpu.VMEM((q, H_A * N_A), BF16)],
        compiler_params=_cparams(("arbitrary", "arbitrary")),
        name="mixer_ssd",
    )(proj, proj, proj, small, cbuf, ssm0_all, conv_w, conv_b.reshape(1, CONV_DIM_A), pad(dt_bias), pad(a_log),
      jnp.repeat(d_skip, P_A).reshape(1, W_GROUP), norm_w.reshape(1, W_GROUP))


def _mlstm_kernel(xc_ref, zc_ref, small_ref, cbuf_ref, c0_ref, n0_ref, m0_ref, cw_ref, cb_ref, wq_ref, wk_ref,
                  wv_ref, igb_ref, fgb_ref, nw_ref, skip_ref, y_ref, cnew_ref, cst_ref, nst_ref, mst_ref,
                  ext_ref, cs_ref, ns_ref, ms_ref, *, q, nchunks, phase="all"):
    c = pl.program_id(1)
    tail = K_C - 1

    if phase != "main":
        @pl.when(c == 0)
        def _():
            ext_ref[CONV_PAD - tail:CONV_PAD, :] = cbuf_ref[...]
            cs_ref[...] = c0_ref[...]
            ns_ref[...] = n0_ref[...]
            ms_ref[...] = m0_ref[...]
    if phase == "init":
        return

    x_in = xc_ref[...]
    ext_ref[CONV_PAD:CONV_PAD + q, :] = x_in
    conv = cb_ref[...]
    for k in range(K_C):
        off = CONV_PAD - tail + k
        conv = conv + cw_ref[k:k + 1, :] * ext_ref[off:off + q, :]
    new_tail = ext_ref[CONV_PAD + q - tail:CONV_PAD + q, :]
    ext_ref[CONV_PAD - tail:CONV_PAD, :] = new_tail
    cnew_ref[...] = new_tail

    xconv = _silu(conv)
    sm = small_ref[...]
    ipre = sm + igb_ref[...]
    fpre = sm + fgb_ref[...]
    logf = jnp.minimum(fpre, 0.0) - jnp.log(1.0 + jnp.exp(-jnp.abs(fpre)))
    bcum = _select_left(_lower_tri(q), logf)
    sel_i = _iota((8, SMALL_W), 1) == _iota((8, SMALL_W), 0) + LANE_I
    sel_f = _iota((8, SMALL_W), 1) == _iota((8, SMALL_W), 0) + LANE_F
    r_row = _select_nt(sel_i, ipre) - _select_nt(sel_f, bcum)
    causal = _lower_tri(q)
    scale_k = DH_C ** -0.5
    heads = range(H_C)
    hs = [slice(h * DH_C, (h + 1) * DH_C) for h in heads]

    xh = [xconv[:, hs[h]].astype(BF16) for h in heads]
    qh = [_dot(xh[h], wq_ref[h]) for h in heads]
    kh = [_dot(xh[h], wk_ref[h]) * scale_k for h in heads]
    vh = [_dot(x_in[:, hs[h]].astype(BF16), wv_ref[h]) for h in heads]
    qb = [t.astype(BF16) for t in qh]
    kb = [t.astype(BF16) for t in kh]
    vb = [t.astype(BF16) for t in vh]
    qk = [_dot_nt(qb[h], kb[h]) for h in heads]
    c_old = [cs_ref[h] for h in heads]
    n_old = [ns_ref[h:h + 1, :] for h in heads]
    qc = [_dot_nt(qb[h], c_old[h].astype(BF16)) for h in heads]

    b_col = [bcum[:, LANE_F + h:LANE_F + h + 1] for h in heads]
    m_prev = [ms_ref[:, h:h + 1] for h in heads]
    d = [jnp.where(causal, b_col[h] + r_row[h:h + 1, :], -jnp.inf) for h in heads]
    inter = [b_col[h] + m_prev[h] for h in heads]
    m_t = [jnp.maximum(inter[h], jnp.max(d[h], axis=-1, keepdims=True)) for h in heads]
    w = [jnp.exp(d[h] - m_t[h]) * qk[h] for h in heads]
    gq = [jnp.exp(inter[h] - m_t[h]) for h in heads]
    m_new = [m_t[h][q - 1:q, :] for h in heads]
    b_last = [b_col[h][q - 1:q, :] for h in heads]
    ws = [jnp.exp(b_last[h] - b_col[h] + ipre[:, LANE_I + h:LANE_I + h + 1] - m_new[h]) for h in heads]
    g_last = [jnp.exp(b_last[h] + m_prev[h] - m_new[h]) for h in heads]
    wv = [_dot(w[h].astype(BF16), vb[h]) for h in heads]
    c_upd = [_dot_tn((vh[h] * ws[h]).astype(BF16), kb[h]) for h in heads]

    for h in heads:
        num = wv[h] + gq[h] * qc[h]
        nq = jnp.sum(w[h], axis=-1, keepdims=True) + gq[h] * jnp.sum(qh[h] * n_old[h], axis=-1, keepdims=True)
        hid = num / jnp.maximum(jnp.abs(nq), jnp.exp(-m_t[h]))
        cs_ref[h] = g_last[h] * c_old[h] + c_upd[h]
        ns_ref[h:h + 1, :] = g_last[h] * n_old[h] + jnp.sum(ws[h] * kh[h], axis=0, keepdims=True)
        ms_ref[:, h:h + 1] = m_new[h]

        mu = jnp.mean(hid, axis=-1, keepdims=True)
        hc = hid - mu
        var = jnp.mean(hc * hc, axis=-1, keepdims=True)
        hn = hc * lax.rsqrt(var + EPS) * nw_ref[:, hs[h]]
        yh = (hn + skip_ref[:, hs[h]] * xconv[:, hs[h]]) * _silu(zc_ref[:, hs[h]])
        y_ref[:, hs[h]] = yh.astype(BF16)

    @pl.when(c == nchunks - 1)
    def _():
        cst_ref[...] = cs_ref[...]
        nst_ref[...] = ns_ref[...]
        mst_ref[...] = ms_ref[...]


def _run_mixers(parts, bsz, bb, nchunks, name):
    n_in = [len(p["in_specs"]) for p in parts]
    n_out = [len(p["out_specs"]) for p in parts]
    n_scr = [len(p["scratch_shapes"]) for p in parts]

    def inner(*refs):
        ins, outs, scrs = refs[:sum(n_in)], refs[sum(n_in):sum(n_in) + sum(n_out)], refs[sum(n_in) + sum(n_out):]
        groups, a, b, c = [], 0, 0, 0
        for ni, no, ns in zip(n_in, n_out, n_scr):
            groups.append(ins[a:a + ni] + outs[b:b + no] + scrs[c:c + ns])
            a, b, c = a + ni, b + no, c + ns
        if len(parts) == 1:
            parts[0]["kernel"](*groups[0])
            return
        for p, g in zip(parts, groups):
            p["kernel"](*g, phase="init")
        for p, g in zip(parts, groups):
            p["kernel"](*g, phase="main")

    batched = tuple(flag for p, ni in zip(parts, n_in) for flag in (True,) * p["n_batched_in"]
                    + (False,) * (ni - p["n_batched_in"]))
    batched += (True,) * sum(n_out) + (False,) * sum(n_scr)
    outs = pl.pallas_call(
        _over_batch(inner, bb, batched),
        grid=(bsz // bb, nchunks),
        in_specs=[s for p in parts for s in p["in_specs"]],
        out_specs=[s for p in parts for s in p["out_specs"]],
        out_shape=[s for p in parts for s in p["out_shape"]],
        scratch_shapes=[s for p in parts for s in p["scratch_shapes"]],
        compiler_params=_cparams(("arbitrary", "arbitrary")),
        name=name,
    )(*[a for p in parts for a in p["args"]])
    split, start = [], 0
    for no in n_out:
        split.append(outs[start:start + no])
        start += no
    return split


def _mlstm_parts(proj, small, cbuf, c0_all, layer, n0, m0, conv_w, conv_b, wq, wk, wv, ig_bias, fg_bias, norm_w,
                 skip, bb):
    bsz, seq, _ = proj.shape
    q = min(seq, 256)
    nchunks = seq // q
    pad_at = lambda v, lane: jnp.zeros((1, SMALL_W), F32).at[0, lane:lane + v.shape[0]].set(v)
    col = lambda j: pl.BlockSpec((bb, q, W_GROUP), lambda b, c: (b, c, j))
    full = lambda shape: pl.BlockSpec(shape, lambda b, c: (0,) * len(shape))
    return dict(
        kernel=functools.partial(_mlstm_kernel, q=q, nchunks=nchunks), n_batched_in=7,
        in_specs=[col(G_XC), col(G_ZC),
                  pl.BlockSpec((bb, q, SMALL_W), lambda b, c: (b, c, 0)),
                  pl.BlockSpec((bb, K_C - 1, W_GROUP), lambda b, c: (b, 0, 0)),
                  pl.BlockSpec((None, bb, H_C, DH_C, DH_C), lambda b, c: (layer, b, 0, 0, 0)),
                  pl.BlockSpec((bb, H_C, DH_C), lambda b, c: (b, 0, 0)),
                  pl.BlockSpec((bb, 1, H_C), lambda b, c: (b, 0, 0)),
                  full((K_C, W_GROUP)), full((1, W_GROUP)),
                  full((H_C, DH_C, DH_C)), full((H_C, DH_C, DH_C)), full((H_C, DH_C, DH_C)),
                  full((1, SMALL_W)), full((1, SMALL_W)), full((1, W_GROUP)), full((1, W_GROUP))],
        out_specs=[pl.BlockSpec((bb, q, W_GROUP), lambda b, c: (b, c, 0)),
                   pl.BlockSpec((bb, K_C - 1, W_GROUP), lambda b, c: (b, 0, 0)),
                   pl.BlockSpec((bb, H_C, DH_C, DH_C), lambda b, c: (b, 0, 0, 0)),
                   pl.BlockSpec((bb, H_C, DH_C), lambda b, c: (b, 0, 0)),
                   pl.BlockSpec((bb, 1, H_C), lambda b, c: (b, 0, 0))],
        out_shape=[jax.ShapeDtypeStruct((bsz, seq, W_GROUP), BF16),
                   jax.ShapeDtypeStruct((bsz, K_C - 1, W_GROUP), F32),
                   jax.ShapeDtypeStruct((bsz, H_C, DH_C, DH_C), F32),
                   jax.ShapeDtypeStruct((bsz, H_C, DH_C), F32),
                   jax.ShapeDtypeStruct((bsz, 1, H_C), F32)],
        scratch_shapes=[pltpu.VMEM((CONV_PAD + q, W_GROUP), F32),
                        pltpu.VMEM((H_C, DH_C, DH_C), F32),
                        pltpu.VMEM((H_C, DH_C), F32),
                        pltpu.VMEM((1, H_C), F32)],
        args=(proj, proj, small, cbuf, c0_all, n0, m0.reshape(bsz, 1, H_C), conv_w, conv_b.reshape(1, W_GROUP),
              wq.astype(BF16), wk.astype(BF16), wv.astype(BF16), pad_at(ig_bias, LANE_I),
              pad_at(fg_bias, LANE_F), norm_w.reshape(1, W_GROUP), skip.reshape(1, W_GROUP)))


CONV_D_PAD = 32


def _conf_kernel(a_ref, b_ref, g_ref, cbuf_ref, cw_ref, cb_ref, lg_ref, lb_ref, y_ref, cnew_ref, ext_ref,
                 win_ref, *, q, nchunks, phase="all"):
    c = pl.program_id(1)
    tail = K_D - 1

    if phase != "main":
        @pl.when(c == 0)
        def _():
            ext_ref[CONV_D_PAD - tail:CONV_D_PAD, :] = cbuf_ref[...]
    if phase == "init":
        return

    ext_ref[CONV_D_PAD:CONV_D_PAD + q, :] = a_ref[...] * _sigmoid(b_ref[...])
    conv = cb_ref[...]
    offsets = [CONV_D_PAD - tail + k for k in range(K_D)]
    for r in range(8):
        taps = [k for k in range(K_D) if offsets[k] % 8 == r]
        if not taps:
            continue
        span = max(offsets[k] for k in taps) - r
        if r == 0:
            window_ref = ext_ref
        else:
            window_ref = win_ref.at[r - 1]
            window_ref[0:span + q, :] = ext_ref[r:r + span + q, :]
        for k in taps:
            a = offsets[k] - r
            conv = conv + cw_ref[k:k + 1, :] * window_ref[a:a + q, :]
    new_tail = ext_ref[CONV_D_PAD + q - tail:CONV_D_PAD + q, :]
    ext_ref[CONV_D_PAD - tail:CONV_D_PAD, :] = new_tail
    cnew_ref[...] = new_tail

    mu = jnp.mean(conv, axis=-1, keepdims=True)
    cc = conv - mu
    var = jnp.mean(cc * cc, axis=-1, keepdims=True)
    v = cc * lax.rsqrt(var + EPS) * lg_ref[...] + lb_ref[...]
    y_ref[...] = (_silu(v) * _silu(g_ref[...])).astype(BF16)


def _conf_parts(proj, cbuf, conv_w, conv_b, ln_g, ln_b, bb):
    bsz, seq, _ = proj.shape
    q = min(seq, 256)
    nchunks = seq // q
    col = lambda j: pl.BlockSpec((bb, q, W_GROUP), lambda b, c: (b, c, j))
    full = lambda shape: pl.BlockSpec(shape, lambda b, c: (0,) * len(shape))
    return dict(
        kernel=functools.partial(_conf_kernel, q=q, nchunks=nchunks), n_batched_in=4,
        in_specs=[col(G_AD), col(G_BD), col(G_GD),
                  pl.BlockSpec((bb, K_D - 1, W_GROUP), lambda b, c: (b, 0, 0)),
                  full((K_D, W_GROUP)), full((1, W_GROUP)), full((1, W_GROUP)), full((1, W_GROUP))],
        out_specs=[pl.BlockSpec((bb, q, W_GROUP), lambda b, c: (b, c, 0)),
                   pl.BlockSpec((bb, K_D - 1, W_GROUP), lambda b, c: (b, 0, 0))],
        out_shape=[jax.ShapeDtypeStruct((bsz, seq, W_GROUP), BF16),
                   jax.ShapeDtypeStruct((bsz, K_D - 1, W_GROUP), F32)],
        scratch_shapes=[pltpu.VMEM((CONV_D_PAD + q, W_GROUP), F32),
                        pltpu.VMEM((7, CONV_D_PAD + q, W_GROUP), F32)],
        args=(proj, proj, proj, cbuf, conv_w, conv_b.reshape(1, W_GROUP), ln_g.reshape(1, W_GROUP),
              ln_b.reshape(1, W_GROUP)))


def mixers_mlstm_conformer(proj, small, mlstm_args, conf_args):
    bsz, seq, _ = proj.shape
    nchunks = seq // min(seq, 256)
    bb = _batch_rows(bsz, nchunks)
    conf = _conf_parts(proj, *conf_args, bb)
    mlstm = _mlstm_parts(proj, small, *mlstm_args, bb)
    if nchunks > 1:
        (y_d, conv_d_new), (y_c, conv_c_new, cst, nst, mst) = _run_mixers([conf, mlstm], bsz, bb, nchunks,
                                                                          "mixers_mlstm_conformer")
    else:
        ((y_c, conv_c_new, cst, nst, mst),) = _run_mixers([mlstm], bsz, bb, nchunks, "mixer_mlstm")
        ((y_d, conv_d_new),) = _run_mixers([conf], bsz, bb, nchunks, "mixer_conformer")
    return (y_c, conv_c_new, cst, nst, mst.reshape(bsz, H_C)), (y_d, conv_d_new)


ATT_BLOCK = 256


LOG2E = math.log2(math.e)
Q_SCALE = D_B ** -0.5 * LOG2E


def _stick_blocks(z2s, mask, carries, suffix_mat):
    masks = mask if isinstance(mask, (list, tuple)) else [mask] * len(z2s)
    keep = lambda m, v: v if m is None else jnp.where(m, v, 0.0)
    sps = [jnp.maximum(z2, 0.0) + jnp.log2(1.0 + jnp.exp2(-jnp.abs(z2))) for z2 in z2s]
    sps = [keep(m, sp) for m, sp in zip(masks, sps)]
    suffixes = [_dot(sp.astype(BF16), suffix_mat) for sp in sps]
    ws = [jnp.exp2(z2 - sp - suffix - carry) for z2, sp, suffix, carry in zip(z2s, sps, suffixes, carries)]
    ws = [keep(m, w) for m, w in zip(masks, ws)]
    totals = [suffix[:, 0:1] + sp[:, 0:1] for suffix, sp in zip(suffixes, sps)]
    return ws, [carry + total for carry, total in zip(carries, totals)]


def _suffix_matrix(n):
    return (jnp.arange(n)[:, None] > jnp.arange(n)[None, :]).astype(BF16)


def _attn_prompt_kernel(q_ref, kt_ref, vt_ref, g_ref, sm_ref, y_ref, kb_ref, vb_ref, z_ref, w_ref, acc_ref,
                        carry_ref, *, t, nblk):
    qi = pl.program_id(2)
    heads = range(kb_ref.shape[0])

    @pl.when(qi == 0)
    def _():
        for hh in heads:
            for blk in range(nblk):
                kb_ref[hh, blk] = kt_ref[hh, :, blk * t:(blk + 1) * t].astype(BF16)
                vb_ref[hh, blk] = vt_ref[hh, :, blk * t:(blk + 1) * t].astype(BF16)

    suffix_mat = sm_ref[...]
    qs = [(q_ref[0, :, hh * D_B:(hh + 1) * D_B] * Q_SCALE).astype(BF16) for hh in heads]

    def form_scores(blk, slot):
        for hh in heads:
            z_ref[slot, hh] = _dot(qs[hh], kb_ref[hh, blk])

    def add_values(blk):
        for hh in heads:
            acc_ref[hh] += _dot_nt(w_ref[hh], vb_ref[hh, blk])

    halves = (slice(0, t // 2), slice(t // 2, t))

    def form_weights(z2s, mask):
        tiles = [(hh, rs) for hh in heads for rs in halves]
        masks = None if mask is None else [mask[rs] for _, rs in tiles]
        ws, carries = _stick_blocks([z2s[hh][rs] for hh, rs in tiles], masks,
                                    [carry_ref[hh, rs] for hh, rs in tiles], suffix_mat)
        for (hh, rs), w, carry in zip(tiles, ws, carries):
            w_ref[hh, rs] = w.astype(BF16)
            carry_ref[hh, rs] = carry

    acc_ref[...] = jnp.zeros_like(acc_ref)
    carry_ref[...] = jnp.zeros_like(carry_ref)
    form_scores(qi, 0)
    form_scores(jnp.maximum(qi - 1, 0), 1)
    form_weights([z_ref[0, hh] for hh in heads], _lower_tri(t, strict=True))

    def body(i, carry):
        blk = qi - 1 - i
        z2s = [z_ref[(i + 1) % 2, hh] for hh in heads]
        add_values(blk + 1)
        form_scores(jnp.maximum(blk - 1, 0), i % 2)
        form_weights(z2s, None)
        return carry
    lax.fori_loop(0, qi, body, 0)
    add_values(0)
    out = jnp.concatenate([acc_ref[hh] for hh in heads], axis=-1)
    y_ref[0] = (out * _silu(g_ref[0])).astype(BF16)


def attention_prompt(proj, kt_all, vt_all, layer):
    bsz, seq, _ = proj.shape
    t = min(seq, ATT_BLOCK)
    nq = seq // t
    lanes = LANES
    per = W_GROUP // lanes
    hp = lanes // D_B
    kern = functools.partial(_attn_prompt_kernel, t=t, nblk=nq)
    t_spec = pl.BlockSpec((None, None, hp, D_B, seq), lambda b, p, i: (layer, b, p, 0, 0))
    return pl.pallas_call(
        kern,
        grid=(bsz, per, nq),
        in_specs=[pl.BlockSpec((1, t, lanes), lambda b, p, i: (b, i, G_Q * per + p)),
                  t_spec, t_spec,
                  pl.BlockSpec((1, t, lanes), lambda b, p, i: (b, i, G_GB * per + p)),
                  pl.BlockSpec((t, t), lambda b, p, i: (0, 0))],
        out_specs=pl.BlockSpec((1, t, lanes), lambda b, p, i: (b, i, p)),
        out_shape=jax.ShapeDtypeStruct((bsz, seq, W_GROUP), BF16),
        scratch_shapes=[pltpu.VMEM((hp, nq, D_B, t), BF16),
                        pltpu.VMEM((hp, nq, D_B, t), BF16),
                        pltpu.VMEM((2, hp, t, t), F32),
                        pltpu.VMEM((hp, t, t), BF16),
                        pltpu.VMEM((hp, t, D_B), F32),
                        pltpu.VMEM((hp, t, 1), F32)],
        compiler_params=_cparams(("arbitrary", "arbitrary", "arbitrary")),
        name="attention_prompt",
    )(proj, kt_all, vt_all, proj, _suffix_matrix(t))


CACHE_BLOCK = 4096


def _attn_sample_kernel(q_ref, kn_ref, vn_ref, g_ref, kc_ref, vc_ref, sm_ref, smn_ref, _, __,
                        y_ref, knew_ref, vnew_ref, qb_ref, acc_ref, carry_ref, *, lq, nkb):
    j = pl.program_id(1)
    rows = H_B * lq
    heads = range(H_B)
    hl = lambda h: slice(h * D_B, (h + 1) * D_B)

    @pl.when(j == 0)
    def _():
        kn, vn = kn_ref[0], vn_ref[0]
        q2 = (q_ref[0] * Q_SCALE).astype(BF16)
        knb, vnb = kn.astype(BF16), vn.astype(BF16)
        for h in heads:
            qb_ref[h] = q2[:, hl(h)]
            knew_ref[:, h, :] = kn[:, hl(h)]
            vnew_ref[:, h, :] = vn[:, hl(h)]
        z2 = jnp.concatenate([_dot_nt(q2[:, hl(h)], knb[:, hl(h)]) for h in heads], axis=0)
        mask = _iota((rows, lq), 1) < _iota((rows, lq), 0) % lq
        (w,), (carry,) = _stick_blocks([z2], mask, [jnp.zeros((rows, 1), F32)], smn_ref[...])
        wb = w.astype(BF16)
        for h in heads:
            acc_ref[h] = _dot(wb[h * lq:(h + 1) * lq], vnb[:, hl(h)])
        carry_ref[...] = carry

    suffix_mat = sm_ref[...]
    subs = [slice(s * ATT_BLOCK, (s + 1) * ATT_BLOCK) for s in reversed(range(CACHE_BLOCK // ATT_BLOCK))]
    z2s = [jnp.concatenate([_dot(qb_ref[h], kc_ref[h, :, ks].astype(BF16)) for h in heads], axis=0)
           for ks in subs]
    zero = jnp.zeros((rows, 1), F32)
    ws, totals = _stick_blocks(z2s, None, [zero] * len(subs), suffix_mat)
    carry = carry_ref[...]
    accs = [acc_ref[h] for h in heads]
    for ks, w, total in zip(subs, ws, totals):
        wb = (w * jnp.exp2(-carry)).astype(BF16)
        accs = [accs[h] + _dot_nt(wb[h * lq:(h + 1) * lq], vc_ref[h, :, ks].astype(BF16)) for h in heads]
        carry = carry + total
    for h in heads:
        acc_ref[h] = accs[h]
    carry_ref[...] = carry

    @pl.when(j == nkb - 1)
    def _():
        out = jnp.concatenate(accs, axis=-1)
        y_ref[0] = (out * _silu(g_ref[0])).astype(BF16)


def attention_sample(proj, ktc, vtc, knew_all, vnew_all, layer):
    bsz, lq, _ = proj.shape
    past = ktc.shape[-1]
    nkb = past // CACHE_BLOCK
    kern = functools.partial(_attn_sample_kernel, lq=lq, nkb=nkb)
    col = lambda c: pl.BlockSpec((1, lq, W_GROUP), lambda b, j: (b, 0, c))
    cache = pl.BlockSpec((None, None, H_B, D_B, CACHE_BLOCK), lambda b, j: (layer, b, 0, 0, nkb - 1 - j))
    any_spec = pl.BlockSpec(memory_space=pl.ANY)
    new_spec = pl.BlockSpec((None, None, lq, H_B, D_B), lambda b, j: (layer, b, 0, 0, 0))
    new_shape = jax.ShapeDtypeStruct(knew_all.shape, F32)
    return pl.pallas_call(
        kern,
        grid=(bsz, nkb),
        in_specs=[col(G_Q), col(G_K), col(G_V), col(G_GB), cache, cache,
                  pl.BlockSpec((ATT_BLOCK, ATT_BLOCK), lambda b, j: (0, 0)),
                  pl.BlockSpec((lq, lq), lambda b, j: (0, 0)),
                  any_spec, any_spec],
        out_specs=[pl.BlockSpec((1, lq, W_GROUP), lambda b, j: (b, 0, 0)), new_spec, new_spec],
        out_shape=[jax.ShapeDtypeStruct((bsz, lq, W_GROUP), BF16), new_shape, new_shape],
        input_output_aliases={8: 1, 9: 2},
        scratch_shapes=[pltpu.VMEM((H_B, lq, D_B), BF16),
                        pltpu.VMEM((H_B, lq, D_B), F32),
                        pltpu.VMEM((H_B * lq, 1), F32)],
        compiler_params=_cparams(("arbitrary", "arbitrary")),
        name="attention_sample",
    )(proj, proj, proj, proj, ktc, vtc, _suffix_matrix(ATT_BLOCK), _suffix_matrix(lq), knew_all, vnew_all)


SUBLANES = 8


def _repack_kernel(starts_ref, w_ref, dt_ref, if_ref, o_ref, small_ref):
    o_ref[...] = w_ref[0].astype(BF16)

    @pl.when(pl.program_id(1) == 0)
    def _():
        rows = jnp.concatenate([dt_ref[0], if_ref[0]], axis=0)
        pad = jnp.zeros((SMALL_W - rows.shape[0], D_MODEL), F32)
        small_ref[...] = jnp.concatenate([rows, pad], axis=0).astype(BF16)


def _repack_w_in(w_in):
    layers = w_in.shape[0]
    wt = jnp.swapaxes(w_in, -1, -2)
    o_dt = W_GROUP + CONV_DIM_A
    o_q = o_dt + H_A
    o_k, o_v, o_gb, o_xc = o_q + W_GROUP, o_q + 2 * W_GROUP, o_q + 3 * W_GROUP, o_q + 4 * W_GROUP
    o_i = o_xc + 2 * W_GROUP
    o_ad = o_i + 2 * H_C
    starts = [0, W_GROUP, 2 * W_GROUP, o_q, o_gb, o_xc, o_xc + W_GROUP, o_ad, o_ad + W_GROUP, o_ad + 2 * W_GROUP,
              o_k, o_v]
    assert all(s % SUBLANES == 0 for s in starts + [o_dt, o_i]) and o_q - o_dt == o_ad - o_i == SUBLANES
    tiles = jnp.asarray([s // SUBLANES for s in starts], jnp.int32)
    rows8 = lambda start: pl.BlockSpec((pl.Element(1), pl.Element(SUBLANES), pl.Element(D_MODEL)),
                                       lambda l, g, tiles: (l, start, 0))
    return pl.pallas_call(
        _repack_kernel,
        grid_spec=pltpu.PrefetchScalarGridSpec(
            num_scalar_prefetch=1, grid=(layers, N_MAIN_GROUPS),
            in_specs=[pl.BlockSpec((pl.Element(1), pl.Element(W_GROUP), pl.Element(D_MODEL)),
                                   lambda l, g, tiles: (l, tiles[g] * SUBLANES, 0)),
                      rows8(o_dt), rows8(o_i)],
            out_specs=[pl.BlockSpec((None, W_GROUP, D_MODEL), lambda l, g, tiles: (l, g, 0)),
                       pl.BlockSpec((None, SMALL_W, D_MODEL), lambda l, g, tiles: (l, 0, 0))]),
        out_shape=[jax.ShapeDtypeStruct((layers, N_MAIN_GROUPS * W_GROUP, D_MODEL), BF16),
                   jax.ShapeDtypeStruct((layers, SMALL_W, D_MODEL), BF16)],
        compiler_params=_cparams(("arbitrary", "arbitrary")),
        name="repack_w_in",
    )(tiles, wt, wt, wt)


STACKED_WEIGHTS = ("w_main", "w_small", "w_out")


def _mixer_layer(x, mod, layer, kv_t, kv_cache_t, conv_a_buf, ssm0_all, conv_c_buf, mc0_all, mn0, mm0, conv_d_buf,
                 lw):
    if kv_cache_t is None:
        proj, small, kt_all, vt_all = in_projection(x, mod, lw["w_main"], lw["w_small"], layer, kv_t)
        y_b = attention_prompt(proj, kt_all, vt_all, layer)
    else:
        proj, small = in_projection(x, mod, lw["w_main"], lw["w_small"], layer)
        y_b, kt_all, vt_all = attention_sample(proj, kv_cache_t[0], kv_cache_t[1], kv_t[0], kv_t[1], layer)
    y_a, conv_a_new, ssm_new = mixer_ssd(proj, small, conv_a_buf, ssm0_all, layer, lw["conv_a_w"], lw["conv_a_b"],
                                         lw["dt_bias"], lw["a_log"], lw["d_skip"], lw["norm_a_w"])
    (y_c, conv_c_new, mc_new, mn_new, mm_new), (y_d, conv_d_new) = mixers_mlstm_conformer(
        proj, small,
        (conv_c_buf, mc0_all, layer, mn0, mm0, lw["conv_c_w"], lw["conv_c_b"], lw["wq_c"], lw["wk_c"], lw["wv_c"],
         lw["ig_bias"], lw["fg_bias"], lw["norm_c_w"], lw["skip_c"]),
        (conv_d_buf, lw["conv_d_w"], lw["conv_d_b"], lw["ln_d_g"], lw["ln_d_b"]))
    x_new = out_projection((y_a, y_b, y_c, y_d), x, mod, lw["w_out"], lw["ln_g"], lw["ln_b"], layer)
    return x_new, (kt_all, vt_all), (conv_a_new, ssm_new, conv_c_new, mc_new, mn_new, mm_new, conv_d_new)


def _run_trunk(x, mods, cache_k, cache_v, st_conv_a, st_ssm, st_conv_c, st_mc, st_mn, st_mm, st_conv_d, weights):
    bsz, seq, _ = x.shape
    outs = [[] for _ in range(7)]
    if cache_k is None:
        kv_cache_t = None
        kv_t = (jnp.zeros((DEPTH, bsz, H_B, D_B, seq), F32),) * 2
    else:
        kv_cache_t = (jnp.transpose(cache_k, (0, 1, 3, 4, 2)), jnp.transpose(cache_v, (0, 1, 3, 4, 2)))
        kv_t = (jnp.zeros((DEPTH, bsz, seq, H_B, D_B), F32),) * 2
    for l in range(DEPTH):
        lw = {name: (w if name in STACKED_WEIGHTS else w[l]) for name, w in weights.items()}
        x, kv_t, new = _mixer_layer(x, mods[l], l, kv_t, kv_cache_t, st_conv_a[l], st_ssm, st_conv_c[l],
                                    st_mc, st_mn[l], st_mm[l], st_conv_d[l], lw)
        for o, t in zip(outs, new):
            o.append(t)
    if cache_k is None:
        kv_new = [jnp.transpose(t, (0, 1, 4, 2, 3)) for t in kv_t]
    else:
        kv_new = list(kv_t)
    return x, kv_new + [jnp.stack(o) for o in outs]


def kernel(x_prompt, x_sample, cache_k, cache_v, state_conv_a, state_ssm, state_conv_c, state_mlstm_c,
           state_mlstm_n, state_mlstm_m, state_conv_d, c_prompt, c_sample, w_mod, b_mod, w_in, conv_a_w,
           conv_a_b, dt_bias, a_log, d_skip, norm_a_w, conv_c_w, conv_c_b, wq_c, wk_c, wv_c, ig_bias, fg_bias,
           norm_c_w, skip_c, conv_d_w, conv_d_b, ln_d_g, ln_d_b, w_out, ln_g, ln_b):
    batch, dec_batch = x_prompt.shape[0], x_sample.shape[0]
    w_main, w_small = _repack_w_in(w_in)
    weights = dict(w_main=w_main, w_small=w_small, conv_a_w=conv_a_w, conv_a_b=conv_a_b, dt_bias=dt_bias,
                   a_log=a_log, d_skip=d_skip, norm_a_w=norm_a_w, conv_c_w=conv_c_w, conv_c_b=conv_c_b,
                   wq_c=wq_c, wk_c=wk_c, wv_c=wv_c, ig_bias=ig_bias, fg_bias=fg_bias, norm_c_w=norm_c_w,
                   skip_c=skip_c, conv_d_w=conv_d_w, conv_d_b=conv_d_b, ln_d_g=ln_d_g, ln_d_b=ln_d_b,
                   w_out=w_out.reshape(DEPTH, 4, W_GROUP, D_MODEL).astype(BF16), ln_g=ln_g, ln_b=ln_b)

    rows = batch + dec_batch
    rows_pad = -(-rows // 8) * 8
    c_all = jnp.concatenate([c_prompt, c_sample, jnp.zeros((rows_pad - rows, D_MODEL), F32)], axis=0)
    mod_all = modulation(c_all, w_mod, b_mod)
    mods_p = mod_all[:, :batch].reshape(DEPTH, batch, 1, 3 * D_MODEL)
    mods_s = mod_all[:, batch:rows].reshape(DEPTH, dec_batch, 1, 3 * D_MODEL)

    def zeros(*shape):
        return jnp.zeros((DEPTH, batch) + shape, F32)

    y_prompt, sp = _run_trunk(x_prompt, mods_p, None, None,
                              zeros(K_A - 1, CONV_DIM_A), zeros(H_A, P_A, N_A), zeros(K_C - 1, W_GROUP),
                              zeros(H_C, DH_C, DH_C), zeros(H_C, DH_C), zeros(H_C), zeros(K_D - 1, W_GROUP),
                              weights)
    y_sample, ss = _run_trunk(x_sample, mods_s, cache_k, cache_v, state_conv_a, state_ssm, state_conv_c,
                              state_mlstm_c, state_mlstm_n, state_mlstm_m, state_conv_d, weights)
    return (y_prompt, y_sample, *sp, *ss)
```
